```python
import math
import jax, jax.numpy as jnp
from jax import lax
import numpy as np

D_MODEL = 1024
BATCH = 4
SEQ = 4096
DEPTH = 2
DEC_BATCH = 128
DEC_SEQ = 1
PAST_LEN = 2048
PAGE_SIZE = 128

N_MIXERS = 2
N_ATTN_LAYERS = (DEPTH + N_MIXERS - 1) // N_MIXERS
N_CONV_LAYERS = DEPTH // N_MIXERS
DIL_GROUPS = ((128, 1), (512, 4), (2048, 16))
N_GROUPS = len(DIL_GROUPS)
HEADS_PER_GROUP = 8
HEAD_DIM = 128
ATTN_WIDTH = HEADS_PER_GROUP * HEAD_DIM
N_BUCKETS = 32
MAX_DISTANCE = 2048
CONV_WIDTH = 31
FFN_HIDDEN = -(-8 * D_MODEL // (3 * 256)) * 256
EPS = 1e-6
NEG_INF = -1e30

kernel_name = 'hybrid_dilated_attn_conformer_decoder_step'


def rms_norm(x, g):
    xf = x.astype(jnp.float32)
    y = xf * lax.rsqrt(jnp.mean(xf * xf, axis=-1, keepdims=True) + EPS)
    return y.astype(x.dtype) * g


def layer_norm(x, g, b):
    xf = x.astype(jnp.float32)
    mu = jnp.mean(xf, axis=-1, keepdims=True)
    var = jnp.mean(jnp.square(xf - mu), axis=-1, keepdims=True)
    return ((xf - mu) * lax.rsqrt(var + EPS)).astype(x.dtype) * g + b


def t5_bucket(dist):
    max_exact = N_BUCKETS // 2
    n = jnp.maximum(dist, 1).astype(jnp.float32)
    large = max_exact + (jnp.log(n / max_exact) / math.log(MAX_DISTANCE / max_exact)
                         * (N_BUCKETS - max_exact)).astype(jnp.int32)
    large = jnp.minimum(large, N_BUCKETS - 1)
    return jnp.where(dist < max_exact, dist, large)


def dilated_attn_prompt(q, k, v, bias_g, window, dilation):
    batch, seq, heads, dh = q.shape
    span = window // dilation
    unit = span * dilation
    padded = -(-seq // unit) * unit
    nb = padded // unit

    def to_blocks(a):
        a = jnp.pad(a, ((0, 0), (0, padded - seq), (0, 0), (0, 0)))
        return a.reshape(batch, nb, span, dilation, heads, dh)

    def with_prev(a):
        prev = jnp.pad(a[:, :-1], ((0, 0), (1, 0), (0, 0), (0, 0), (0, 0), (0, 0)))
        return jnp.concatenate([prev, a], axis=2)

    qb = to_blocks(q)
    kband = with_prev(to_blocks(k))
    vband = with_prev(to_blocks(v))
    logits = jnp.einsum('bnqrhd,bnkrhd->bnqrhk', qb, kband).astype(jnp.float32) * (HEAD_DIM ** -0.5)
    qi = jnp.arange(span)[:, None]
    ki = jnp.arange(2 * span)[None, :]
    delta = qi + span - ki
    in_band = (delta >= 0) & (delta <= span)
    bucket = t5_bucket(jnp.clip(delta, 0, span) * dilation)
    bias = jnp.transpose(bias_g[bucket], (0, 2, 1)).astype(jnp.float32)
    not_before_start = (jnp.arange(nb)[:, None, None] > 0) | (ki >= span)[None]
    valid = in_band[None] & not_before_start
    logits = logits + bias[None, None, :, None]
    logits = jnp.where(valid[None, :, :, None, None, :], logits, NEG_INF)
    lse = jax.nn.logsumexp(logits, axis=-1)
    p = jnp.exp(logits - lse[..., None]).astype(v.dtype)
    o = jnp.einsum('bnqrhk,bnkrhd->bnqrhd', p, vband)
    return o.reshape(batch, padded, heads, dh)[:, :seq], lse.reshape(batch, padded, heads)[:, :seq]


def dilated_attn_sample(q, k_all, v_all, bias_g, window, dilation, n_buf):
    n_new = q.shape[1]
    span = window // dilation
    j = jnp.arange(span + 1)
    idx = n_buf + jnp.arange(n_new)[:, None] - j[None, :] * dilation
    valid = idx >= 0
    idx = jnp.maximum(idx, 0)
    kg = k_all[:, idx]
    vg = v_all[:, idx]
    logits = jnp.einsum('bthd,btjhd->bthj', q, kg).astype(jnp.float32) * (HEAD_DIM ** -0.5)
    bias = jnp.transpose(bias_g[t5_bucket(j * dilation)], (1, 0)).astype(jnp.float32)
    logits = logits + bias[None, None]
    logits = jnp.where(valid[None, :, None, :], logits, NEG_INF)
    lse = jax.nn.logsumexp(logits, axis=-1)
    p = jnp.exp(logits - lse[..., None]).astype(v_all.dtype)
    o = jnp.einsum('bthj,btjhd->bthd', p, vg)
    return o, lse


def dilated_mixture(h, w_qkv, w_o, rel_bias, bufs):
    batch, n_tok, _ = h.shape
    qkv = (h @ w_qkv).reshape(batch, n_tok, N_GROUPS, 3, HEADS_PER_GROUP, HEAD_DIM)
    outs, lses, new_state = [], [], []
    for g, (window, dilation) in enumerate(DIL_GROUPS):
        q, k, v = qkv[:, :, g, 0], qkv[:, :, g, 1], qkv[:, :, g, 2]
        bias_g = rel_bias[:, g * HEADS_PER_GROUP:(g + 1) * HEADS_PER_GROUP]
        if bufs is None:
            o, lse = dilated_attn_prompt(q, k, v, bias_g, window, dilation)
            keep = min(window, n_tok)
            new_state.append(jnp.stack([k[:, n_tok - keep:], v[:, n_tok - keep:]], axis=1))
        else:
            buf = bufs[g]
            n_buf = buf.shape[2]
            k_all = jnp.concatenate([buf[:, 0], k], axis=1)
            v_all = jnp.concatenate([buf[:, 1], v], axis=1)
            o, lse = dilated_attn_sample(q, k_all, v_all, bias_g, window, dilation, n_buf)
            new_state.append(jnp.stack([k, v], axis=1))
        outs.append(o)
        lses.append(lse)
    w = jax.nn.softmax(jnp.stack(lses, axis=2), axis=2).astype(h.dtype)
    o = jnp.einsum('btghd,btgh->bthd', jnp.stack(outs, axis=2), w)
    return o.reshape(batch, n_tok, ATTN_WIDTH) @ w_o, new_state


def conformer_conv(h, w_pw1, b_pw1, w_dw, b_dw, ln_g, ln_b, w_pw2, b_pw2, buf):
    batch, n_tok, d = h.shape
    a, gate = jnp.split(h @ w_pw1 + b_pw1, 2, axis=-1)
    u = a * jax.nn.sigmoid(gate)
    prefix = jnp.zeros((batch, CONV_WIDTH - 1, d), u.dtype) if buf is None else buf.astype(u.dtype)
    ext = jnp.concatenate([prefix, u], axis=1)
    z = lax.conv_general_dilated(ext, w_dw[:, None, :].astype(u.dtype), window_strides=(1,), padding='VALID',
                                 dimension_numbers=('NWC', 'WIO', 'NWC'), feature_group_count=d) + b_dw
    z = jax.nn.silu(layer_norm(z, ln_g, ln_b))
    return z @ w_pw2 + b_pw2, ext[:, -(CONV_WIDTH - 1):]


def swiglu(h, w_gate, w_up, w_down):
    return (jax.nn.silu(h @ w_gate) * (h @ w_up)) @ w_down


def trunk(x, c, attn_bufs, conv_bufs, w_mod, b_mod, g_mix, g_ffn, g_final, w_qkv, w_o, rel_bias,
          w_pw1, b_pw1, w_dw, b_dw, ln_g, ln_b, w_pw2, b_pw2, w_gate, w_up, w_down):
    new_attn = [[] for _ in range(N_GROUPS)]
    new_conv = []
    for i in range(DEPTH):
        mod = jax.nn.silu(c) @ w_mod[i] + b_mod[i]
        sh1, sc1, g1, sh2, sc2, g2 = jnp.split(mod[:, None, :], 6, axis=-1)
        h = rms_norm(x, g_mix[i]) * (1 + sc1) + sh1
        if i % N_MIXERS == 0:
            a = i // N_MIXERS
            bufs = None if attn_bufs is None else tuple(b[a] for b in attn_bufs)
            out, st = dilated_mixture(h, w_qkv[a], w_o[a], rel_bias, bufs)
            for g in range(N_GROUPS):
                new_attn[g].append(st[g])
        else:
            b = i // N_MIXERS
            buf = None if conv_bufs is None else conv_bufs[b]
            out, st = conformer_conv(h, w_pw1[b], b_pw1[b], w_dw[b], b_dw[b], ln_g[b], ln_b[b],
                                     w_pw2[b], b_pw2[b], buf)
            new_conv.append(st)
        x = x + g1 * out
        h = rms_norm(x, g_ffn[i]) * (1 + sc2) + sh2
        x = x + g2 * swiglu(h, w_gate[i], w_up[i], w_down[i])
    y = rms_norm(x, g_final)
    return y, [jnp.stack(s, axis=0) for s in new_attn], jnp.stack(new_conv, axis=0)


def setup_inputs(seed: int = 0) -> dict:
    key = jax.random.key(seed)
    ks = jax.random.split(key, 32)
    d = D_MODEL
    n_a, n_b = N_ATTN_LAYERS, N_CONV_LAYERS
    buf_len = [min(w, PAST_LEN) for w, _ in DIL_GROUPS]

    def nrm(k, shape, scale=1.0):
        return jax.random.normal(k, shape, jnp.float32) * scale

    return {
        'x_prompt': nrm(ks[0], (BATCH, SEQ, d)),
        'x_sample': nrm(ks[1], (DEC_BATCH, DEC_SEQ, d)),
        'cache_kv_w128': nrm(ks[2], (n_a, DEC_BATCH, 2, buf_len[0], HEADS_PER_GROUP, HEAD_DIM)),
        'cache_kv_w512': nrm(ks[3], (n_a, DEC_BATCH, 2, buf_len[1], HEADS_PER_GROUP, HEAD_DIM)),
        'cache_kv_w2048': nrm(ks[4], (n_a, DEC_BATCH, 2, buf_len[2], HEADS_PER_GROUP, HEAD_DIM)),
        'state_conv': nrm(ks[5], (n_b, DEC_BATCH, CONV_WIDTH - 1, d), 0.5),
        'c_prompt': nrm(ks[6], (BATCH, d)),
        'c_sample': nrm(ks[7], (DEC_BATCH, d)),
        'w_mod': nrm(ks[8], (DEPTH, d, 6 * d), 0.5 * d ** -0.5),
        'b_mod': nrm(ks[9], (DEPTH, 6 * d), 0.02),
        'g_mix': 1.0 + nrm(ks[10], (DEPTH, d), 0.02),
        'g_ffn': 1.0 + nrm(ks[11], (DEPTH, d), 0.02),
        'g_final': 1.0 + nrm(ks[12], (d,), 0.02),
        'w_qkv': nrm(ks[13], (n_a, d, N_GROUPS * 3 * ATTN_WIDTH), d ** -0.5),
        'w_o': nrm(ks[14], (n_a, ATTN_WIDTH, d), ATTN_WIDTH ** -0.5),
        'rel_bias': nrm(ks[15], (N_BUCKETS, N_GROUPS * HEADS_PER_GROUP), 0.2),
        'w_pw1': nrm(ks[16], (n_b, d, 2 * d), d ** -0.5),
        'b_pw1': nrm(ks[17], (n_b, 2 * d), 0.02),
        'w_dw': nrm(ks[18], (n_b, CONV_WIDTH, d), CONV_WIDTH ** -0.5),
        'b_dw': nrm(ks[19], (n_b, d), 0.02),
        'ln_g': 1.0 + nrm(ks[20], (n_b, d), 0.02),
        'ln_b': nrm(ks[21], (n_b, d), 0.02),
        'w_pw2': nrm(ks[22], (n_b, d, d), d ** -0.5),
        'b_pw2': nrm(ks[23], (n_b, d), 0.02),
        'w_gate': nrm(ks[24], (DEPTH, d, FFN_HIDDEN), d ** -0.5),
        'w_up': nrm(ks[25], (DEPTH, d, FFN_HIDDEN), d ** -0.5),
        'w_down': nrm(ks[26], (DEPTH, FFN_HIDDEN, d), FFN_HIDDEN ** -0.5),
    }


def reference(x_prompt, x_sample, cache_kv_w128, cache_kv_w512, cache_kv_w2048, state_conv,
              c_prompt, c_sample, w_mod, b_mod, g_mix, g_ffn, g_final, w_qkv, w_o, rel_bias,
              w_pw1, b_pw1, w_dw, b_dw, ln_g, ln_b, w_pw2, b_pw2, w_gate, w_up, w_down):
    y_prompt, attn_p, conv_p = trunk(x_prompt, c_prompt, None, None, w_mod, b_mod, g_mix, g_ffn, g_final,
                                     w_qkv, w_o, rel_bias, w_pw1, b_pw1, w_dw, b_dw, ln_g, ln_b,
                                     w_pw2, b_pw2, w_gate, w_up, w_down)
    y_sample, attn_s, conv_s = trunk(x_sample, c_sample, (cache_kv_w128, cache_kv_w512, cache_kv_w2048),
                                     state_conv, w_mod, b_mod, g_mix, g_ffn, g_final,
                                     w_qkv, w_o, rel_bias, w_pw1, b_pw1, w_dw, b_dw, ln_g, ln_b,
                                     w_pw2, b_pw2, w_gate, w_up, w_down)
    kv128_p, kv512_p, kv2048_p = attn_p
    kv128_s, kv512_s, kv2048_s = attn_s
    return (y_prompt, y_sample, kv128_p, kv512_p, kv2048_p, conv_p, kv128_s, kv512_s, kv2048_s, conv_s)
```

```python
import functools
import math

import numpy as np
import jax
import jax.numpy as jnp
from jax import lax
from jax.experimental import pallas as pl
from jax.experimental.pallas import tpu as pltpu

D_MODEL = 1024
DIL_GROUPS = ((128, 1), (512, 4), (2048, 16))
N_GROUPS = len(DIL_GROUPS)
HEADS = 8
HEAD_DIM = 128
SPAN = 128
N_BUCKETS = 32
MAX_DISTANCE = 2048
CONV_WIDTH = 31
FFN_HIDDEN = 2816
EPS = 1e-6
NEG_INF = -1e30
SCALE = HEAD_DIM ** -0.5

F32 = jnp.float32
BF16 = jnp.bfloat16

VMEM_LIMIT_BYTES = 56 * 1024 * 1024


def _params(n_axes, vmem=VMEM_LIMIT_BYTES):
    return pltpu.CompilerParams(dimension_semantics=("arbitrary",) * n_axes, vmem_limit_bytes=vmem)


def _t5_bucket_np(dist):
    max_exact = N_BUCKETS // 2
    n = np.maximum(dist, 1).astype(np.float32)
    large = max_exact + (np.log(n / np.float32(max_exact)) / np.float32(math.log(MAX_DISTANCE / max_exact))
                         * np.float32(N_BUCKETS - max_exact)).astype(np.int32)
    large = np.minimum(large, N_BUCKETS - 1)
    return np.where(dist < max_exact, dist, large).astype(np.int32)


def _bucket_tables():
    qi = np.arange(SPAN)[:, None]
    ki = np.arange(2 * SPAN)[None, :]
    delta = qi + SPAN - ki
    in_band = (delta >= 0) & (delta <= SPAN)
    prompt, sample = [], []
    for _, dilation in DIL_GROUPS:
        b = _t5_bucket_np(np.clip(delta, 0, SPAN) * dilation)
        prompt.append(np.where(in_band, b, -1))
        j = SPAN - np.arange(136)
        sample.append(_t5_bucket_np(np.maximum(j, 0) * dilation)[:, None])
    return np.stack(prompt).astype(np.int32), np.stack(sample).astype(np.int32)


def _bias_kernel(rb_smem, rb_ref, bp_ref, bs_ref, op_ref, os_ref):
    g = pl.program_id(0)
    bp = bp_ref[...]
    for h in range(HEADS):
        acc = jnp.full(bp.shape, NEG_INF, F32)
        for b in range(N_BUCKETS):
            acc = jnp.where(bp == b, rb_smem[b, g * HEADS + h], acc)
        op_ref[h] = acc
    bs = bs_ref[...]
    acc = jnp.zeros((bs.shape[0], rb_ref.shape[1]), F32)
    for b in range(N_BUCKETS):
        acc = jnp.where(bs == b, rb_ref[b:b + 1, :], acc)
    os_ref[...] = acc


def _bias_tables(rel_bias):
    bp, bs = _bucket_tables()
    n_cols = rel_bias.shape[1]
    return pl.pallas_call(
        _bias_kernel,
        grid=(N_GROUPS,),
        in_specs=[
            pl.BlockSpec(memory_space=pltpu.SMEM),
            pl.BlockSpec((N_BUCKETS, n_cols), lambda g: (0, 0)),
            pl.BlockSpec((None, SPAN, 2 * SPAN), lambda g: (g, 0, 0)),
            pl.BlockSpec((None, 136, 1), lambda g: (g, 0, 0)),
        ],
        out_specs=[
            pl.BlockSpec((None, HEADS, SPAN, 2 * SPAN), lambda g: (g, 0, 0, 0)),
            pl.BlockSpec((None, 136, n_cols), lambda g: (g, 0, 0)),
        ],
        out_shape=[
            jax.ShapeDtypeStruct((N_GROUPS, HEADS, SPAN, 2 * SPAN), F32),
            jax.ShapeDtypeStruct((N_GROUPS, 136, n_cols), F32),
        ],
        compiler_params=_params(1),
        name="bias_tables",
    )(rel_bias, rel_bias, jnp.asarray(bp), jnp.asarray(bs))


def _mod_kernel(c_ref, w_ref, b_ref, o_ref):
    c = c_ref[...]
    a = (c * jax.nn.sigmoid(c)).astype(BF16)
    o_ref[...] = jnp.dot(a, w_ref[...].astype(BF16), preferred_element_type=F32) + b_ref[...]


def _modulation(c_all, w_mod, b_mod):
    depth, d, n = w_mod.shape
    rows = c_all.shape[0]
    tn = 1536
    return pl.pallas_call(
        _mod_kernel,
        grid=(depth, n // tn),
        in_specs=[
            pl.BlockSpec((rows, d), lambda l, j: (0, 0)),
            pl.BlockSpec((None, d, tn), lambda l, j: (l, 0, j)),
            pl.BlockSpec((None, 1, tn), lambda l, j: (l, 0, j)),
        ],
        out_specs=pl.BlockSpec((None, rows, tn), lambda l, j: (l, 0, j)),
        out_shape=jax.ShapeDtypeStruct((depth, rows, n), F32),
        compiler_params=_params(2),
        name="modulation",
    )(c_all, w_mod, b_mod.reshape(depth, 1, n))


def _normed(x_ref, g_ref, sc_ref, sh_ref):
    x = x_ref[...]
    y = x * lax.rsqrt(jnp.mean(x * x, axis=-1, keepdims=True) + EPS)
    return ((y * g_ref[...]) * (1.0 + sc_ref[...]) + sh_ref[...]).astype(BF16)


def _norm_mm_kernel(x_ref, g_ref, sc_ref, sh_ref, w_ref, o_ref, h_ref):
    @pl.when(pl.program_id(1) == 0)
    def _():
        h_ref[...] = _normed(x_ref, g_ref, sc_ref, sh_ref)
    o_ref[...] = jnp.dot(h_ref[...], w_ref[...], preferred_element_type=F32).astype(o_ref.dtype)


def _norm_glu_kernel(x_ref, g_ref, sc_ref, sh_ref, wa_ref, wg_ref, ba_ref, bg_ref, o_ref, h_ref):
    @pl.when(pl.program_id(1) == 0)
    def _():
        h_ref[...] = _normed(x_ref, g_ref, sc_ref, sh_ref)
    h = h_ref[...]
    a = jnp.dot(h, wa_ref[...], preferred_element_type=F32) + ba_ref[...]
    gate = jnp.dot(h, wg_ref[...], preferred_element_type=F32) + bg_ref[...]
    o_ref[...] = a * jax.nn.sigmoid(gate)


def _norm_swiglu_kernel(x_ref, g_ref, sc_ref, sh_ref, wg_ref, wu_ref, o_ref, h_ref):
    @pl.when(pl.program_id(1) == 0)
    def _():
        h_ref[...] = _normed(x_ref, g_ref, sc_ref, sh_ref)
    h = h_ref[...]
    gate = jnp.dot(h, wg_ref[...], preferred_element_type=F32)
    up = jnp.dot(h, wu_ref[...], preferred_element_type=F32)
    o_ref[...] = ((gate * jax.nn.sigmoid(gate)) * up).astype(o_ref.dtype)


class _Mod:
    def __init__(self, mod_all, layer, per_token, rows_per_seq):
        self.layer = layer
        self.per_token = per_token
        self.rows_per_seq = rows_per_seq
        depth, rows, n = mod_all.shape
        self.n_sample = rows - 8
        self.arr = mod_all if per_token else mod_all.reshape(depth, rows, 1, n)

    def spec(self, chunk, tm, row_of):
        layer = self.layer
        if self.per_token:
            return pl.BlockSpec((None, tm, D_MODEL), lambda *i: (layer, row_of(*i) // tm, chunk))
        first, per = self.n_sample, self.rows_per_seq
        return pl.BlockSpec((None, None, 1, D_MODEL), lambda *i: (layer, first + row_of(*i) // per, 0, chunk))


def _call_norm_matmul(kind, x, gain, mod, chunks, weights, biases, *, tm, tn, n_cols, out_dtype, name,
                      n_rows=None, row_of=None, out_shape=None, out_spec=None):
    t = x.shape[0] if n_rows is None else n_rows
    if row_of is None:
        row_of = lambda i, j: i * tm
    kernel = {"mm": _norm_mm_kernel, "glu": _norm_glu_kernel, "swiglu": _norm_swiglu_kernel}[kind]
    chunk_shift, chunk_scale = chunks
    in_specs = [
        pl.BlockSpec((tm, D_MODEL), lambda i, j: (row_of(i, j) // tm, 0)),
        pl.BlockSpec((1, D_MODEL), lambda i, j: (0, 0)),
        mod.spec(chunk_scale, tm, row_of),
        mod.spec(chunk_shift, tm, row_of),
    ]
    args = [x, gain.reshape(1, D_MODEL), mod.arr, mod.arr]
    for w, col0 in weights:
        in_specs.append(pl.BlockSpec((D_MODEL, tn), lambda i, j, col0=col0: (0, col0 + j)))
        args.append(w)
    for b, col0 in biases:
        in_specs.append(pl.BlockSpec((1, tn), lambda i, j, col0=col0: (0, col0 + j)))
        args.append(b)
    if out_shape is None:
        out_shape = jax.ShapeDtypeStruct((t, n_cols), out_dtype)
        out_spec = pl.BlockSpec((tm, tn), lambda i, j: (i, j))
    return pl.pallas_call(
        kernel,
        grid=(t // tm, n_cols // tn),
        in_specs=in_specs,
        out_specs=out_spec,
        out_shape=out_shape,
        scratch_shapes=[pltpu.VMEM((tm, D_MODEL), BF16)],
        compiler_params=_params(2),
        name=name,
    )(*args)


def _mm_res_kernel(*refs, has_bias, final_norm):
    a_ref, w_ref = refs[0], refs[1]
    k = 2
    b_ref = None
    if has_bias:
        b_ref = refs[k]
        k += 1
    x_ref, gate_ref = refs[k], refs[k + 1]
    k += 2
    gf_ref = None
    if final_norm:
        gf_ref = refs[k]
        k += 1
    o_ref = refs[k]
    out = jnp.dot(a_ref[...], w_ref[...], preferred_element_type=F32)
    if has_bias:
        out = out + b_ref[...]
    x = x_ref[...] + gate_ref[...] * out
    if final_norm:
        x = (x * lax.rsqrt(jnp.mean(x * x, axis=-1, keepdims=True) + EPS)) * gf_ref[...]
    o_ref[...] = x


def _matmul_residual(a, w, bias, x, mod, chunk_gate, *, tm, name, g_final=None):
    t, k = a.shape
    row_of = lambda i: i * tm
    in_specs = [
        pl.BlockSpec((tm, k), lambda i: (i, 0)),
        pl.BlockSpec((k, D_MODEL), lambda i: (0, 0)),
    ]
    args = [a, w]
    if bias is not None:
        in_specs.append(pl.BlockSpec((1, D_MODEL), lambda i: (0, 0)))
        args.append(bias.reshape(1, D_MODEL))
    in_specs += [pl.BlockSpec((tm, D_MODEL), lambda i: (i, 0)), mod.spec(chunk_gate, tm, row_of)]
    args += [x, mod.arr]
    if g_final is not None:
        in_specs.append(pl.BlockSpec((1, D_MODEL), lambda i: (0, 0)))
        args.append(g_final.reshape(1, D_MODEL))
    return pl.pallas_call(
        functools.partial(_mm_res_kernel, has_bias=bias is not None, final_norm=g_final is not None),
        grid=(t // tm,),
        in_specs=in_specs,
        out_specs=pl.BlockSpec((tm, D_MODEL), lambda i: (i, 0)),
        out_shape=jax.ShapeDtypeStruct((t, D_MODEL), F32),
        compiler_params=_params(1),
        name=name,
    )(*args)


def _attn_prompt_kernel(q_ref, kp_ref, kc_ref, vp_ref, vc_ref, bias_ref, o_ref, lse_ref):
    first = pl.program_id(2) == 0
    lane = lax.broadcasted_iota(jnp.int32, (SPAN, HEAD_DIM), 1)
    lse_all = jnp.zeros((SPAN, HEAD_DIM), F32)
    dims = (((1,), (1,)), ((), ()))
    for h in range(HEADS):
        sl = slice(h * HEAD_DIM, (h + 1) * HEAD_DIM)
        q = q_ref[:, sl]
        sp = lax.dot_general(q, kp_ref[:, sl], dims, preferred_element_type=F32) * SCALE + bias_ref[h, :, :SPAN]
        sc = lax.dot_general(q, kc_ref[:, sl], dims, preferred_element_type=F32) * SCALE + bias_ref[h, :, SPAN:]
        sp = jnp.where(first, NEG_INF, sp)
        m = jnp.maximum(jnp.max(sp, axis=-1, keepdims=True), jnp.max(sc, axis=-1, keepdims=True))
        ep = jnp.exp(sp - m)
        ec = jnp.exp(sc - m)
        l = jnp.sum(ep, axis=-1, keepdims=True) + jnp.sum(ec, axis=-1, keepdims=True)
        o = (jnp.dot(ep.astype(BF16), vp_ref[:, sl], preferred_element_type=F32)
             + jnp.dot(ec.astype(BF16), vc_ref[:, sl], preferred_element_type=F32))
        o_ref[:, sl] = o / l
        lse_all = jnp.where(lane == h, m + jnp.log(l), lse_all)
    lse_ref[...] = lse_all


def _attn_prompt(qkv, bias_p, g, batch, seq):
    _, dil = DIL_GROUPS[g]
    sub = seq // dil
    n_col = 3 * N_GROUPS
    view = qkv.reshape(batch, sub, dil * n_col * D_MODEL)
    blk = (None, SPAN, D_MODEL)

    def col(c):
        return lambda b, r, t: (b, t, r * n_col + 3 * g + c)

    def col_prev(c):
        return lambda b, r, t: (b, jnp.maximum(t - 1, 0), r * n_col + 3 * g + c)

    o, lse = pl.pallas_call(
        _attn_prompt_kernel,
        grid=(batch, dil, sub // SPAN),
        in_specs=[
            pl.BlockSpec(blk, col(0)),
            pl.BlockSpec(blk, col_prev(1)),
            pl.BlockSpec(blk, col(1)),
            pl.BlockSpec(blk, col_prev(2)),
            pl.BlockSpec(blk, col(2)),
            pl.BlockSpec((None, HEADS, SPAN, 2 * SPAN), lambda b, r, t: (g, 0, 0, 0)),
        ],
        out_specs=[
            pl.BlockSpec(blk, lambda b, r, t: (b, t, r)),
            pl.BlockSpec((None, None, SPAN, HEAD_DIM), lambda b, r, t: (b, r, t, 0)),
        ],
        out_shape=[
            jax.ShapeDtypeStruct((batch, sub, dil * D_MODEL), F32),
            jax.ShapeDtypeStruct((batch, dil, sub, HEAD_DIM), F32),
        ],
        compiler_params=_params(3),
        name=f"attn_prompt_g{g}",
    )(view, view, view, view, view, bias_p)
    lse = jnp.transpose(lse, (0, 2, 1, 3)).reshape(batch * seq, HEAD_DIM)
    return o.reshape(batch * seq, D_MODEL), lse


def _attn_sample_kernel(q_ref, kn_ref, vn_ref, k_ref, v_ref, bias_ref, o_ref, lse_ref, *, g, bb):
    lane = lax.broadcasted_iota(jnp.int32, (1, HEAD_DIM), 1)

    for bi in range(bb):
        row = slice(bi, bi + 1)
        lse_row = jnp.zeros((1, HEAD_DIM), F32)
        for h in range(HEADS):
            sl = slice(h * HEAD_DIM, (h + 1) * HEAD_DIM)
            c = g * HEADS + h
            q = q_ref[row, sl]
            s = jnp.sum(k_ref[bi, :, sl] * q, axis=-1, keepdims=True) * SCALE + bias_ref[:SPAN, c:c + 1]
            s_new = (jnp.sum(kn_ref[row, sl] * q, axis=-1, keepdims=True) * SCALE
                     + bias_ref[SPAN:SPAN + 1, c:c + 1])
            m = jnp.maximum(jnp.max(s, axis=0, keepdims=True), s_new)
            e = jnp.exp(s - m)
            e_new = jnp.exp(s_new - m)
            l = jnp.sum(e, axis=0, keepdims=True) + e_new
            o = jnp.sum(e * v_ref[bi, :, sl], axis=0, keepdims=True) + e_new * vn_ref[row, sl]
            o_ref[row, sl] = o / l
            lse_row = jnp.where(lane == h, m + jnp.log(l), lse_row)
        lse_ref[row, :] = lse_row


def _attn_sample(qkv, cache, bias_s, g):
    _, dil = DIL_GROUPS[g]
    b = qkv.shape[0]
    n_buf = cache.shape[2]
    assert n_buf == SPAN * dil
    view = cache.reshape(b, 2, SPAN, dil * D_MODEL)
    bb = 8
    n_cols = bias_s.shape[-1]
    row_blk = lambda c: pl.BlockSpec((bb, D_MODEL), lambda i: (i, 3 * g + c))
    return pl.pallas_call(
        functools.partial(_attn_sample_kernel, g=g, bb=bb),
        grid=(b // bb,),
        in_specs=[
            row_blk(0), row_blk(1), row_blk(2),
            pl.BlockSpec((bb, None, SPAN, D_MODEL), lambda i: (i, 0, 0, 0)),
            pl.BlockSpec((bb, None, SPAN, D_MODEL), lambda i: (i, 1, 0, 0)),
            pl.BlockSpec((None, 136, n_cols), lambda i: (g, 0, 0)),
        ],
        out_specs=[
            pl.BlockSpec((bb, D_MODEL), lambda i: (i, 0)),
            pl.BlockSpec((bb, HEAD_DIM), lambda i: (i, 0)),
        ],
        out_shape=[
            jax.ShapeDtypeStruct((b, D_MODEL), F32),
            jax.ShapeDtypeStruct((b, HEAD_DIM), F32),
        ],
        compiler_params=_params(1),
        name=f"attn_sample_g{g}",
    )(qkv, qkv, qkv, view, view, bias_s)


def _combine_kernel(o0_ref, o1_ref, o2_ref, l0_ref, l1_ref, l2_ref, a_ref):
    l0, l1, l2 = l0_ref[...], l1_ref[...], l2_ref[...]
    m = jnp.maximum(jnp.maximum(l0, l1), l2)
    e0, e1, e2 = jnp.exp(l0 - m), jnp.exp(l1 - m), jnp.exp(l2 - m)
    den = e0 + e1 + e2
    w0, w1, w2 = e0 / den, e1 / den, e2 / den
    for h in range(HEADS):
        sl = slice(h * HEAD_DIM, (h + 1) * HEAD_DIM)
        o = (o0_ref[:, sl] * w0[:, h:h + 1] + o1_ref[:, sl] * w1[:, h:h + 1]) + o2_ref[:, sl] * w2[:, h:h + 1]
        a_ref[:, sl] = o.astype(a_ref.dtype)


def _combine(outs, lses, tm):
    t = outs[0].shape[0]
    o_spec = pl.BlockSpec((tm, D_MODEL), lambda i: (i, 0))
    l_spec = pl.BlockSpec((tm, HEAD_DIM), lambda i: (i, 0))
    return pl.pallas_call(
        _combine_kernel,
        grid=(t // tm,),
        in_specs=[o_spec] * 3 + [l_spec] * 3,
        out_specs=o_spec,
        out_shape=jax.ShapeDtypeStruct((t, D_MODEL), BF16),
        compiler_params=_params(1),
        name="combine_groups",
    )(*outs, *lses)


def _ln_silu(z, g_ref, b_ref):
    mu = jnp.mean(z, axis=-1, keepdims=True)
    zc = z - mu
    var = jnp.mean(zc * zc, axis=-1, keepdims=True)
    y = (zc * lax.rsqrt(var + EPS)) * g_ref[...] + b_ref[...]
    return y * jax.nn.sigmoid(y)


HALO = 32
CONV_ROWS = 64
CONV_LANES = 256


def _conv_prompt_kernel(u_ref, prev_ref, w_ref, bdw_ref, g_ref, b_ref, a_ref, ext_ref, z_ref, *, tiles_per_seq):
    tm = u_ref.shape[0]
    starts_seq = pl.program_id(0) % tiles_per_seq == 0
    ext_ref[:HALO, :] = jnp.where(starts_seq, 0.0, prev_ref[...])
    ext_ref[HALO:, :] = u_ref[...]
    off = HALO - (CONV_WIDTH - 1)
    for r0 in range(0, tm, CONV_ROWS):
        for c0 in range(0, D_MODEL, CONV_LANES):
            cs = slice(c0, c0 + CONV_LANES)
            acc = jnp.zeros((CONV_ROWS, CONV_LANES), F32)
            for k in range(CONV_WIDTH):
                acc = acc + ext_ref[r0 + off + k:r0 + off + k + CONV_ROWS, cs] * w_ref[k:k + 1, cs]
            z_ref[r0:r0 + CONV_ROWS, cs] = acc + bdw_ref[:, cs]
    a_ref[...] = _ln_silu(z_ref[...], g_ref, b_ref).astype(a_ref.dtype)


def _conv_prompt(u, w_dw, b_dw, ln_g, ln_b, seq, tm):
    t = u.shape[0]
    row = lambda v: v.reshape(1, D_MODEL)
    vec = pl.BlockSpec((1, D_MODEL), lambda i: (0, 0))
    return pl.pallas_call(
        functools.partial(_conv_prompt_kernel, tiles_per_seq=seq // tm),
        grid=(t // tm,),
        in_specs=[
            pl.BlockSpec((tm, D_MODEL), lambda i: (i, 0)),
            pl.BlockSpec((HALO, D_MODEL), lambda i: (jnp.maximum(i * (tm // HALO) - 1, 0), 0)),
            pl.BlockSpec((CONV_WIDTH, D_MODEL), lambda i: (0, 0)),
            vec, vec, vec,
        ],
        out_specs=pl.BlockSpec((tm, D_MODEL), lambda i: (i, 0)),
        out_shape=jax.ShapeDtypeStruct((t, D_MODEL), BF16),
        scratch_shapes=[pltpu.VMEM((tm + HALO, D_MODEL), F32), pltpu.VMEM((tm, D_MODEL), F32)],
        compiler_params=_params(1),
        name="conv_prompt",
    )(u, u, w_dw, row(b_dw), row(ln_g), row(ln_b))


def _conv_sample_kernel(u_ref, st_ref, w_ref, bdw_ref, g_ref, b_ref, a_ref):
    n_state = st_ref.shape[1]
    z = jnp.sum(st_ref[...] * w_ref[:n_state, :][None], axis=1)
    z = z + u_ref[...] * w_ref[n_state:n_state + 1, :] + bdw_ref[...]
    a_ref[...] = _ln_silu(z, g_ref, b_ref).astype(a_ref.dtype)


def _conv_sample(u, state, w_dw, b_dw, ln_g, ln_b):
    b, n_state, _ = state.shape
    bb = 16
    row = lambda v: v.reshape(1, D_MODEL)
    vec = pl.BlockSpec((1, D_MODEL), lambda i: (0, 0))
    return pl.pallas_call(
        _conv_sample_kernel,
        grid=(b // bb,),
        in_specs=[
            pl.BlockSpec((bb, D_MODEL), lambda i: (i, 0)),
            pl.BlockSpec((bb, n_state, D_MODEL), lambda i: (i, 0, 0)),
            pl.BlockSpec((CONV_WIDTH, D_MODEL), lambda i: (0, 0)),
            vec, vec, vec,
        ],
        out_specs=pl.BlockSpec((bb, D_MODEL), lambda i: (i, 0)),
        out_shape=jax.ShapeDtypeStruct((b, D_MODEL), BF16),
        compiler_params=_params(1),
        name="conv_sample",
    )(u, state, w_dw, row(b_dw), row(ln_g), row(ln_b))


def _ffn(x, mod, g_ffn, w_gate, w_up, w_down, *, tm_up, tm_down, tag, g_final=None):
    u = _call_norm_matmul("swiglu", x, g_ffn, mod, (3, 4), [(w_gate, 0), (w_up, 0)], [],
                          tm=tm_up, tn=FFN_HIDDEN // 2, n_cols=FFN_HIDDEN, out_dtype=BF16,
                          name=f"ffn_up_{tag}")
    return _matmul_residual(u, w_down, None, x, mod, 5, tm=tm_down, name=f"ffn_down_{tag}", g_final=g_final)


def kernel(x_prompt, x_sample, cache_kv_w128, cache_kv_w512, cache_kv_w2048, state_conv, c_prompt, c_sample,
           w_mod, b_mod, g_mix, g_ffn, g_final, w_qkv, w_o, rel_bias, w_pw1, b_pw1, w_dw, b_dw, ln_g, ln_b,
           w_pw2, b_pw2, w_gate, w_up, w_down):
    batch, seq, d = x_prompt.shape
    dec_batch = x_sample.shape[0]
    assert d == D_MODEL and x_sample.shape[1] == 1
    caches = (cache_kv_w128, cache_kv_w512, cache_kv_w2048)

    bf = lambda w: w.astype(BF16)
    w_qkv, w_o, w_pw1, w_pw2, w_gate, w_up, w_down = map(bf, (w_qkv, w_o, w_pw1, w_pw2, w_gate, w_up, w_down))

    c_all = jnp.concatenate([c_sample, c_prompt, jnp.zeros((8 - batch, d), F32)], axis=0)
    mod_all = _modulation(c_all, w_mod, b_mod)
    bias_p, bias_s = _bias_tables(rel_bias)

    xp = x_prompt.reshape(batch * seq, d)
    xs = x_sample.reshape(dec_batch, d)
    n_qkv = 3 * N_GROUPS * d

    mod_p = _Mod(mod_all, 0, False, seq)
    mod_s = _Mod(mod_all, 0, True, 1)

    qkv_p = _call_norm_matmul("mm", xp, g_mix[0], mod_p, (0, 1), [(w_qkv[0], 0)], [],
                              tm=1024, tn=1024, n_cols=n_qkv, out_dtype=BF16, name="qkv_prompt")
    qkv_s = _call_norm_matmul("mm", xs, g_mix[0], mod_s, (0, 1), [(w_qkv[0], 0)], [],
                              tm=dec_batch, tn=1536, n_cols=n_qkv, out_dtype=F32, name="qkv_sample")

    kv_prompt = []
    for g, (window, _) in enumerate(DIL_GROUPS):
        keep = min(window, seq)
        tmk = min(keep, 1024)
        per_seq = keep // tmk
        row_of = lambda i, j, per_seq=per_seq, tmk=tmk, keep=keep: (i // per_seq) * seq + seq - keep + (i % per_seq) * tmk
        kv = _call_norm_matmul(
            "mm", xp, g_mix[0], mod_p, (0, 1), [(w_qkv[0], 3 * g + 1)], [],
            tm=tmk, tn=D_MODEL, n_cols=2 * D_MODEL, out_dtype=F32, name=f"kv_tail_g{g}",
            n_rows=batch * keep, row_of=row_of,
            out_shape=jax.ShapeDtypeStruct((batch, 2, keep, D_MODEL), F32),
            out_spec=pl.BlockSpec((None, None, tmk, D_MODEL),
                                  lambda i, j, per_seq=per_seq: (i // per_seq, j, i % per_seq, 0)))
        kv_prompt.append(kv.reshape(1, batch, 2, keep, HEADS, HEAD_DIM))

    outs, lses = zip(*[_attn_prompt(qkv_p, bias_p, g, batch, seq) for g in range(N_GROUPS)])
    a_p = _combine(outs, lses, 512)
    xp = _matmul_residual(a_p, w_o[0], None, xp, mod_p, 2, tm=1024, name="attn_out_prompt")
    xp = _ffn(xp, mod_p, g_ffn[0], w_gate[0], w_up[0], w_down[0], tm_up=1024, tm_down=512, tag="prompt0")

    outs, lses = zip(*[_attn_sample(qkv_s, caches[g][0], bias_s, g) for g in range(N_GROUPS)])
    a_s = _combine(outs, lses, dec_batch)
    xs = _matmul_residual(a_s, w_o[0], None, xs, mod_s, 2, tm=dec_batch, name="attn_out_sample")
    xs = _ffn(xs, mod_s, g_ffn[0], w_gate[0], w_up[0], w_down[0], tm_up=dec_batch, tm_down=dec_batch,
              tag="sample0")

    mod_p = _Mod(mod_all, 1, False, seq)
    mod_s = _Mod(mod_all, 1, True, 1)
    b_pw1_row = b_pw1[0].reshape(1, 2 * d)

    def glu(x, mod, tm, tag):
        return _call_norm_matmul("glu", x, g_mix[1], mod, (0, 1), [(w_pw1[0], 0), (w_pw1[0], d // 512)],
                                 [(b_pw1_row, 0), (b_pw1_row, d // 512)],
                                 tm=tm, tn=512, n_cols=d, out_dtype=F32, name=f"glu_{tag}")

    u_p = glu(xp, mod_p, 1024, "prompt")
    a_p = _conv_prompt(u_p, w_dw[0], b_dw[0], ln_g[0], ln_b[0], seq, 256)
    xp = _matmul_residual(a_p, w_pw2[0], b_pw2[0], xp, mod_p, 2, tm=1024, name="conv_out_prompt")
    y_p = _ffn(xp, mod_p, g_ffn[1], w_gate[1], w_up[1], w_down[1], tm_up=1024, tm_down=512, tag="prompt1",
               g_final=g_final)

    u_s = glu(xs, mod_s, dec_batch, "sample")
    a_s = _conv_sample(u_s, state_conv[0], w_dw[0], b_dw[0], ln_g[0], ln_b[0])
    xs = _matmul_residual(a_s, w_pw2[0], b_pw2[0], xs, mod_s, 2, tm=dec_batch, name="conv_out_sample")
    y_s = _ffn(xs, mod_s, g_ffn[1], w_gate[1], w_up[1], w_down[1], tm_up=dec_batch, tm_down=dec_batch,
               tag="sample1", g_final=g_final)

    conv_p = u_p.reshape(batch, seq, d)[:, seq - (CONV_WIDTH - 1):][None]
    conv_s = jnp.concatenate([state_conv[0][:, 1:], u_s[:, None, :]], axis=1)[None]
    qkv_s4 = qkv_s.reshape(dec_batch, N_GROUPS, 3, d)
    kv_sample = [qkv_s4[:, g, 1:3].reshape(1, dec_batch, 2, 1, HEADS, HEAD_DIM) for g in range(N_GROUPS)]
    return (y_p.reshape(batch, seq, d), y_s.reshape(dec_batch, 1, d),
            kv_prompt[0], kv_prompt[1], kv_prompt[2], conv_p,
            kv_sample[0], kv_sample[1], kv_sample[2], conv_s)
```

```python
import functools
import math

import numpy as np
import jax
import jax.numpy as jnp
from jax import lax
from jax.experimental import pallas as pl
from jax.experimental.pallas import tpu as pltpu

D_MODEL = 1024
DIL_GROUPS = ((128, 1), (512, 4), (2048, 16))
N_GROUPS = len(DIL_GROUPS)
HEADS = 8
HEAD_DIM = 128
SPAN = 128
N_BUCKETS = 32
MAX_DISTANCE = 2048
CONV_WIDTH = 31
FFN_HIDDEN = 2816
EPS = 1e-6
NEG_INF = -1e30
SCALE = HEAD_DIM ** -0.5

F32 = jnp.float32
BF16 = jnp.bfloat16

VMEM_LIMIT_BYTES = 56 * 1024 * 1024


def _params(n_axes, vmem=VMEM_LIMIT_BYTES):
    return pltpu.CompilerParams(dimension_semantics=("arbitrary",) * n_axes, vmem_limit_bytes=vmem)


def _t5_bucket_np(dist):
    max_exact = N_BUCKETS // 2
    n = np.maximum(dist, 1).astype(np.float32)
    large = max_exact + (np.log(n / np.float32(max_exact)) / np.float32(math.log(MAX_DISTANCE / max_exact))
                         * np.float32(N_BUCKETS - max_exact)).astype(np.int32)
    large = np.minimum(large, N_BUCKETS - 1)
    return np.where(dist < max_exact, dist, large).astype(np.int32)


def _bucket_tables():
    qi = np.arange(SPAN)[:, None]
    ki = np.arange(2 * SPAN)[None, :]
    delta = qi + SPAN - ki
    in_band = (delta >= 0) & (delta <= SPAN)
    prompt, sample = [], []
    for _, dilation in DIL_GROUPS:
        b = _t5_bucket_np(np.clip(delta, 0, SPAN) * dilation)
        prompt.append(np.where(in_band, b, -1))
        j = SPAN - np.arange(136)
        sample.append(_t5_bucket_np(np.maximum(j, 0) * dilation)[:, None])
    return np.stack(prompt).astype(np.int32), np.stack(sample).astype(np.int32)


def _bias_kernel(rb_smem, rb_ref, bp_ref, bs_ref, op_ref, os_ref):
    g = pl.program_id(0)
    bp = bp_ref[...]
    for h in range(HEADS):
        acc = jnp.full(bp.shape, NEG_INF, F32)
        for b in range(N_BUCKETS):
            acc = jnp.where(bp == b, rb_smem[b, g * HEADS + h], acc)
        op_ref[h] = acc
    bs = bs_ref[...]
    acc = jnp.zeros((bs.shape[0], rb_ref.shape[1]), F32)
    for b in range(N_BUCKETS):
        acc = jnp.where(bs == b, rb_ref[b:b + 1, :], acc)
    os_ref[...] = acc


def _bias_tables(rel_bias):
    bp, bs = _bucket_tables()
    n_cols = rel_bias.shape[1]
    return pl.pallas_call(
        _bias_kernel,
        grid=(N_GROUPS,),
        in_specs=[
            pl.BlockSpec(memory_space=pltpu.SMEM),
            pl.BlockSpec((N_BUCKETS, n_cols), lambda g: (0, 0)),
            pl.BlockSpec((None, SPAN, 2 * SPAN), lambda g: (g, 0, 0)),
            pl.BlockSpec((None, 136, 1), lambda g: (g, 0, 0)),
        ],
        out_specs=[
            pl.BlockSpec((None, HEADS, SPAN, 2 * SPAN), lambda g: (g, 0, 0, 0)),
            pl.BlockSpec((None, 136, n_cols), lambda g: (g, 0, 0)),
        ],
        out_shape=[
            jax.ShapeDtypeStruct((N_GROUPS, HEADS, SPAN, 2 * SPAN), F32),
            jax.ShapeDtypeStruct((N_GROUPS, 136, n_cols), F32),
        ],
        compiler_params=_params(1),
        name="bias_tables",
    )(rel_bias, rel_bias, jnp.asarray(bp), jnp.asarray(bs))


def _mod_kernel(c_ref, w_ref, b_ref, o_ref):
    c = c_ref[...]
    a = (c * jax.nn.sigmoid(c)).astype(BF16)
    o_ref[...] = jnp.dot(a, w_ref[...].astype(BF16), preferred_element_type=F32) + b_ref[...]


def _modulation(c_all, w_mod, b_mod):
    depth, d, n = w_mod.shape
    rows = c_all.shape[0]
    tn = 1536
    return pl.pallas_call(
        _mod_kernel,
        grid=(depth, n // tn),
        in_specs=[
            pl.BlockSpec((rows, d), lambda l, j: (0, 0)),
            pl.BlockSpec((None, d, tn), lambda l, j: (l, 0, j)),
            pl.BlockSpec((None, 1, tn), lambda l, j: (l, 0, j)),
        ],
        out_specs=pl.BlockSpec((None, rows, tn), lambda l, j: (l, 0, j)),
        out_shape=jax.ShapeDtypeStruct((depth, rows, n), F32),
        compiler_params=_params(2),
        name="modulation",
    )(c_all, w_mod, b_mod.reshape(depth, 1, n))


def _normed(x_ref, g_ref, sc_ref, sh_ref):
    x = x_ref[...]
    y = x * lax.rsqrt(jnp.mean(x * x, axis=-1, keepdims=True) + EPS)
    return ((y * g_ref[...]) * (1.0 + sc_ref[...]) + sh_ref[...]).astype(BF16)


def _norm_mm_kernel(x_ref, g_ref, sc_ref, sh_ref, w_ref, o_ref, h_ref):
    @pl.when(pl.program_id(1) == 0)
    def _():
        h_ref[...] = _normed(x_ref, g_ref, sc_ref, sh_ref)
    o_ref[...] = jnp.dot(h_ref[...], w_ref[...], preferred_element_type=F32).astype(o_ref.dtype)


def _norm_glu_kernel(x_ref, g_ref, sc_ref, sh_ref, wa_ref, wg_ref, ba_ref, bg_ref, o_ref, h_ref):
    @pl.when(pl.program_id(1) == 0)
    def _():
        h_ref[...] = _normed(x_ref, g_ref, sc_ref, sh_ref)
    h = h_ref[...]
    a = jnp.dot(h, wa_ref[...], preferred_element_type=F32) + ba_ref[...]
    gate = jnp.dot(h, wg_ref[...], preferred_element_type=F32) + bg_ref[...]
    o_ref[...] = a * jax.nn.sigmoid(gate)


def _norm_swiglu_kernel(x_ref, g_ref, sc_ref, sh_ref, wg_ref, wu_ref, o_ref, h_ref):
    @pl.when(pl.program_id(1) == 0)
    def _():
        h_ref[...] = _normed(x_ref, g_ref, sc_ref, sh_ref)
    h = h_ref[...]
    gate = jnp.dot(h, wg_ref[...], preferred_element_type=F32)
    up = jnp.dot(h, wu_ref[...], preferred_element_type=F32)
    o_ref[...] = ((gate * jax.nn.sigmoid(gate)) * up).astype(o_ref.dtype)


class _Mod:
    def __init__(self, mod_all, layer, per_token, rows_per_seq):
        self.layer = layer
        self.per_token = per_token
        self.rows_per_seq = rows_per_seq
        depth, rows, n = mod_all.shape
        self.n_sample = rows - 8
        self.arr = mod_all if per_token else mod_all.reshape(depth, rows, 1, n)

    def spec(self, chunk, tm, row_of):
        layer = self.layer
        if self.per_token:
            return pl.BlockSpec((None, tm, D_MODEL), lambda *i: (layer, row_of(*i) // tm, chunk))
        first, per = self.n_sample, self.rows_per_seq
        return pl.BlockSpec((None, None, 1, D_MODEL), lambda *i: (layer, first + row_of(*i) // per, 0, chunk))


def _call_norm_matmul(kind, x, gain, mod, chunks, weights, biases, *, tm, tn, n_cols, out_dtype, name,
                      n_rows=None, row_of=None, out_shape=None, out_spec=None):
    t = x.shape[0] if n_rows is None else n_rows
    if row_of is None:
        row_of = lambda i, j: i * tm
    kernel = {"mm": _norm_mm_kernel, "glu": _norm_glu_kernel, "swiglu": _norm_swiglu_kernel}[kind]
    chunk_shift, chunk_scale = chunks
    in_specs = [
        pl.BlockSpec((tm, D_MODEL), lambda i, j: (row_of(i, j) // tm, 0)),
        pl.BlockSpec((1, D_MODEL), lambda i, j: (0, 0)),
        mod.spec(chunk_scale, tm, row_of),
        mod.spec(chunk_shift, tm, row_of),
    ]
    args = [x, gain.reshape(1, D_MODEL), mod.arr, mod.arr]
    for w, col0 in weights:
        in_specs.append(pl.BlockSpec((D_MODEL, tn), lambda i, j, col0=col0: (0, col0 + j)))
        args.append(w)
    for b, col0 in biases:
        in_specs.append(pl.BlockSpec((1, tn), lambda i, j, col0=col0: (0, col0 + j)))
        args.append(b)
    if out_shape is None:
        out_shape = jax.ShapeDtypeStruct((t, n_cols), out_dtype)
        out_spec = pl.BlockSpec((tm, tn), lambda i, j: (i, j))
    return pl.pallas_call(
        kernel,
        grid=(t // tm, n_cols // tn),
        in_specs=in_specs,
        out_specs=out_spec,
        out_shape=out_shape,
        scratch_shapes=[pltpu.VMEM((tm, D_MODEL), BF16)],
        compiler_params=_params(2),
        name=name,
    )(*args)


LANES = 128
N_LANE_SLABS = D_MODEL // LANES


def _qkv_prompt_kernel(x_ref, g_ref, sc_ref, sh_ref, w_ref, o0_ref, o1_ref, o2_ref, h_ref, slab_ref):
    tm = x_ref.shape[0]
    n = pl.program_id(1)
    o_refs = (o0_ref, o1_ref, o2_ref)

    @pl.when(n == 0)
    def _():
        x = x_ref[...]
        y = x * lax.rsqrt(jnp.mean(x * x, axis=-1, keepdims=True) + EPS)
        h = (y * g_ref[...]) * (1.0 + sc_ref[...]) + sh_ref[...]
        for c in range(N_LANE_SLABS):
            slab_ref[c] = h[:, c * LANES:(c + 1) * LANES]
        for gi, (_, dil) in enumerate(DIL_GROUPS):
            rows = tm // dil
            for r in range(dil):
                for c in range(N_LANE_SLABS):
                    src = slab_ref[c] if dil == 1 else slab_ref[c, pl.ds(r, rows, stride=dil), :]
                    h_ref[gi, r * rows:(r + 1) * rows, c * LANES:(c + 1) * LANES] = src.astype(BF16)

    for gi, (_, dil) in enumerate(DIL_GROUPS):
        @pl.when(n // 3 == gi)
        def _(gi=gi, dil=dil):
            acc = jnp.dot(h_ref[gi], w_ref[...], preferred_element_type=F32).astype(BF16)
            rows = tm // dil
            for r in range(dil):
                o_refs[gi][r] = acc[r * rows:(r + 1) * rows, :]


def _qkv_prompt(x, gain, mod, w, batch, seq, tm):
    tiles_per_seq = seq // tm
    row_of = lambda i, n: i * tm
    out_specs, out_shapes = [], []
    for g, (_, dil) in enumerate(DIL_GROUPS):
        out_specs.append(pl.BlockSpec(
            (None, dil, tm // dil, D_MODEL),
            lambda i, n, g=g: (i // tiles_per_seq, 0, i % tiles_per_seq, jnp.clip(n - 3 * g, 0, 2))))
        out_shapes.append(jax.ShapeDtypeStruct((batch, dil, seq // dil, 3 * D_MODEL), BF16))
    return pl.pallas_call(
        _qkv_prompt_kernel,
        grid=(batch * tiles_per_seq, 3 * N_GROUPS),
        in_specs=[
            pl.BlockSpec((tm, D_MODEL), lambda i, n: (i, 0)),
            pl.BlockSpec((1, D_MODEL), lambda i, n: (0, 0)),
            mod.spec(1, tm, row_of),
            mod.spec(0, tm, row_of),
            pl.BlockSpec((D_MODEL, D_MODEL), lambda i, n: (0, n)),
        ],
        out_specs=out_specs,
        out_shape=out_shapes,
        scratch_shapes=[pltpu.VMEM((N_GROUPS, tm, D_MODEL), BF16), pltpu.VMEM((N_LANE_SLABS, tm, LANES), F32)],
        compiler_params=_params(2),
        name="qkv_prompt",
    )(x, gain.reshape(1, D_MODEL), mod.arr, mod.arr, w)


def _mm_res_kernel(*refs, has_bias, final_norm):
    a_ref, w_ref = refs[0], refs[1]
    k = 2
    b_ref = None
    if has_bias:
        b_ref = refs[k]
        k += 1
    x_ref, gate_ref = refs[k], refs[k + 1]
    k += 2
    gf_ref = None
    if final_norm:
        gf_ref = refs[k]
        k += 1
    o_ref = refs[k]
    out = jnp.dot(a_ref[...], w_ref[...], preferred_element_type=F32)
    if has_bias:
        out = out + b_ref[...]
    x = x_ref[...] + gate_ref[...] * out
    if final_norm:
        x = (x * lax.rsqrt(jnp.mean(x * x, axis=-1, keepdims=True) + EPS)) * gf_ref[...]
    o_ref[...] = x


def _matmul_residual(a, w, bias, x, mod, chunk_gate, *, tm, name, g_final=None):
    t, k = a.shape
    row_of = lambda i: i * tm
    in_specs = [
        pl.BlockSpec((tm, k), lambda i: (i, 0)),
        pl.BlockSpec((k, D_MODEL), lambda i: (0, 0)),
    ]
    args = [a, w]
    if bias is not None:
        in_specs.append(pl.BlockSpec((1, D_MODEL), lambda i: (0, 0)))
        args.append(bias.reshape(1, D_MODEL))
    in_specs += [pl.BlockSpec((tm, D_MODEL), lambda i: (i, 0)), mod.spec(chunk_gate, tm, row_of)]
    args += [x, mod.arr]
    if g_final is not None:
        in_specs.append(pl.BlockSpec((1, D_MODEL), lambda i: (0, 0)))
        args.append(g_final.reshape(1, D_MODEL))
    return pl.pallas_call(
        functools.partial(_mm_res_kernel, has_bias=bias is not None, final_norm=g_final is not None),
        grid=(t // tm,),
        in_specs=in_specs,
        out_specs=pl.BlockSpec((tm, D_MODEL), lambda i: (i, 0)),
        out_shape=jax.ShapeDtypeStruct((t, D_MODEL), F32),
        compiler_params=_params(1),
        name=name,
    )(*args)


def _attn_prompt_kernel(q_ref, kp_ref, kc_ref, vp_ref, vc_ref, bias_ref, o_ref, lse_ref):
    first = pl.program_id(2) == 0
    lane = lax.broadcasted_iota(jnp.int32, (SPAN, HEAD_DIM), 1)
    lse_all = jnp.zeros((SPAN, HEAD_DIM), F32)
    dims = (((1,), (1,)), ((), ()))
    for h in range(HEADS):
        sl = slice(h * HEAD_DIM, (h + 1) * HEAD_DIM)
        q = q_ref[:, sl]
        sp = lax.dot_general(q, kp_ref[:, sl], dims, preferred_element_type=F32) * SCALE + bias_ref[h, :, :SPAN]
        sc = lax.dot_general(q, kc_ref[:, sl], dims, preferred_element_type=F32) * SCALE + bias_ref[h, :, SPAN:]
        sp = jnp.where(first, NEG_INF, sp)
        m = jnp.maximum(jnp.max(sp, axis=-1, keepdims=True), jnp.max(sc, axis=-1, keepdims=True))
        ep = jnp.exp(sp - m)
        ec = jnp.exp(sc - m)
        l = jnp.sum(ep, axis=-1, keepdims=True) + jnp.sum(ec, axis=-1, keepdims=True)
        o = (jnp.dot(ep.astype(BF16), vp_ref[:, sl], preferred_element_type=F32)
             + jnp.dot(ec.astype(BF16), vc_ref[:, sl], preferred_element_type=F32))
        o_ref[:, sl] = o / l
        lse_all = jnp.where(lane == h, m + jnp.log(l), lse_all)
    lse_ref[...] = lse_all


def _attn_prompt(qkv_g, bias_p, g):
    batch, dil, sub, _ = qkv_g.shape
    blk = (None, None, SPAN, D_MODEL)

    def col(c):
        return lambda b, r, t: (b, r, t, c)

    def col_prev(c):
        return lambda b, r, t: (b, r, jnp.maximum(t - 1, 0), c)

    return pl.pallas_call(
        _attn_prompt_kernel,
        grid=(batch, dil, sub // SPAN),
        in_specs=[
            pl.BlockSpec(blk, col(0)),
            pl.BlockSpec(blk, col_prev(1)),
            pl.BlockSpec(blk, col(1)),
            pl.BlockSpec(blk, col_prev(2)),
            pl.BlockSpec(blk, col(2)),
            pl.BlockSpec((None, HEADS, SPAN, 2 * SPAN), lambda b, r, t: (g, 0, 0, 0)),
        ],
        out_specs=[
            pl.BlockSpec(blk, lambda b, r, t: (b, r, t, 0)),
            pl.BlockSpec((None, None, SPAN, HEAD_DIM), lambda b, r, t: (b, r, t, 0)),
        ],
        out_shape=[
            jax.ShapeDtypeStruct((batch, dil, sub, D_MODEL), F32),
            jax.ShapeDtypeStruct((batch, dil, sub, HEAD_DIM), F32),
        ],
        compiler_params=_params(3),
        name=f"attn_prompt_g{g}",
    )(qkv_g, qkv_g, qkv_g, qkv_g, qkv_g, bias_p)


def _attn_sample_kernel(qkv_ref, k0_ref, v0_ref, k1_ref, v1_ref, k2_ref, v2_ref, bias_ref, a_ref, *, bb):
    kv_refs = ((k0_ref, v0_ref), (k1_ref, v1_ref), (k2_ref, v2_ref))
    for bi in range(bb):
        outs, lses = [], []
        for g, (k_ref, v_ref) in enumerate(kv_refs):
            q, k_new, v_new = qkv_ref[bi, 3 * g], qkv_ref[bi, 3 * g + 1], qkv_ref[bi, 3 * g + 2]
            s = jnp.sum(k_ref[bi] * q[None], axis=-1, keepdims=True) * SCALE + bias_ref[g, :SPAN]
            s_new = jnp.sum(k_new * q, axis=-1, keepdims=True) * SCALE + bias_ref[g, SPAN]
            m = jnp.maximum(jnp.max(s, axis=0), s_new)
            e = jnp.exp(s - m[None])
            e_new = jnp.exp(s_new - m)
            l = jnp.sum(e, axis=0) + e_new
            outs.append((jnp.sum(e * v_ref[bi], axis=0) + e_new * v_new) / l)
            lses.append(m + jnp.log(l))
        top = jnp.maximum(jnp.maximum(lses[0], lses[1]), lses[2])
        ws = [jnp.exp(lse - top) for lse in lses]
        den = ws[0] + ws[1] + ws[2]
        a_ref[bi] = (outs[0] * (ws[0] / den) + outs[1] * (ws[1] / den)) + outs[2] * (ws[2] / den)


def _attn_sample(qkv, caches, layer, bias_s):
    b = qkv.shape[0]
    bb = 4
    in_specs = [pl.BlockSpec((bb,) + qkv.shape[1:], lambda i: (i, 0, 0, 0))]
    args = [qkv]
    for g, (_, dil) in enumerate(DIL_GROUPS):
        cache = caches[g]
        assert cache.shape[3] == SPAN * dil
        view = cache.reshape(cache.shape[:3] + (SPAN, dil, HEADS, HEAD_DIM))
        for kv in range(2):
            in_specs.append(pl.BlockSpec((None, bb, None, SPAN, None, HEADS, HEAD_DIM),
                                         lambda i, kv=kv: (layer, i, kv, 0, 0, 0, 0)))
            args.append(view)
    in_specs.append(pl.BlockSpec(bias_s.shape, lambda i: (0, 0, 0, 0)))
    args.append(bias_s)
    return pl.pallas_call(
        functools.partial(_attn_sample_kernel, bb=bb),
        grid=(b // bb,),
        in_specs=in_specs,
        out_specs=pl.BlockSpec((bb, HEADS, HEAD_DIM), lambda i: (i, 0, 0)),
        out_shape=jax.ShapeDtypeStruct((b, HEADS, HEAD_DIM), F32),
        compiler_params=_params(1),
        name="attn_sample",
    )(*args)


def _combine_kernel(o0_ref, o1_ref, o2_ref, l0_ref, l1_ref, l2_ref, a_ref, os_ref, ls_ref):
    tm = a_ref.shape[0]
    for gi, o_ref, l_ref in ((1, o1_ref, l1_ref), (2, o2_ref, l2_ref)):
        dil = DIL_GROUPS[gi][1]
        rows = tm // dil
        for r in range(dil):
            ls_ref[gi - 1, pl.ds(r, rows, stride=dil), :] = l_ref[r]
            for h in range(HEADS):
                os_ref[gi - 1, h, pl.ds(r, rows, stride=dil), :] = o_ref[r, :, h * HEAD_DIM:(h + 1) * HEAD_DIM]
    l0, l1, l2 = l0_ref[0], ls_ref[0], ls_ref[1]
    m = jnp.maximum(jnp.maximum(l0, l1), l2)
    e0, e1, e2 = jnp.exp(l0 - m), jnp.exp(l1 - m), jnp.exp(l2 - m)
    den = e0 + e1 + e2
    w0, w1, w2 = e0 / den, e1 / den, e2 / den
    for h in range(HEADS):
        sl = slice(h * HEAD_DIM, (h + 1) * HEAD_DIM)
        o = (o0_ref[0, :, sl] * w0[:, h:h + 1] + os_ref[0, h] * w1[:, h:h + 1]) + os_ref[1, h] * w2[:, h:h + 1]
        a_ref[:, sl] = o.astype(a_ref.dtype)


def _combine(outs, lses, seq, tm):
    batch = outs[0].shape[0]
    tiles_per_seq = seq // tm
    in_specs = []
    for width in (D_MODEL, HEAD_DIM):
        for _, dil in DIL_GROUPS:
            in_specs.append(pl.BlockSpec((None, dil, tm // dil, width),
                                         lambda i: (i // tiles_per_seq, 0, i % tiles_per_seq, 0)))
    return pl.pallas_call(
        _combine_kernel,
        grid=(batch * tiles_per_seq,),
        in_specs=in_specs,
        out_specs=pl.BlockSpec((tm, D_MODEL), lambda i: (i, 0)),
        out_shape=jax.ShapeDtypeStruct((batch * seq, D_MODEL), BF16),
        scratch_shapes=[pltpu.VMEM((N_GROUPS - 1, HEADS, tm, HEAD_DIM), F32),
                        pltpu.VMEM((N_GROUPS - 1, tm, HEAD_DIM), F32)],
        compiler_params=_params(1),
        name="combine_groups",
    )(*outs, *lses)


def _ln_silu(z, g_ref, b_ref):
    mu = jnp.mean(z, axis=-1, keepdims=True)
    zc = z - mu
    var = jnp.mean(zc * zc, axis=-1, keepdims=True)
    y = (zc * lax.rsqrt(var + EPS)) * g_ref[...] + b_ref[...]
    return y * jax.nn.sigmoid(y)


HALO = 32
CONV_ROWS = 64
CONV_LANES = 256


def _conv_prompt_kernel(u_ref, prev_ref, w_ref, bdw_ref, g_ref, b_ref, a_ref, ext_ref, z_ref, *, tiles_per_seq):
    tm = u_ref.shape[0]
    starts_seq = pl.program_id(0) % tiles_per_seq == 0
    ext_ref[:HALO, :] = jnp.where(starts_seq, 0.0, prev_ref[...])
    ext_ref[HALO:, :] = u_ref[...]
    off = HALO - (CONV_WIDTH - 1)
    for r0 in range(0, tm, CONV_ROWS):
        for c0 in range(0, D_MODEL, CONV_LANES):
            cs = slice(c0, c0 + CONV_LANES)
            acc = jnp.zeros((CONV_ROWS, CONV_LANES), F32)
            for k in range(CONV_WIDTH):
                acc = acc + ext_ref[r0 + off + k:r0 + off + k + CONV_ROWS, cs] * w_ref[k:k + 1, cs]
            z_ref[r0:r0 + CONV_ROWS, cs] = acc + bdw_ref[:, cs]
    a_ref[...] = _ln_silu(z_ref[...], g_ref, b_ref).astype(a_ref.dtype)


def _conv_prompt(u, w_dw, b_dw, ln_g, ln_b, seq, tm):
    t = u.shape[0]
    row = lambda v: v.reshape(1, D_MODEL)
    vec = pl.BlockSpec((1, D_MODEL), lambda i: (0, 0))
    return pl.pallas_call(
        functools.partial(_conv_prompt_kernel, tiles_per_seq=seq // tm),
        grid=(t // tm,),
        in_specs=[
            pl.BlockSpec((tm, D_MODEL), lambda i: (i, 0)),
            pl.BlockSpec((HALO, D_MODEL), lambda i: (jnp.maximum(i * (tm // HALO) - 1, 0), 0)),
            pl.BlockSpec((CONV_WIDTH, D_MODEL), lambda i: (0, 0)),
            vec, vec, vec,
        ],
        out_specs=pl.BlockSpec((tm, D_MODEL), lambda i: (i, 0)),
        out_shape=jax.ShapeDtypeStruct((t, D_MODEL), BF16),
        scratch_shapes=[pltpu.VMEM((tm + HALO, D_MODEL), F32), pltpu.VMEM((tm, D_MODEL), F32)],
        compiler_params=_params(1),
        name="conv_prompt",
    )(u, u, w_dw, row(b_dw), row(ln_g), row(ln_b))


def _conv_sample_kernel(u_ref, st_ref, w_ref, bdw_ref, g_ref, b_ref, a_ref):
    n_state = st_ref.shape[1]
    z = jnp.sum(st_ref[...] * w_ref[:n_state, :][None], axis=1)
    z = z + u_ref[...] * w_ref[n_state:n_state + 1, :] + bdw_ref[...]
    a_ref[...] = _ln_silu(z, g_ref, b_ref).astype(a_ref.dtype)


def _conv_sample(u, state, w_dw, b_dw, ln_g, ln_b):
    b, n_state, _ = state.shape
    bb = 16
    row = lambda v: v.reshape(1, D_MODEL)
    vec = pl.BlockSpec((1, D_MODEL), lambda i: (0, 0))
    return pl.pallas_call(
        _conv_sample_kernel,
        grid=(b // bb,),
        in_specs=[
            pl.BlockSpec((bb, D_MODEL), lambda i: (i, 0)),
            pl.BlockSpec((bb, n_state, D_MODEL), lambda i: (i, 0, 0)),
            pl.BlockSpec((CONV_WIDTH, D_MODEL), lambda i: (0, 0)),
            vec, vec, vec,
        ],
        out_specs=pl.BlockSpec((bb, D_MODEL), lambda i: (i, 0)),
        out_shape=jax.ShapeDtypeStruct((b, D_MODEL), BF16),
        compiler_params=_params(1),
        name="conv_sample",
    )(u, state, w_dw, row(b_dw), row(ln_g), row(ln_b))


def _ffn(x, mod, g_ffn, w_gate, w_up, w_down, *, tm_up, tm_down, tag, g_final=None):
    u = _call_norm_matmul("swiglu", x, g_ffn, mod, (3, 4), [(w_gate, 0), (w_up, 0)], [],
                          tm=tm_up, tn=FFN_HIDDEN // 2, n_cols=FFN_HIDDEN, out_dtype=BF16,
                          name=f"ffn_up_{tag}")
    return _matmul_residual(u, w_down, None, x, mod, 5, tm=tm_down, name=f"ffn_down_{tag}", g_final=g_final)


def kernel(x_prompt, x_sample, cache_kv_w128, cache_kv_w512, cache_kv_w2048, state_conv, c_prompt, c_sample,
           w_mod, b_mod, g_mix, g_ffn, g_final, w_qkv, w_o, rel_bias, w_pw1, b_pw1, w_dw, b_dw, ln_g, ln_b,
           w_pw2, b_pw2, w_gate, w_up, w_down):
    batch, seq, d = x_prompt.shape
    dec_batch = x_sample.shape[0]
    assert d == D_MODEL and x_sample.shape[1] == 1
    caches = (cache_kv_w128, cache_kv_w512, cache_kv_w2048)

    bf = lambda w: w.astype(BF16)
    w_qkv, w_o, w_pw1, w_pw2, w_gate, w_up, w_down = map(bf, (w_qkv, w_o, w_pw1, w_pw2, w_gate, w_up, w_down))

    c_all = jnp.concatenate([c_sample, c_prompt, jnp.zeros((8 - batch, d), F32)], axis=0)
    mod_all = _modulation(c_all, w_mod, b_mod)
    bias_p, bias_s = _bias_tables(rel_bias)

    xp = x_prompt.reshape(batch * seq, d)
    xs = x_sample.reshape(dec_batch, d)
    n_qkv = 3 * N_GROUPS * d

    mod_p = _Mod(mod_all, 0, False, seq)
    mod_s = _Mod(mod_all, 0, True, 1)

    qkv_p = _qkv_prompt(xp, g_mix[0], mod_p, w_qkv[0], batch, seq, 1024)
    qkv_s = _call_norm_matmul("mm", xs, g_mix[0], mod_s, (0, 1), [(w_qkv[0], 0)], [],
                              tm=dec_batch, tn=1536, n_cols=n_qkv, out_dtype=F32, name="qkv_sample")

    kv_prompt = []
    for g, (window, _) in enumerate(DIL_GROUPS):
        keep = min(window, seq)
        tmk = min(keep, 1024)
        per_seq = keep // tmk
        row_of = lambda i, j, per_seq=per_seq, tmk=tmk, keep=keep: (i // per_seq) * seq + seq - keep + (i % per_seq) * tmk
        kv = _call_norm_matmul(
            "mm", xp, g_mix[0], mod_p, (0, 1), [(w_qkv[0], 3 * g + 1)], [],
            tm=tmk, tn=D_MODEL, n_cols=2 * D_MODEL, out_dtype=F32, name=f"kv_tail_g{g}",
            n_rows=batch * keep, row_of=row_of,
            out_shape=jax.ShapeDtypeStruct((batch, 2, keep, D_MODEL), F32),
            out_spec=pl.BlockSpec((None, None, tmk, D_MODEL),
                                  lambda i, j, per_seq=per_seq: (i // per_seq, j, i % per_seq, 0)))
        kv_prompt.append(kv.reshape(1, batch, 2, keep, HEADS, HEAD_DIM))

    outs, lses = zip(*[_attn_prompt(qkv_p[g], bias_p, g) for g in range(N_GROUPS)])
    a_p = _combine(outs, lses, seq, 512)
    xp = _matmul_residual(a_p, w_o[0], None, xp, mod_p, 2, tm=1024, name="attn_out_prompt")
    xp = _ffn(xp, mod_p, g_ffn[0], w_gate[0], w_up[0], w_down[0], tm_up=1024, tm_down=512, tag="prompt0")

    qkv_s4 = qkv_s.reshape(dec_batch, 3 * N_GROUPS, HEADS, HEAD_DIM)
    bias_s4 = jnp.stack([bias_s[g, :, g * HEADS:(g + 1) * HEADS] for g in range(N_GROUPS)])[..., None]
    a_s = _attn_sample(qkv_s4, caches, 0, bias_s4).reshape(dec_batch, d).astype(BF16)
    xs = _matmul_residual(a_s, w_o[0], None, xs, mod_s, 2, tm=dec_batch, name="attn_out_sample")
    xs = _ffn(xs, mod_s, g_ffn[0], w_gate[0], w_up[0], w_down[0], tm_up=dec_batch, tm_down=dec_batch,
              tag="sample0")

    mod_p = _Mod(mod_all, 1, False, seq)
    mod_s = _Mod(mod_all, 1, True, 1)
    b_pw1_row = b_pw1[0].reshape(1, 2 * d)

    def glu(x, mod, tm, tag):
        return _call_norm_matmul("glu", x, g_mix[1], mod, (0, 1), [(w_pw1[0], 0), (w_pw1[0], d // 512)],
                                 [(b_pw1_row, 0), (b_pw1_row, d // 512)],
                                 tm=tm, tn=512, n_cols=d, out_dtype=F32, name=f"glu_{tag}")

    u_p = glu(xp, mod_p, 1024, "prompt")
    a_p = _conv_prompt(u_p, w_dw[0], b_dw[0], ln_g[0], ln_b[0], seq, 256)
    xp = _matmul_residual(a_p, w_pw2[0], b_pw2[0], xp, mod_p, 2, tm=1024, name="conv_out_prompt")
    y_p = _ffn(xp, mod_p, g_ffn[1], w_gate[1], w_up[1], w_down[1], tm_up=1024, tm_down=512, tag="prompt1",
               g_final=g_final)

    u_s = glu(xs, mod_s, dec_batch, "sample")
    a_s = _conv_sample(u_s, state_conv[0], w_dw[0], b_dw[0], ln_g[0], ln_b[0])
    xs = _matmul_residual(a_s, w_pw2[0], b_pw2[0], xs, mod_s, 2, tm=dec_batch, name="conv_out_sample")
    y_s = _ffn(xs, mod_s, g_ffn[1], w_gate[1], w_up[1], w_down[1], tm_up=dec_batch, tm_down=dec_batch,
               tag="sample1", g_final=g_final)

    conv_p = u_p.reshape(batch, seq, d)[:, seq - (CONV_WIDTH - 1):][None]
    conv_s = jnp.concatenate([state_conv[0][:, 1:], u_s[:, None, :]], axis=1)[None]
    kv_sample = [qkv_s4[:, 3 * g + 1:3 * g + 3].reshape(1, dec_batch, 2, 1, HEADS, HEAD_DIM)
                 for g in range(N_GROUPS)]
    return (y_p.reshape(batch, seq, d), y_s.reshape(dec_batch, 1, d),
            kv_prompt[0], kv_prompt[1], kv_prompt[2], conv_p,
            kv_sample[0], kv_sample[1], kv_sample[2], conv_s)
```

```python
import functools
import math

import numpy as np
import jax
import jax.numpy as jnp
from jax import lax
from jax.experimental import pallas as pl
from jax.experimental.pallas import tpu as pltpu

D_MODEL = 1024
DIL_GROUPS = ((128, 1), (512, 4), (2048, 16))
N_GROUPS = len(DIL_GROUPS)
HEADS = 8
HEAD_DIM = 128
SPAN = 128
N_BUCKETS = 32
MAX_DISTANCE = 2048
CONV_WIDTH = 31
FFN_HIDDEN = 2816
EPS = 1e-6
NEG_INF = -1e30
SCALE = HEAD_DIM ** -0.5
LOG2E = math.log2(math.e)
LN2 = math.log(2.0)

F32 = jnp.float32
BF16 = jnp.bfloat16

VMEM_LIMIT_BYTES = 56 * 1024 * 1024


def _params(n_axes, vmem=VMEM_LIMIT_BYTES):
    return pltpu.CompilerParams(dimension_semantics=("arbitrary",) * n_axes, vmem_limit_bytes=vmem)


def _t5_bucket_np(dist):
    max_exact = N_BUCKETS // 2
    n = np.maximum(dist, 1).astype(np.float32)
    large = max_exact + (np.log(n / np.float32(max_exact)) / np.float32(math.log(MAX_DISTANCE / max_exact))
                         * np.float32(N_BUCKETS - max_exact)).astype(np.int32)
    large = np.minimum(large, N_BUCKETS - 1)
    return np.where(dist < max_exact, dist, large).astype(np.int32)


def _bucket_tables():
    qi = np.arange(SPAN)[:, None]
    ki = np.arange(2 * SPAN)[None, :]
    delta = qi + SPAN - ki
    in_band = (delta >= 0) & (delta <= SPAN)
    prompt, sample = [], []
    for _, dilation in DIL_GROUPS:
        b = _t5_bucket_np(np.clip(delta, 0, SPAN) * dilation)
        prompt.append(np.where(in_band, b, -1))
        j = SPAN - np.arange(136)
        sample.append(_t5_bucket_np(np.maximum(j, 0) * dilation)[:, None])
    return np.stack(prompt).astype(np.int32), np.stack(sample).astype(np.int32)


def _bias_kernel(rb_smem, rb_ref, bp_ref, bs_ref, op_ref, os_ref):
    g = pl.program_id(0)
    bp = bp_ref[...]
    in_prev_block = lax.broadcasted_iota(jnp.int32, bp.shape, 1) < SPAN
    for h in range(HEADS):
        acc = jnp.full(bp.shape, NEG_INF, F32)
        for b in range(N_BUCKETS):
            acc = jnp.where(bp == b, rb_smem[b, g * HEADS + h] * LOG2E, acc)
        op_ref[0, h] = acc
        op_ref[1, h] = jnp.where(in_prev_block, NEG_INF, acc)
    bs = bs_ref[...]
    acc = jnp.zeros((bs.shape[0], rb_ref.shape[1]), F32)
    for b in range(N_BUCKETS):
        acc = jnp.where(bs == b, rb_ref[b:b + 1, :], acc)
    os_ref[...] = acc


def _bias_tables(rel_bias):
    bp, bs = _bucket_tables()
    n_cols = rel_bias.shape[1]
    return pl.pallas_call(
        _bias_kernel,
        grid=(N_GROUPS,),
        in_specs=[
            pl.BlockSpec(memory_space=pltpu.SMEM),
            pl.BlockSpec((N_BUCKETS, n_cols), lambda g: (0, 0)),
            pl.BlockSpec((None, SPAN, 2 * SPAN), lambda g: (g, 0, 0)),
            pl.BlockSpec((None, 136, 1), lambda g: (g, 0, 0)),
        ],
        out_specs=[
            pl.BlockSpec((None, 2, HEADS, SPAN, 2 * SPAN), lambda g: (g, 0, 0, 0, 0)),
            pl.BlockSpec((None, 136, n_cols), lambda g: (g, 0, 0)),
        ],
        out_shape=[
            jax.ShapeDtypeStruct((N_GROUPS, 2, HEADS, SPAN, 2 * SPAN), F32),
            jax.ShapeDtypeStruct((N_GROUPS, 136, n_cols), F32),
        ],
        compiler_params=_params(1),
        name="bias_tables",
    )(rel_bias, rel_bias, jnp.asarray(bp), jnp.asarray(bs))


def _mod_kernel(c_ref, w_ref, b_ref, o_ref):
    c = c_ref[...]
    a = (c * jax.nn.sigmoid(c)).astype(BF16)
    o_ref[...] = jnp.dot(a, w_ref[...].astype(BF16), preferred_element_type=F32) + b_ref[...]


def _modulation(c_all, w_mod, b_mod):
    depth, d, n = w_mod.shape
    rows = c_all.shape[0]
    tn = 1536
    return pl.pallas_call(
        _mod_kernel,
        grid=(depth, n // tn),
        in_specs=[
            pl.BlockSpec((rows, d), lambda l, j: (0, 0)),
            pl.BlockSpec((None, d, tn), lambda l, j: (l, 0, j)),
            pl.BlockSpec((None, 1, tn), lambda l, j: (l, 0, j)),
        ],
        out_specs=pl.BlockSpec((None, rows, tn), lambda l, j: (l, 0, j)),
        out_shape=jax.ShapeDtypeStruct((depth, rows, n), F32),
        compiler_params=_params(2),
        name="modulation",
    )(c_all, w_mod, b_mod.reshape(depth, 1, n))


def _normed(x_ref, g_ref, sc_ref, sh_ref):
    x = x_ref[...]
    y = x * lax.rsqrt(jnp.mean(x * x, axis=-1, keepdims=True) + EPS)
    return ((y * g_ref[...]) * (1.0 + sc_ref[...]) + sh_ref[...]).astype(BF16)


def _norm_mm_kernel(x_ref, g_ref, sc_ref, sh_ref, w_ref, o_ref, h_ref):
    @pl.when(pl.program_id(1) == 0)
    def _():
        h_ref[...] = _normed(x_ref, g_ref, sc_ref, sh_ref)
    o_ref[...] = jnp.dot(h_ref[...], w_ref[...], preferred_element_type=F32).astype(o_ref.dtype)


def _norm_glu_kernel(x_ref, g_ref, sc_ref, sh_ref, wa_ref, wg_ref, ba_ref, bg_ref, o_ref, h_ref):
    @pl.when(pl.program_id(1) == 0)
    def _():
        h_ref[...] = _normed(x_ref, g_ref, sc_ref, sh_ref)
    h = h_ref[...]
    a = jnp.dot(h, wa_ref[...], preferred_element_type=F32) + ba_ref[...]
    gate = jnp.dot(h, wg_ref[...], preferred_element_type=F32) + bg_ref[...]
    o_ref[...] = a * jax.nn.sigmoid(gate)


def _norm_swiglu_kernel(x_ref, g_ref, sc_ref, sh_ref, wg_ref, wu_ref, o_ref, h_ref):
    @pl.when(pl.program_id(1) == 0)
    def _():
        h_ref[...] = _normed(x_ref, g_ref, sc_ref, sh_ref)
    h = h_ref[...]
    gate = jnp.dot(h, wg_ref[...], preferred_element_type=F32)
    up = jnp.dot(h, wu_ref[...], preferred_element_type=F32)
    o_ref[...] = ((gate * jax.nn.sigmoid(gate)) * up).astype(o_ref.dtype)


class _Mod:
    def __init__(self, mod_all, layer, per_token, rows_per_seq):
        self.layer = layer
        self.per_token = per_token
        self.rows_per_seq = rows_per_seq
        depth, rows, n = mod_all.shape
        self.n_sample = rows - 8
        self.arr = mod_all if per_token else mod_all.reshape(depth, rows, 1, n)

    def spec(self, chunk, tm, row_of):
        layer = self.layer
        if self.per_token:
            return pl.BlockSpec((None, tm, D_MODEL), lambda *i: (layer, row_of(*i) // tm, chunk))
        first, per = self.n_sample, self.rows_per_seq
        return pl.BlockSpec((None, None, 1, D_MODEL), lambda *i: (layer, first + row_of(*i) // per, 0, chunk))


def _call_norm_matmul(kind, x, gain, mod, chunks, weights, biases, *, tm, tn, n_cols, out_dtype, name,
                      n_rows=None, row_of=None, out_shape=None, out_spec=None):
    t = x.shape[0] if n_rows is None else n_rows
    if row_of is None:
        row_of = lambda i, j: i * tm
    kernel = {"mm": _norm_mm_kernel, "glu": _norm_glu_kernel, "swiglu": _norm_swiglu_kernel}[kind]
    chunk_shift, chunk_scale = chunks
    in_specs = [
        pl.BlockSpec((tm, D_MODEL), lambda i, j: (row_of(i, j) // tm, 0)),
        pl.BlockSpec((1, D_MODEL), lambda i, j: (0, 0)),
        mod.spec(chunk_scale, tm, row_of),
        mod.spec(chunk_shift, tm, row_of),
    ]
    args = [x, gain.reshape(1, D_MODEL), mod.arr, mod.arr]
    for w, col0 in weights:
        in_specs.append(pl.BlockSpec((D_MODEL, tn), lambda i, j, col0=col0: (0, col0 + j)))
        args.append(w)
    for b, col0 in biases:
        in_specs.append(pl.BlockSpec((1, tn), lambda i, j, col0=col0: (0, col0 + j)))
        args.append(b)
    if out_shape is None:
        out_shape = jax.ShapeDtypeStruct((t, n_cols), out_dtype)
        out_spec = pl.BlockSpec((tm, tn), lambda i, j: (i, j))
    return pl.pallas_call(
        kernel,
        grid=(t // tm, n_cols // tn),
        in_specs=in_specs,
        out_specs=out_spec,
        out_shape=out_shape,
        scratch_shapes=[pltpu.VMEM((tm, D_MODEL), BF16)],
        compiler_params=_params(2),
        name=name,
    )(*args)


LANES = 128
N_LANE_SLABS = D_MODEL // LANES


def _qkv_prompt_kernel(x_ref, g_ref, sc_ref, sh_ref, w_ref, o0_ref, o1_ref, o2_ref, h_ref, slab_ref):
    tm = x_ref.shape[0]
    n = pl.program_id(1)
    o_refs = (o0_ref, o1_ref, o2_ref)

    @pl.when(n == 0)
    def _():
        x = x_ref[...]
        y = x * lax.rsqrt(jnp.mean(x * x, axis=-1, keepdims=True) + EPS)
        h = (y * g_ref[...]) * (1.0 + sc_ref[...]) + sh_ref[...]
        for c in range(N_LANE_SLABS):
            slab_ref[c] = h[:, c * LANES:(c + 1) * LANES]
        for gi, (_, dil) in enumerate(DIL_GROUPS):
            rows = tm // dil
            for r in range(dil):
                for c in range(N_LANE_SLABS):
                    src = slab_ref[c] if dil == 1 else slab_ref[c, pl.ds(r, rows, stride=dil), :]
                    h_ref[gi, r * rows:(r + 1) * rows, c * LANES:(c + 1) * LANES] = src.astype(BF16)

    for gi, (_, dil) in enumerate(DIL_GROUPS):
        @pl.when(n // 3 == gi)
        def _(gi=gi, dil=dil):
            acc = jnp.dot(h_ref[gi], w_ref[...], preferred_element_type=F32).astype(BF16)
            rows = tm // dil
            for r in range(dil):
                o_refs[gi][r] = acc[r * rows:(r + 1) * rows, :]


def _qkv_prompt(x, gain, mod, w, batch, seq, tm):
    tiles_per_seq = seq // tm
    row_of = lambda i, n: i * tm
    out_specs, out_shapes = [], []
    for g, (_, dil) in enumerate(DIL_GROUPS):
        out_specs.append(pl.BlockSpec(
            (None, dil, tm // dil, D_MODEL),
            lambda i, n, g=g: (i // tiles_per_seq, 0, i % tiles_per_seq, jnp.clip(n - 3 * g, 0, 2))))
        out_shapes.append(jax.ShapeDtypeStruct((batch, dil, seq // dil, 3 * D_MODEL), BF16))
    return pl.pallas_call(
        _qkv_prompt_kernel,
        grid=(batch * tiles_per_seq, 3 * N_GROUPS),
        in_specs=[
            pl.BlockSpec((tm, D_MODEL), lambda i, n: (i, 0)),
            pl.BlockSpec((1, D_MODEL), lambda i, n: (0, 0)),
            mod.spec(1, tm, row_of),
            mod.spec(0, tm, row_of),
            pl.BlockSpec((D_MODEL, D_MODEL), lambda i, n: (0, n)),
        ],
        out_specs=out_specs,
        out_shape=out_shapes,
        scratch_shapes=[pltpu.VMEM((N_GROUPS, tm, D_MODEL), BF16), pltpu.VMEM((N_LANE_SLABS, tm, LANES), F32)],
        compiler_params=_params(2),
        name="qkv_prompt",
    )(x, gain.reshape(1, D_MODEL), mod.arr, mod.arr, w)


def _mm_res_kernel(*refs, has_bias, final_norm):
    a_ref, w_ref = refs[0], refs[1]
    k = 2
    b_ref = None
    if has_bias:
        b_ref = refs[k]
        k += 1
    x_ref, gate_ref = refs[k], refs[k + 1]
    k += 2
    gf_ref = None
    if final_norm:
        gf_ref = refs[k]
        k += 1
    o_ref = refs[k]
    out = jnp.dot(a_ref[...], w_ref[...], preferred_element_type=F32)
    if has_bias:
        out = out + b_ref[...]
    x = x_ref[...] + gate_ref[...] * out
    if final_norm:
        x = (x * lax.rsqrt(jnp.mean(x * x, axis=-1, keepdims=True) + EPS)) * gf_ref[...]
    o_ref[...] = x


def _matmul_residual(a, w, bias, x, mod, chunk_gate, *, tm, name, g_final=None):
    t, k = a.shape
    row_of = lambda i: i * tm
    in_specs = [
        pl.BlockSpec((tm, k), lambda i: (i, 0)),
        pl.BlockSpec((k, D_MODEL), lambda i: (0, 0)),
    ]
    args = [a, w]
    if bias is not None:
        in_specs.append(pl.BlockSpec((1, D_MODEL), lambda i: (0, 0)))
        args.append(bias.reshape(1, D_MODEL))
    in_specs += [pl.BlockSpec((tm, D_MODEL), lambda i: (i, 0)), mod.spec(chunk_gate, tm, row_of)]
    args += [x, mod.arr]
    if g_final is not None:
        in_specs.append(pl.BlockSpec((1, D_MODEL), lambda i: (0, 0)))
        args.append(g_final.reshape(1, D_MODEL))
    return pl.pallas_call(
        functools.partial(_mm_res_kernel, has_bias=bias is not None, final_norm=g_final is not None),
        grid=(t // tm,),
        in_specs=in_specs,
        out_specs=pl.BlockSpec((tm, D_MODEL), lambda i: (i, 0)),
        out_shape=jax.ShapeDtypeStruct((t, D_MODEL), F32),
        compiler_params=_params(1),
        name=name,
    )(*args)


ATTN_TILES = 2


def _attn_prompt_kernel(q_ref, k_ref, v_ref, bias_ref, o_ref, lse_ref, kcat_ref, vcat_ref):
    step = pl.program_id(2)
    nq = q_ref.shape[0]

    @pl.when(step == 0)
    def _():
        kcat_ref[:SPAN] = jnp.zeros((SPAN, D_MODEL), BF16)
        vcat_ref[:SPAN] = jnp.zeros((SPAN, D_MODEL), BF16)

    @pl.when(step > 0)
    def _():
        kcat_ref[:SPAN] = kcat_ref[nq:]
        vcat_ref[:SPAN] = vcat_ref[nq:]

    kcat_ref[SPAN:] = k_ref[...]
    vcat_ref[SPAN:] = v_ref[...]

    starts_seq = jnp.where(step == 0, 1, 0)
    lane = lax.broadcasted_iota(jnp.int32, (SPAN, HEAD_DIM), 1)
    dims = (((1,), (1,)), ((), ()))
    work = [(j, h) for j in range(nq // SPAN) for h in range(HEADS)]
    rows = lambda j: slice(j * SPAN, (j + 1) * SPAN)
    keys = lambda j: slice(j * SPAN, (j + 2) * SPAN)
    cols = lambda h: slice(h * HEAD_DIM, (h + 1) * HEAD_DIM)

    s, m, p, l, o = {}, {}, {}, {}, {}
    for j, h in work:
        bias = bias_ref[starts_seq if j == 0 else 0, h]
        qk = lax.dot_general(q_ref[rows(j), cols(h)], kcat_ref[keys(j), cols(h)], dims, preferred_element_type=F32)
        s[j, h] = qk * (SCALE * LOG2E) + bias
    for j, h in work:
        m[j, h] = jnp.max(jnp.maximum(s[j, h][:, :SPAN], s[j, h][:, SPAN:]), axis=-1, keepdims=True)
    for j, h in work:
        e = jnp.exp2(s[j, h] - m[j, h])
        l[j, h] = jnp.sum(e[:, :SPAN] + e[:, SPAN:], axis=-1, keepdims=True)
        p[j, h] = e.astype(BF16)
    for j, h in work:
        o[j, h] = jnp.dot(p[j, h], vcat_ref[keys(j), cols(h)], preferred_element_type=F32)
    for j in range(nq // SPAN):
        lse_all = jnp.zeros((SPAN, HEAD_DIM), F32)
        for h in range(HEADS):
            o_ref[rows(j), cols(h)] = o[j, h] / l[j, h]
            lse_all = jnp.where(lane == h, (m[j, h] + jnp.log2(l[j, h])) * LN2, lse_all)
        lse_ref[rows(j), :] = lse_all


def _attn_prompt(qkv_g, bias_p, g):
    batch, dil, sub, _ = qkv_g.shape
    nq = ATTN_TILES * SPAN
    blk = (None, None, nq, D_MODEL)
    return pl.pallas_call(
        _attn_prompt_kernel,
        grid=(batch, dil, sub // nq),
        in_specs=[
            pl.BlockSpec(blk, lambda b, r, t: (b, r, t, 0)),
            pl.BlockSpec(blk, lambda b, r, t: (b, r, t, 1)),
            pl.BlockSpec(blk, lambda b, r, t: (b, r, t, 2)),
            pl.BlockSpec((None, 2, HEADS, SPAN, 2 * SPAN), lambda b, r, t: (g, 0, 0, 0, 0)),
        ],
        out_specs=[
            pl.BlockSpec(blk, lambda b, r, t: (b, r, t, 0)),
            pl.BlockSpec((None, None, nq, HEAD_DIM), lambda b, r, t: (b, r, t, 0)),
        ],
        out_shape=[
            jax.ShapeDtypeStruct((batch, dil, sub, D_MODEL), F32),
            jax.ShapeDtypeStruct((batch, dil, sub, HEAD_DIM), F32),
        ],
        scratch_shapes=[pltpu.VMEM((SPAN + nq, D_MODEL), BF16), pltpu.VMEM((SPAN + nq, D_MODEL), BF16)],
        compiler_params=_params(3),
        name=f"attn_prompt_g{g}",
    )(qkv_g, qkv_g, qkv_g, bias_p)


def _attn_sample_kernel(qkv_ref, k0_ref, v0_ref, k1_ref, v1_ref, k2_ref, v2_ref, bias_ref, a_ref, *, bb):
    kv_refs = ((k0_ref, v0_ref), (k1_ref, v1_ref), (k2_ref, v2_ref))
    for bi in range(bb):
        outs, lses = [], []
        for g, (k_ref, v_ref) in enumerate(kv_refs):
            q, k_new, v_new = qkv_ref[bi, 3 * g], qkv_ref[bi, 3 * g + 1], qkv_ref[bi, 3 * g + 2]
            s = jnp.sum(k_ref[bi] * q[None], axis=-1, keepdims=True) * SCALE + bias_ref[g, :SPAN]
            s_new = jnp.sum(k_new * q, axis=-1, keepdims=True) * SCALE + bias_ref[g, SPAN]
            m = jnp.maximum(jnp.max(s, axis=0), s_new)
            e = jnp.exp(s - m[None])
            e_new = jnp.exp(s_new - m)
            l = jnp.sum(e, axis=0) + e_new
            outs.append((jnp.sum(e * v_ref[bi], axis=0) + e_new * v_new) / l)
            lses.append(m + jnp.log(l))
        top = jnp.maximum(jnp.maximum(lses[0], lses[1]), lses[2])
        ws = [jnp.exp(lse - top) for lse in lses]
        den = ws[0] + ws[1] + ws[2]
        a_ref[bi] = (outs[0] * (ws[0] / den) + outs[1] * (ws[1] / den)) + outs[2] * (ws[2] / den)


def _attn_sample(qkv, caches, layer, bias_s):
    b = qkv.shape[0]
    bb = 4
    in_specs = [pl.BlockSpec((bb,) + qkv.shape[1:], lambda i: (i, 0, 0, 0))]
    args = [qkv]
    for g, (_, dil) in enumerate(DIL_GROUPS):
        cache = caches[g]
        assert cache.shape[3] == SPAN * dil
        view = cache.reshape(cache.shape[:3] + (SPAN, dil, HEADS, HEAD_DIM))
        for kv in range(2):
            in_specs.append(pl.BlockSpec((None, bb, None, SPAN, None, HEADS, HEAD_DIM),
                                         lambda i, kv=kv: (layer, i, kv, 0, 0, 0, 0)))
            args.append(view)
    in_specs.append(pl.BlockSpec(bias_s.shape, lambda i: (0, 0, 0, 0)))
    args.append(bias_s)
    return pl.pallas_call(
        functools.partial(_attn_sample_kernel, bb=bb),
        grid=(b // bb,),
        in_specs=in_specs,
        out_specs=pl.BlockSpec((bb, HEADS, HEAD_DIM), lambda i: (i, 0, 0)),
        out_shape=jax.ShapeDtypeStruct((b, HEADS, HEAD_DIM), F32),
        compiler_params=_params(1),
        name="attn_sample",
    )(*args)


def _combine_kernel(o0_ref, o1_ref, o2_ref, l0_ref, l1_ref, l2_ref, a_ref, os_ref, ls_ref):
    tm = a_ref.shape[0]
    for gi, o_ref, l_ref in ((1, o1_ref, l1_ref), (2, o2_ref, l2_ref)):
        dil = DIL_GROUPS[gi][1]
        rows = tm // dil
        for r in range(dil):
            ls_ref[gi - 1, pl.ds(r, rows, stride=dil), :] = l_ref[r]
            for h in range(HEADS):
                os_ref[gi - 1, h, pl.ds(r, rows, stride=dil), :] = o_ref[r, :, h * HEAD_DIM:(h + 1) * HEAD_DIM]
    l0, l1, l2 = l0_ref[0], ls_ref[0], ls_ref[1]
    m = jnp.maximum(jnp.maximum(l0, l1), l2)
    e0, e1, e2 = jnp.exp(l0 - m), jnp.exp(l1 - m), jnp.exp(l2 - m)
    den = e0 + e1 + e2
    w0, w1, w2 = e0 / den, e1 / den, e2 / den
    for h in range(HEADS):
        sl = slice(h * HEAD_DIM, (h + 1) * HEAD_DIM)
        o = (o0_ref[0, :, sl] * w0[:, h:h + 1] + os_ref[0, h] * w1[:, h:h + 1]) + os_ref[1, h] * w2[:, h:h + 1]
        a_ref[:, sl] = o.astype(a_ref.dtype)


def _combine(outs, lses, seq, tm):
    batch = outs[0].shape[0]
    tiles_per_seq = seq // tm
    in_specs = []
    for width in (D_MODEL, HEAD_DIM):
        for _, dil in DIL_GROUPS:
            in_specs.append(pl.BlockSpec((None, dil, tm // dil, width),
                                         lambda i: (i // tiles_per_seq, 0, i % tiles_per_seq, 0)))
    return pl.pallas_call(
        _combine_kernel,
        grid=(batch * tiles_per_seq,),
        in_specs=in_specs,
        out_specs=pl.BlockSpec((tm, D_MODEL), lambda i: (i, 0)),
        out_shape=jax.ShapeDtypeStruct((batch * seq, D_MODEL), BF16),
        scratch_shapes=[pltpu.VMEM((N_GROUPS - 1, HEADS, tm, HEAD_DIM), F32),
                        pltpu.VMEM((N_GROUPS - 1, tm, HEAD_DIM), F32)],
        compiler_params=_params(1),
        name="combine_groups",
    )(*outs, *lses)


def _ln_silu(z, g_ref, b_ref):
    mu = jnp.mean(z, axis=-1, keepdims=True)
    zc = z - mu
    var = jnp.mean(zc * zc, axis=-1, keepdims=True)
    y = (zc * lax.rsqrt(var + EPS)) * g_ref[...] + b_ref[...]
    return y * jax.nn.sigmoid(y)


SUBLANES = 8
HALO = 32
CONV_ROWS = 64
CONV_LANES = 256


def _conv_prompt_kernel(u_ref, prev_ref, w_ref, bdw_ref, g_ref, b_ref, a_ref, ext_ref, z_ref, *, tiles_per_seq):
    tm = u_ref.shape[0]
    starts_seq = pl.program_id(0) % tiles_per_seq == 0
    ext_ref[0, :HALO, :] = jnp.where(starts_seq, 0.0, prev_ref[...])
    ext_ref[0, HALO:, :] = u_ref[...]
    n_shifted = tm + HALO - SUBLANES
    for s in range(1, SUBLANES):
        for c0 in range(0, D_MODEL, CONV_LANES):
            cs = slice(c0, c0 + CONV_LANES)
            ext_ref[s, :n_shifted, cs] = ext_ref[0, s:s + n_shifted, cs]
    off = HALO - (CONV_WIDTH - 1)
    for r0 in range(0, tm, CONV_ROWS):
        for c0 in range(0, D_MODEL, CONV_LANES):
            cs = slice(c0, c0 + CONV_LANES)
            acc = jnp.zeros((CONV_ROWS, CONV_LANES), F32)
            for k in range(CONV_WIDTH):
                shift = (r0 + off + k) % SUBLANES
                base = r0 + off + k - shift
                acc = acc + ext_ref[shift, base:base + CONV_ROWS, cs] * w_ref[k:k + 1, cs]
            z_ref[r0:r0 + CONV_ROWS, cs] = acc + bdw_ref[:, cs]
    a_ref[...] = _ln_silu(z_ref[...], g_ref, b_ref).astype(a_ref.dtype)


def _conv_prompt(u, w_dw, b_dw, ln_g, ln_b, seq, tm):
    t = u.shape[0]
    row = lambda v: v.reshape(1, D_MODEL)
    vec = pl.BlockSpec((1, D_MODEL), lambda i: (0, 0))
    return pl.pallas_call(
        functools.partial(_conv_prompt_kernel, tiles_per_seq=seq // tm),
        grid=(t // tm,),
        in_specs=[
            pl.BlockSpec((tm, D_MODEL), lambda i: (i, 0)),
            pl.BlockSpec((HALO, D_MODEL), lambda i: (jnp.maximum(i * (tm // HALO) - 1, 0), 0)),
            pl.BlockSpec((CONV_WIDTH, D_MODEL), lambda i: (0, 0)),
            vec, vec, vec,
        ],
        out_specs=pl.BlockSpec((tm, D_MODEL), lambda i: (i, 0)),
        out_shape=jax.ShapeDtypeStruct((t, D_MODEL), BF16),
        scratch_shapes=[pltpu.VMEM((SUBLANES, tm + HALO, D_MODEL), F32), pltpu.VMEM((tm, D_MODEL), F32)],
        compiler_params=_params(1),
        name="conv_prompt",
    )(u, u, w_dw, row(b_dw), row(ln_g), row(ln_b))


def _conv_sample_kernel(u_ref, st_ref, w_ref, bdw_ref, g_ref, b_ref, a_ref):
    n_state = st_ref.shape[1]
    z = jnp.sum(st_ref[...] * w_ref[:n_state, :][None], axis=1)
    z = z + u_ref[...] * w_ref[n_state:n_state + 1, :] + bdw_ref[...]
    a_ref[...] = _ln_silu(z, g_ref, b_ref).astype(a_ref.dtype)


def _conv_sample(u, state, w_dw, b_dw, ln_g, ln_b):
    b, n_state, _ = state.shape
    bb = 16
    row = lambda v: v.reshape(1, D_MODEL)
    vec = pl.BlockSpec((1, D_MODEL), lambda i: (0, 0))
    return pl.pallas_call(
        _conv_sample_kernel,
        grid=(b // bb,),
        in_specs=[
            pl.BlockSpec((bb, D_MODEL), lambda i: (i, 0)),
            pl.BlockSpec((bb, n_state, D_MODEL), lambda i: (i, 0, 0)),
            pl.BlockSpec((CONV_WIDTH, D_MODEL), lambda i: (0, 0)),
            vec, vec, vec,
        ],
        out_specs=pl.BlockSpec((bb, D_MODEL), lambda i: (i, 0)),
        out_shape=jax.ShapeDtypeStruct((b, D_MODEL), BF16),
        compiler_params=_params(1),
        name="conv_sample",
    )(u, state, w_dw, row(b_dw), row(ln_g), row(ln_b))


def _ffn(x, mod, g_ffn, w_gate, w_up, w_down, *, tm_up, tm_down, tag, g_final=None):
    u = _call_norm_matmul("swiglu", x, g_ffn, mod, (3, 4), [(w_gate, 0), (w_up, 0)], [],
                          tm=tm_up, tn=FFN_HIDDEN // 2, n_cols=FFN_HIDDEN, out_dtype=BF16,
                          name=f"ffn_up_{tag}")
    return _matmul_residual(u, w_down, None, x, mod, 5, tm=tm_down, name=f"ffn_down_{tag}", g_final=g_final)


def kernel(x_prompt, x_sample, cache_kv_w128, cache_kv_w512, cache_kv_w2048, state_conv, c_prompt, c_sample,
           w_mod, b_mod, g_mix, g_ffn, g_final, w_qkv, w_o, rel_bias, w_pw1, b_pw1, w_dw, b_dw, ln_g, ln_b,
           w_pw2, b_pw2, w_gate, w_up, w_down):
    batch, seq, d = x_prompt.shape
    dec_batch = x_sample.shape[0]
    assert d == D_MODEL and x_sample.shape[1] == 1
    caches = (cache_kv_w128, cache_kv_w512, cache_kv_w2048)

    bf = lambda w: w.astype(BF16)
    w_qkv, w_o, w_pw1, w_pw2, w_gate, w_up, w_down = map(bf, (w_qkv, w_o, w_pw1, w_pw2, w_gate, w_up, w_down))

    c_all = jnp.concatenate([c_sample, c_prompt, jnp.zeros((8 - batch, d), F32)], axis=0)
    mod_all = _modulation(c_all, w_mod, b_mod)
    bias_p, bias_s = _bias_tables(rel_bias)

    xp = x_prompt.reshape(batch * seq, d)
    xs = x_sample.reshape(dec_batch, d)
    n_qkv = 3 * N_GROUPS * d

    mod_p = _Mod(mod_all, 0, False, seq)
    mod_s = _Mod(mod_all, 0, True, 1)

    qkv_p = _qkv_prompt(xp, g_mix[0], mod_p, w_qkv[0], batch, seq, 1024)
    qkv_s = _call_norm_matmul("mm", xs, g_mix[0], mod_s, (0, 1), [(w_qkv[0], 0)], [],
                              tm=dec_batch, tn=1536, n_cols=n_qkv, out_dtype=F32, name="qkv_sample")

    kv_prompt = []
    for g, (window, _) in enumerate(DIL_GROUPS):
        keep = min(window, seq)
        tmk = min(keep, 1024)
        per_seq = keep // tmk
        row_of = lambda i, j, per_seq=per_seq, tmk=tmk, keep=keep: (i // per_seq) * seq + seq - keep + (i % per_seq) * tmk
        kv = _call_norm_matmul(
            "mm", xp, g_mix[0], mod_p, (0, 1), [(w_qkv[0], 3 * g + 1)], [],
            tm=tmk, tn=D_MODEL, n_cols=2 * D_MODEL, out_dtype=F32, name=f"kv_tail_g{g}",
            n_rows=batch * keep, row_of=row_of,
            out_shape=jax.ShapeDtypeStruct((batch, 2, keep, D_MODEL), F32),
            out_spec=pl.BlockSpec((None, None, tmk, D_MODEL),
                                  lambda i, j, per_seq=per_seq: (i // per_seq, j, i % per_seq, 0)))
        kv_prompt.append(kv.reshape(1, batch, 2, keep, HEADS, HEAD_DIM))

    outs, lses = zip(*[_attn_prompt(qkv_p[g], bias_p, g) for g in range(N_GROUPS)])
    a_p = _combine(outs, lses, seq, 512)
    xp = _matmul_residual(a_p, w_o[0], None, xp, mod_p, 2, tm=1024, name="attn_out_prompt")
    xp = _ffn(xp, mod_p, g_ffn[0], w_gate[0], w_up[0], w_down[0], tm_up=1024, tm_down=512, tag="prompt0")

    qkv_s4 = qkv_s.reshape(dec_batch, 3 * N_GROUPS, HEADS, HEAD_DIM)
    bias_s4 = jnp.stack([bias_s[g, :, g * HEADS:(g + 1) * HEADS] for g in range(N_GROUPS)])[..., None]
    a_s = _attn_sample(qkv_s4, caches, 0, bias_s4).reshape(dec_batch, d).astype(BF16)
    xs = _matmul_residual(a_s, w_o[0], None, xs, mod_s, 2, tm=dec_batch, name="attn_out_sample")
    xs = _ffn(xs, mod_s, g_ffn[0], w_gate[0], w_up[0], w_down[0], tm_up=dec_batch, tm_down=dec_batch,
              tag="sample0")

    mod_p = _Mod(mod_all, 1, False, seq)
    mod_s = _Mod(mod_all, 1, True, 1)
    b_pw1_row = b_pw1[0].reshape(1, 2 * d)

    def glu(x, mod, tm, tag):
        return _call_norm_matmul("glu", x, g_mix[1], mod, (0, 1), [(w_pw1[0], 0), (w_pw1[0], d // 512)],
                                 [(b_pw1_row, 0), (b_pw1_row, d // 512)],
                                 tm=tm, tn=512, n_cols=d, out_dtype=F32, name=f"glu_{tag}")

    u_p = glu(xp, mod_p, 1024, "prompt")
    a_p = _conv_prompt(u_p, w_dw[0], b_dw[0], ln_g[0], ln_b[0], seq, 256)
    xp = _matmul_residual(a_p, w_pw2[0], b_pw2[0], xp, mod_p, 2, tm=1024, name="conv_out_prompt")
    y_p = _ffn(xp, mod_p, g_ffn[1], w_gate[1], w_up[1], w_down[1], tm_up=1024, tm_down=512, tag="prompt1",
               g_final=g_final)

    u_s = glu(xs, mod_s, dec_batch, "sample")
    a_s = _conv_sample(u_s, state_conv[0], w_dw[0], b_dw[0], ln_g[0], ln_b[0])
    xs = _matmul_residual(a_s, w_pw2[0], b_pw2[0], xs, mod_s, 2, tm=dec_batch, name="conv_out_sample")
    y_s = _ffn(xs, mod_s, g_ffn[1], w_gate[1], w_up[1], w_down[1], tm_up=dec_batch, tm_down=dec_batch,
               tag="sample1", g_final=g_final)

    conv_p = u_p.reshape(batch, seq, d)[:, seq - (CONV_WIDTH - 1):][None]
    conv_s = jnp.concatenate([state_conv[0][:, 1:], u_s[:, None, :]], axis=1)[None]
    kv_sample = [qkv_s4[:, 3 * g + 1:3 * g + 3].reshape(1, dec_batch, 2, 1, HEADS, HEAD_DIM)
                 for g in range(N_GROUPS)]
    return (y_p.reshape(batch, seq, d), y_s.reshape(dec_batch, 1, d),
            kv_prompt[0], kv_prompt[1], kv_prompt[2], conv_p,
            kv_sample[0], kv_sample[1], kv_sample[2], conv_s)
```

```python
import functools
import math

import numpy as np
import jax
import jax.numpy as jnp
from jax import lax
from jax.experimental import pallas as pl
from jax.experimental.pallas import tpu as pltpu

D_MODEL = 1024
DIL_GROUPS = ((128, 1), (512, 4), (2048, 16))
N_GROUPS = len(DIL_GROUPS)
HEADS = 8
HEAD_DIM = 128
SPAN = 128
N_BUCKETS = 32
MAX_DISTANCE = 2048
CONV_WIDTH = 31
FFN_HIDDEN = 2816
EPS = 1e-6
NEG_INF = -1e30
SCALE = HEAD_DIM ** -0.5
LOG2E = math.log2(math.e)
LN2 = math.log(2.0)

F32 = jnp.float32
BF16 = jnp.bfloat16

VMEM_LIMIT_BYTES = 56 * 1024 * 1024


def _params(n_axes, vmem=VMEM_LIMIT_BYTES):
    return pltpu.CompilerParams(dimension_semantics=("arbitrary",) * n_axes, vmem_limit_bytes=vmem)


def _t5_bucket_np(dist):
    max_exact = N_BUCKETS // 2
    n = np.maximum(dist, 1).astype(np.float32)
    large = max_exact + (np.log(n / np.float32(max_exact)) / np.float32(math.log(MAX_DISTANCE / max_exact))
                         * np.float32(N_BUCKETS - max_exact)).astype(np.int32)
    large = np.minimum(large, N_BUCKETS - 1)
    return np.where(dist < max_exact, dist, large).astype(np.int32)


def _bucket_tables():
    qi = np.arange(SPAN)[:, None]
    ki = np.arange(2 * SPAN)[None, :]
    delta = qi + SPAN - ki
    in_band = (delta >= 0) & (delta <= SPAN)
    prompt, sample = [], []
    for _, dilation in DIL_GROUPS:
        b = _t5_bucket_np(np.clip(delta, 0, SPAN) * dilation)
        prompt.append(np.where(in_band, b, -1))
        j = SPAN - np.arange(136)
        sample.append(_t5_bucket_np(np.maximum(j, 0) * dilation)[:, None])
    return np.stack(prompt).astype(np.int32), np.stack(sample).astype(np.int32)


def _bias_kernel(rb_smem, rb_ref, bp_ref, bs_ref, op_ref, os_ref):
    g = pl.program_id(0)
    bp = bp_ref[...]
    in_prev_block = lax.broadcasted_iota(jnp.int32, bp.shape, 1) < SPAN
    for h in range(HEADS):
        acc = jnp.full(bp.shape, NEG_INF, F32)
        for b in range(N_BUCKETS):
            acc = jnp.where(bp == b, rb_smem[b, g * HEADS + h] * LOG2E, acc)
        op_ref[0, h] = acc
        op_ref[1, h] = jnp.where(in_prev_block, NEG_INF, acc)
    bs = bs_ref[...]
    acc = jnp.zeros((bs.shape[0], rb_ref.shape[1]), F32)
    for b in range(N_BUCKETS):
        acc = jnp.where(bs == b, rb_ref[b:b + 1, :], acc)
    os_ref[...] = acc


def _bias_tables(rel_bias):
    bp, bs = _bucket_tables()
    n_cols = rel_bias.shape[1]
    return pl.pallas_call(
        _bias_kernel,
        grid=(N_GROUPS,),
        in_specs=[
            pl.BlockSpec(memory_space=pltpu.SMEM),
            pl.BlockSpec((N_BUCKETS, n_cols), lambda g: (0, 0)),
            pl.BlockSpec((None, SPAN, 2 * SPAN), lambda g: (g, 0, 0)),
            pl.BlockSpec((None, 136, 1), lambda g: (g, 0, 0)),
        ],
        out_specs=[
            pl.BlockSpec((None, 2, HEADS, SPAN, 2 * SPAN), lambda g: (g, 0, 0, 0, 0)),
            pl.BlockSpec((None, 136, n_cols), lambda g: (g, 0, 0)),
        ],
        out_shape=[
            jax.ShapeDtypeStruct((N_GROUPS, 2, HEADS, SPAN, 2 * SPAN), F32),
            jax.ShapeDtypeStruct((N_GROUPS, 136, n_cols), F32),
        ],
        compiler_params=_params(1),
        name="bias_tables",
    )(rel_bias, rel_bias, jnp.asarray(bp), jnp.asarray(bs))


def _mod_kernel(c_ref, w_ref, b_ref, o_ref):
    c = c_ref[...]
    a = (c * jax.nn.sigmoid(c)).astype(BF16)
    o_ref[...] = jnp.dot(a, w_ref[...].astype(BF16), preferred_element_type=F32) + b_ref[...]


def _modulation(c_all, w_mod, b_mod):
    depth, d, n = w_mod.shape
    rows = c_all.shape[0]
    tn = 1536
    return pl.pallas_call(
        _mod_kernel,
        grid=(depth, n // tn),
        in_specs=[
            pl.BlockSpec((rows, d), lambda l, j: (0, 0)),
            pl.BlockSpec((None, d, tn), lambda l, j: (l, 0, j)),
            pl.BlockSpec((None, 1, tn), lambda l, j: (l, 0, j)),
        ],
        out_specs=pl.BlockSpec((None, rows, tn), lambda l, j: (l, 0, j)),
        out_shape=jax.ShapeDtypeStruct((depth, rows, n), F32),
        compiler_params=_params(2),
        name="modulation",
    )(c_all, w_mod, b_mod.reshape(depth, 1, n))


def _normed(x_ref, g_ref, sc_ref, sh_ref):
    x = x_ref[...]
    y = x * lax.rsqrt(jnp.mean(x * x, axis=-1, keepdims=True) + EPS)
    return ((y * g_ref[...]) * (1.0 + sc_ref[...]) + sh_ref[...]).astype(BF16)


def _norm_mm_kernel(x_ref, g_ref, sc_ref, sh_ref, w_ref, o_ref, h_ref):
    @pl.when(pl.program_id(1) == 0)
    def _():
        h_ref[...] = _normed(x_ref, g_ref, sc_ref, sh_ref)
    o_ref[...] = jnp.dot(h_ref[...], w_ref[...], preferred_element_type=F32).astype(o_ref.dtype)


def _norm_glu_kernel(x_ref, g_ref, sc_ref, sh_ref, wa_ref, wg_ref, ba_ref, bg_ref, o_ref, h_ref):
    @pl.when(pl.program_id(1) == 0)
    def _():
        h_ref[...] = _normed(x_ref, g_ref, sc_ref, sh_ref)
    h = h_ref[...]
    a = jnp.dot(h, wa_ref[...], preferred_element_type=F32) + ba_ref[...]
    gate = jnp.dot(h, wg_ref[...], preferred_element_type=F32) + bg_ref[...]
    o_ref[...] = a * jax.nn.sigmoid(gate)


def _norm_swiglu_kernel(x_ref, g_ref, sc_ref, sh_ref, wg_ref, wu_ref, o_ref, h_ref):
    @pl.when(pl.program_id(1) == 0)
    def _():
        h_ref[...] = _normed(x_ref, g_ref, sc_ref, sh_ref)
    h = h_ref[...]
    gate = jnp.dot(h, wg_ref[...], preferred_element_type=F32)
    up = jnp.dot(h, wu_ref[...], preferred_element_type=F32)
    o_ref[...] = ((gate * jax.nn.sigmoid(gate)) * up).astype(o_ref.dtype)


class _Mod:
    def __init__(self, mod_all, layer, per_token, rows_per_seq):
        self.layer = layer
        self.per_token = per_token
        self.rows_per_seq = rows_per_seq
        depth, rows, n = mod_all.shape
        self.n_sample = rows - 8
        self.arr = mod_all if per_token else mod_all.reshape(depth, rows, 1, n)

    def spec(self, chunk, tm, row_of):
        layer = self.layer
        if self.per_token:
            return pl.BlockSpec((None, tm, D_MODEL), lambda *i: (layer, row_of(*i) // tm, chunk))
        first, per = self.n_sample, self.rows_per_seq
        return pl.BlockSpec((None, None, 1, D_MODEL), lambda *i: (layer, first + row_of(*i) // per, 0, chunk))


def _call_norm_matmul(kind, x, gain, mod, chunks, weights, biases, *, tm, tn, n_cols, out_dtype, name):
    t = x.shape[0]
    row_of = lambda i, j: i * tm
    kernel = {"mm": _norm_mm_kernel, "glu": _norm_glu_kernel, "swiglu": _norm_swiglu_kernel}[kind]
    chunk_shift, chunk_scale = chunks
    in_specs = [
        pl.BlockSpec((tm, D_MODEL), lambda i, j: (i, 0)),
        pl.BlockSpec((1, D_MODEL), lambda i, j: (0, 0)),
        mod.spec(chunk_scale, tm, row_of),
        mod.spec(chunk_shift, tm, row_of),
    ]
    args = [x, gain.reshape(1, D_MODEL), mod.arr, mod.arr]
    resident = dict(pipeline_mode=pl.Buffered(1)) if n_cols == tn else {}
    for w, col0 in weights:
        in_specs.append(pl.BlockSpec((D_MODEL, tn), lambda i, j, col0=col0: (0, col0 + j), **resident))
        args.append(w)
    for b, col0 in biases:
        in_specs.append(pl.BlockSpec((1, tn), lambda i, j, col0=col0: (0, col0 + j)))
        args.append(b)
    return pl.pallas_call(
        kernel,
        grid=(t // tm, n_cols // tn),
        in_specs=in_specs,
        out_specs=pl.BlockSpec((tm, tn), lambda i, j: (i, j)),
        out_shape=jax.ShapeDtypeStruct((t, n_cols), out_dtype),
        scratch_shapes=[pltpu.VMEM((tm, D_MODEL), BF16)],
        compiler_params=_params(2),
        name=name,
    )(*args)


def _mm_kernel(a_ref, w_ref, o_ref):
    o_ref[...] = jnp.dot(a_ref[...], w_ref[...], preferred_element_type=F32)


def _kv_tail(h, w_qkv, g, batch, seq, keep):
    tm = min(keep, 1024)
    per_seq = keep // tm
    first = (seq - keep) // tm
    return pl.pallas_call(
        _mm_kernel,
        grid=(batch * per_seq, 2),
        in_specs=[
            pl.BlockSpec((tm, D_MODEL), lambda i, j: ((i // per_seq) * (seq // tm) + first + i % per_seq, 0)),
            pl.BlockSpec((D_MODEL, D_MODEL), lambda i, j: (0, 3 * g + 1 + j)),
        ],
        out_specs=pl.BlockSpec((None, None, tm, D_MODEL), lambda i, j: (i // per_seq, j, i % per_seq, 0)),
        out_shape=jax.ShapeDtypeStruct((batch, 2, keep, D_MODEL), F32),
        compiler_params=_params(2),
        name=f"kv_tail_g{g}",
    )(h, w_qkv)


LANES = 128
N_LANE_SLABS = D_MODEL // LANES


def _qkv_prompt_kernel(x_ref, g_ref, sc_ref, sh_ref, w_ref, o0_ref, o1_ref, o2_ref, hn_ref, h_ref, slab_ref):
    tm = x_ref.shape[0]
    n = pl.program_id(1)
    o_refs = (o0_ref, o1_ref, o2_ref)

    @pl.when(n == 0)
    def _():
        x = x_ref[...]
        y = x * lax.rsqrt(jnp.mean(x * x, axis=-1, keepdims=True) + EPS)
        h = (y * g_ref[...]) * (1.0 + sc_ref[...]) + sh_ref[...]
        hn_ref[...] = h.astype(BF16)
        h_ref[0] = h.astype(BF16)
        quarter, sixteenth = tm // 4, tm // 16
        for c in range(N_LANE_SLABS):
            lanes = slice(c * LANES, (c + 1) * LANES)
            slab_ref[0, c] = h[:, lanes]
            for b in range(4):
                part = slab_ref[0, c, pl.ds(b, quarter, stride=4), :]
                slab_ref[1, c, b * quarter:(b + 1) * quarter, :] = part
                h_ref[1, b * quarter:(b + 1) * quarter, lanes] = part.astype(BF16)
            for b in range(4):
                for a in range(4):
                    r = 4 * a + b
                    part = slab_ref[1, c, pl.ds(b * quarter + a, sixteenth, stride=4), :]
                    h_ref[2, r * sixteenth:(r + 1) * sixteenth, lanes] = part.astype(BF16)

    for gi, (_, dil) in enumerate(DIL_GROUPS):
        @pl.when(n // 3 == gi)
        def _(gi=gi, dil=dil):
            acc = jnp.dot(h_ref[gi], w_ref[...], preferred_element_type=F32).astype(BF16)
            rows = tm // dil
            for r in range(dil):
                o_refs[gi][r] = acc[r * rows:(r + 1) * rows, :]


def _qkv_prompt(x, gain, mod, w, batch, seq, tm):
    assert [dil for _, dil in DIL_GROUPS] == [1, 4, 16]
    tiles_per_seq = seq // tm
    row_of = lambda i, n: i * tm
    out_specs, out_shapes = [], []
    for g, (_, dil) in enumerate(DIL_GROUPS):
        out_specs.append(pl.BlockSpec(
            (None, dil, tm // dil, D_MODEL),
            lambda i, n, g=g: (i // tiles_per_seq, 0, i % tiles_per_seq, jnp.clip(n - 3 * g, 0, 2))))
        out_shapes.append(jax.ShapeDtypeStruct((batch, dil, seq // dil, 3 * D_MODEL), BF16))
    out_specs.append(pl.BlockSpec((tm, D_MODEL), lambda i, n: (i, 0)))
    out_shapes.append(jax.ShapeDtypeStruct((batch * seq, D_MODEL), BF16))
    return pl.pallas_call(
        _qkv_prompt_kernel,
        grid=(batch * tiles_per_seq, 3 * N_GROUPS),
        in_specs=[
            pl.BlockSpec((tm, D_MODEL), lambda i, n: (i, 0)),
            pl.BlockSpec((1, D_MODEL), lambda i, n: (0, 0)),
            mod.spec(1, tm, row_of),
            mod.spec(0, tm, row_of),
            pl.BlockSpec((D_MODEL, D_MODEL), lambda i, n: (0, n)),
        ],
        out_specs=out_specs,
        out_shape=out_shapes,
        scratch_shapes=[pltpu.VMEM((N_GROUPS, tm, D_MODEL), BF16),
                        pltpu.VMEM((2, N_LANE_SLABS, tm, LANES), F32)],
        compiler_params=_params(2),
        name="qkv_prompt",
    )(x, gain.reshape(1, D_MODEL), mod.arr, mod.arr, w)


def _mm_res_kernel(*refs, has_bias, final_norm):
    a_ref, w_ref = refs[0], refs[1]
    k = 2
    b_ref = None
    if has_bias:
        b_ref = refs[k]
        k += 1
    x_ref, gate_ref = refs[k], refs[k + 1]
    k += 2
    gf_ref = None
    if final_norm:
        gf_ref = refs[k]
        k += 1
    o_ref = refs[k]
    out = jnp.dot(a_ref[...], w_ref[...], preferred_element_type=F32)
    if has_bias:
        out = out + b_ref[...]
    x = x_ref[...] + gate_ref[...] * out
    if final_norm:
        x = (x * lax.rsqrt(jnp.mean(x * x, axis=-1, keepdims=True) + EPS)) * gf_ref[...]
    o_ref[...] = x


def _matmul_residual(a, w, bias, x, mod, chunk_gate, *, tm, name, g_final=None):
    t, k = a.shape
    row_of = lambda i: i * tm
    in_specs = [
        pl.BlockSpec((tm, k), lambda i: (i, 0)),
        pl.BlockSpec((k, D_MODEL), lambda i: (0, 0)),
    ]
    args = [a, w]
    if bias is not None:
        in_specs.append(pl.BlockSpec((1, D_MODEL), lambda i: (0, 0)))
        args.append(bias.reshape(1, D_MODEL))
    in_specs += [pl.BlockSpec((tm, D_MODEL), lambda i: (i, 0)), mod.spec(chunk_gate, tm, row_of)]
    args += [x, mod.arr]
    if g_final is not None:
        in_specs.append(pl.BlockSpec((1, D_MODEL), lambda i: (0, 0)))
        args.append(g_final.reshape(1, D_MODEL))
    return pl.pallas_call(
        functools.partial(_mm_res_kernel, has_bias=bias is not None, final_norm=g_final is not None),
        grid=(t // tm,),
        in_specs=in_specs,
        out_specs=pl.BlockSpec((tm, D_MODEL), lambda i: (i, 0)),
        out_shape=jax.ShapeDtypeStruct((t, D_MODEL), F32),
        compiler_params=_params(1),
        name=name,
    )(*args)


ATTN_TILES = 2
ATTN_ROWS = 1024


def _attn_prompt_kernel(q_ref, k_ref, v_ref, bias_ref, o_ref, lse_ref, kcat_ref, vcat_ref):
    step = pl.program_id(2)
    nq = q_ref.shape[0]

    @pl.when(step == 0)
    def _():
        kcat_ref[:SPAN] = jnp.zeros((SPAN, D_MODEL), BF16)
        vcat_ref[:SPAN] = jnp.zeros((SPAN, D_MODEL), BF16)

    @pl.when(step > 0)
    def _():
        kcat_ref[:SPAN] = kcat_ref[nq:]
        vcat_ref[:SPAN] = vcat_ref[nq:]

    kcat_ref[SPAN:] = k_ref[...]
    vcat_ref[SPAN:] = v_ref[...]

    lane = lax.broadcasted_iota(jnp.int32, (SPAN, HEAD_DIM), 1)
    dims = (((1,), (1,)), ((), ()))
    work = [(j, h) for j in range(ATTN_TILES) for h in range(HEADS)]
    cols = lambda h: slice(h * HEAD_DIM, (h + 1) * HEAD_DIM)

    def tiles(it, carry):
        rows = lambda j: pl.ds(pl.multiple_of((it * ATTN_TILES + j) * SPAN, SPAN), SPAN)
        keys = lambda j: pl.ds(pl.multiple_of((it * ATTN_TILES + j) * SPAN, SPAN), 2 * SPAN)
        starts_seq = jnp.where((step == 0) & (it == 0), 1, 0)
        s, m, p, l, o = {}, {}, {}, {}, {}
        for j, h in work:
            bias = bias_ref[starts_seq if j == 0 else 0, h]
            qk = lax.dot_general(q_ref[rows(j), cols(h)], kcat_ref[keys(j), cols(h)], dims,
                                 preferred_element_type=F32)
            s[j, h] = qk * (SCALE * LOG2E) + bias
        for j, h in work:
            m[j, h] = jnp.max(jnp.maximum(s[j, h][:, :SPAN], s[j, h][:, SPAN:]), axis=-1, keepdims=True)
        for j, h in work:
            e = jnp.exp2(s[j, h] - m[j, h])
            l[j, h] = jnp.sum(e[:, :SPAN] + e[:, SPAN:], axis=-1, keepdims=True)
            p[j, h] = e.astype(BF16)
        for j, h in work:
            o[j, h] = jnp.dot(p[j, h], vcat_ref[keys(j), cols(h)], preferred_element_type=F32)
        for j in range(ATTN_TILES):
            lse_all = jnp.zeros((SPAN, HEAD_DIM), F32)
            for h in range(HEADS):
                o_ref[rows(j), cols(h)] = o[j, h] / l[j, h]
                lse_all = jnp.where(lane == h, (m[j, h] + jnp.log2(l[j, h])) * LN2, lse_all)
            lse_ref[rows(j), :] = lse_all
        return carry

    lax.fori_loop(0, nq // (ATTN_TILES * SPAN), tiles, 0)


def _attn_prompt(qkv_g, bias_p, g):
    batch, dil, sub, _ = qkv_g.shape
    nq = min(sub, ATTN_ROWS)
    blk = (None, None, nq, D_MODEL)
    return pl.pallas_call(
        _attn_prompt_kernel,
        grid=(batch, dil, sub // nq),
        in_specs=[
            pl.BlockSpec(blk, lambda b, r, t: (b, r, t, 0)),
            pl.BlockSpec(blk, lambda b, r, t: (b, r, t, 1)),
            pl.BlockSpec(blk, lambda b, r, t: (b, r, t, 2)),
            pl.BlockSpec((None, 2, HEADS, SPAN, 2 * SPAN), lambda b, r, t: (g, 0, 0, 0, 0)),
        ],
        out_specs=[
            pl.BlockSpec(blk, lambda b, r, t: (b, r, t, 0)),
            pl.BlockSpec((None, None, nq, HEAD_DIM), lambda b, r, t: (b, r, t, 0)),
        ],
        out_shape=[
            jax.ShapeDtypeStruct((batch, dil, sub, D_MODEL), F32),
            jax.ShapeDtypeStruct((batch, dil, sub, HEAD_DIM), F32),
        ],
        scratch_shapes=[pltpu.VMEM((SPAN + nq, D_MODEL), BF16), pltpu.VMEM((SPAN + nq, D_MODEL), BF16)],
        compiler_params=_params(3),
        name=f"attn_prompt_g{g}",
    )(qkv_g, qkv_g, qkv_g, bias_p)


def _attn_sample_kernel(qkv_ref, k0_ref, v0_ref, k1_ref, v1_ref, k2_ref, v2_ref, bias_ref, a_ref, *, bb):
    kv_refs = ((k0_ref, v0_ref), (k1_ref, v1_ref), (k2_ref, v2_ref))
    for bi in range(bb):
        outs, lses = [], []
        for g, (k_ref, v_ref) in enumerate(kv_refs):
            q, k_new, v_new = qkv_ref[bi, 3 * g], qkv_ref[bi, 3 * g + 1], qkv_ref[bi, 3 * g + 2]
            s = jnp.sum(k_ref[bi] * q[None], axis=-1, keepdims=True) * SCALE + bias_ref[g, :SPAN]
            s_new = jnp.sum(k_new * q, axis=-1, keepdims=True) * SCALE + bias_ref[g, SPAN]
            m = jnp.maximum(jnp.max(s, axis=0), s_new)
            e = jnp.exp(s - m[None])
            e_new = jnp.exp(s_new - m)
            l = jnp.sum(e, axis=0) + e_new
            outs.append((jnp.sum(e * v_ref[bi], axis=0) + e_new * v_new) / l)
            lses.append(m + jnp.log(l))
        top = jnp.maximum(jnp.maximum(lses[0], lses[1]), lses[2])
        ws = [jnp.exp(lse - top) for lse in lses]
        den = ws[0] + ws[1] + ws[2]
        a_ref[bi] = (outs[0] * (ws[0] / den) + outs[1] * (ws[1] / den)) + outs[2] * (ws[2] / den)


def _attn_sample(qkv, caches, layer, bias_s):
    b = qkv.shape[0]
    bb = 4
    in_specs = [pl.BlockSpec((bb,) + qkv.shape[1:], lambda i: (i, 0, 0, 0))]
    args = [qkv]
    for g, (_, dil) in enumerate(DIL_GROUPS):
        cache = caches[g]
        assert cache.shape[3] == SPAN * dil
        view = cache.reshape(cache.shape[:3] + (SPAN, dil, HEADS, HEAD_DIM))
        for kv in range(2):
            in_specs.append(pl.BlockSpec((None, bb, None, SPAN, None, HEADS, HEAD_DIM),
                                         lambda i, kv=kv: (layer, i, kv, 0, 0, 0, 0)))
            args.append(view)
    in_specs.append(pl.BlockSpec(bias_s.shape, lambda i: (0, 0, 0, 0)))
    args.append(bias_s)
    return pl.pallas_call(
        functools.partial(_attn_sample_kernel, bb=bb),
        grid=(b // bb,),
        in_specs=in_specs,
        out_specs=pl.BlockSpec((bb, HEADS, HEAD_DIM), lambda i: (i, 0, 0)),
        out_shape=jax.ShapeDtypeStruct((b, HEADS, HEAD_DIM), F32),
        compiler_params=_params(1),
        name="attn_sample",
    )(*args)


def _combine_kernel(o0_ref, o1_ref, o2_ref, l0_ref, l1_ref, l2_ref, a_ref, os_ref, ls_ref):
    tm = a_ref.shape[0]
    for gi, o_ref, l_ref in ((1, o1_ref, l1_ref), (2, o2_ref, l2_ref)):
        dil = DIL_GROUPS[gi][1]
        rows = tm // dil
        for r in range(dil):
            ls_ref[gi - 1, pl.ds(r, rows, stride=dil), :] = l_ref[r]
            for h in range(HEADS):
                os_ref[gi - 1, h, pl.ds(r, rows, stride=dil), :] = o_ref[r, :, h * HEAD_DIM:(h + 1) * HEAD_DIM]
    l0, l1, l2 = l0_ref[0], ls_ref[0], ls_ref[1]
    m = jnp.maximum(jnp.maximum(l0, l1), l2)
    e0, e1, e2 = jnp.exp(l0 - m), jnp.exp(l1 - m), jnp.exp(l2 - m)
    den = e0 + e1 + e2
    w0, w1, w2 = e0 / den, e1 / den, e2 / den
    for h in range(HEADS):
        sl = slice(h * HEAD_DIM, (h + 1) * HEAD_DIM)
        o = (o0_ref[0, :, sl] * w0[:, h:h + 1] + os_ref[0, h] * w1[:, h:h + 1]) + os_ref[1, h] * w2[:, h:h + 1]
        a_ref[:, sl] = o.astype(a_ref.dtype)


def _combine(outs, lses, seq, tm):
    batch = outs[0].shape[0]
    tiles_per_seq = seq // tm
    in_specs = []
    for width in (D_MODEL, HEAD_DIM):
        for _, dil in DIL_GROUPS:
            in_specs.append(pl.BlockSpec((None, dil, tm // dil, width),
                                         lambda i: (i // tiles_per_seq, 0, i % tiles_per_seq, 0)))
    return pl.pallas_call(
        _combine_kernel,
        grid=(batch * tiles_per_seq,),
        in_specs=in_specs,
        out_specs=pl.BlockSpec((tm, D_MODEL), lambda i: (i, 0)),
        out_shape=jax.ShapeDtypeStruct((batch * seq, D_MODEL), BF16),
        scratch_shapes=[pltpu.VMEM((N_GROUPS - 1, HEADS, tm, HEAD_DIM), F32),
                        pltpu.VMEM((N_GROUPS - 1, tm, HEAD_DIM), F32)],
        compiler_params=_params(1),
        name="combine_groups",
    )(*outs, *lses)


def _ln_silu(z, g_ref, b_ref):
    mu = jnp.mean(z, axis=-1, keepdims=True)
    zc = z - mu
    var = jnp.mean(zc * zc, axis=-1, keepdims=True)
    y = (zc * lax.rsqrt(var + EPS)) * g_ref[...] + b_ref[...]
    return y * jax.nn.sigmoid(y)


SUBLANES = 8
HALO = 32
CONV_ROWS = 64
CONV_LANES = 256


def _conv_prompt_kernel(u_ref, prev_ref, w_ref, bdw_ref, g_ref, b_ref, a_ref, ext_ref, z_ref, *, tiles_per_seq):
    tm = u_ref.shape[0]
    starts_seq = pl.program_id(0) % tiles_per_seq == 0
    ext_ref[0, :HALO, :] = jnp.where(starts_seq, 0.0, prev_ref[...])
    ext_ref[0, HALO:, :] = u_ref[...]
    n_shifted = tm + HALO - SUBLANES
    for s in range(1, SUBLANES):
        for c0 in range(0, D_MODEL, CONV_LANES):
            cs = slice(c0, c0 + CONV_LANES)
            ext_ref[s, :n_shifted, cs] = ext_ref[0, s:s + n_shifted, cs]
    off = HALO - (CONV_WIDTH - 1)
    for r0 in range(0, tm, CONV_ROWS):
        for c0 in range(0, D_MODEL, CONV_LANES):
            cs = slice(c0, c0 + CONV_LANES)
            acc = jnp.zeros((CONV_ROWS, CONV_LANES), F32)
            for k in range(CONV_WIDTH):
                shift = (r0 + off + k) % SUBLANES
                base = r0 + off + k - shift
                acc = acc + ext_ref[shift, base:base + CONV_ROWS, cs] * w_ref[k:k + 1, cs]
            z_ref[r0:r0 + CONV_ROWS, cs] = acc + bdw_ref[:, cs]
    a_ref[...] = _ln_silu(z_ref[...], g_ref, b_ref).astype(a_ref.dtype)


def _conv_prompt(u, w_dw, b_dw, ln_g, ln_b, seq, tm):
    t = u.shape[0]
    row = lambda v: v.reshape(1, D_MODEL)
    vec = pl.BlockSpec((1, D_MODEL), lambda i: (0, 0))
    return pl.pallas_call(
        functools.partial(_conv_prompt_kernel, tiles_per_seq=seq // tm),
        grid=(t // tm,),
        in_specs=[
            pl.BlockSpec((tm, D_MODEL), lambda i: (i, 0)),
            pl.BlockSpec((HALO, D_MODEL), lambda i: (jnp.maximum(i * (tm // HALO) - 1, 0), 0)),
            pl.BlockSpec((CONV_WIDTH, D_MODEL), lambda i: (0, 0)),
            vec, vec, vec,
        ],
        out_specs=pl.BlockSpec((tm, D_MODEL), lambda i: (i, 0)),
        out_shape=jax.ShapeDtypeStruct((t, D_MODEL), BF16),
        scratch_shapes=[pltpu.VMEM((SUBLANES, tm + HALO, D_MODEL), F32), pltpu.VMEM((tm, D_MODEL), F32)],
        compiler_params=_params(1),
        name="conv_prompt",
    )(u, u, w_dw, row(b_dw), row(ln_g), row(ln_b))


def _conv_sample_kernel(u_ref, st_ref, w_ref, bdw_ref, g_ref, b_ref, a_ref):
    n_state = st_ref.shape[1]
    z = jnp.sum(st_ref[...] * w_ref[:n_state, :][None], axis=1)
    z = z + u_ref[...] * w_ref[n_state:n_state + 1, :] + bdw_ref[...]
    a_ref[...] = _ln_silu(z, g_ref, b_ref).astype(a_ref.dtype)


def _conv_sample(u, state, w_dw, b_dw, ln_g, ln_b):
    b, n_state, _ = state.shape
    bb = 16
    row = lambda v: v.reshape(1, D_MODEL)
    vec = pl.BlockSpec((1, D_MODEL), lambda i: (0, 0))
    return pl.pallas_call(
        _conv_sample_kernel,
        grid=(b // bb,),
        in_specs=[
            pl.BlockSpec((bb, D_MODEL), lambda i: (i, 0)),
            pl.BlockSpec((bb, n_state, D_MODEL), lambda i: (i, 0, 0)),
            pl.BlockSpec((CONV_WIDTH, D_MODEL), lambda i: (0, 0)),
            vec, vec, vec,
        ],
        out_specs=pl.BlockSpec((bb, D_MODEL), lambda i: (i, 0)),
        out_shape=jax.ShapeDtypeStruct((b, D_MODEL), BF16),
        compiler_params=_params(1),
        name="conv_sample",
    )(u, state, w_dw, row(b_dw), row(ln_g), row(ln_b))


def _ffn(x, mod, g_ffn, w_gate, w_up, w_down, *, tm_up, tm_down, tag, g_final=None):
    u = _call_norm_matmul("swiglu", x, g_ffn, mod, (3, 4), [(w_gate, 0), (w_up, 0)], [],
                          tm=tm_up, tn=FFN_HIDDEN, n_cols=FFN_HIDDEN, out_dtype=BF16,
                          name=f"ffn_up_{tag}")
    return _matmul_residual(u, w_down, None, x, mod, 5, tm=tm_down, name=f"ffn_down_{tag}", g_final=g_final)


def kernel(x_prompt, x_sample, cache_kv_w128, cache_kv_w512, cache_kv_w2048, state_conv, c_prompt, c_sample,
           w_mod, b_mod, g_mix, g_ffn, g_final, w_qkv, w_o, rel_bias, w_pw1, b_pw1, w_dw, b_dw, ln_g, ln_b,
           w_pw2, b_pw2, w_gate, w_up, w_down):
    batch, seq, d = x_prompt.shape
    dec_batch = x_sample.shape[0]
    assert d == D_MODEL and x_sample.shape[1] == 1
    caches = (cache_kv_w128, cache_kv_w512, cache_kv_w2048)

    bf = lambda w: w.astype(BF16)
    w_qkv, w_o, w_pw1, w_pw2, w_gate, w_up, w_down = map(bf, (w_qkv, w_o, w_pw1, w_pw2, w_gate, w_up, w_down))

    c_all = jnp.concatenate([c_sample, c_prompt, jnp.zeros((8 - batch, d), F32)], axis=0)
    mod_all = _modulation(c_all, w_mod, b_mod)
    bias_p, bias_s = _bias_tables(rel_bias)

    xp = x_prompt.reshape(batch * seq, d)
    xs = x_sample.reshape(dec_batch, d)
    n_qkv = 3 * N_GROUPS * d

    mod_p = _Mod(mod_all, 0, False, seq)
    mod_s = _Mod(mod_all, 0, True, 1)

    qkv_p = _qkv_prompt(xp, g_mix[0], mod_p, w_qkv[0], batch, seq, 1024)
    qkv_s = _call_norm_matmul("mm", xs, g_mix[0], mod_s, (0, 1), [(w_qkv[0], 0)], [],
                              tm=dec_batch, tn=1536, n_cols=n_qkv, out_dtype=F32, name="qkv_sample")

    kv_prompt = [_kv_tail(qkv_p[N_GROUPS], w_qkv[0], g, batch, seq, min(window, seq))
                 .reshape(1, batch, 2, min(window, seq), HEADS, HEAD_DIM) for g, (window, _) in enumerate(DIL_GROUPS)]

    outs, lses = zip(*[_attn_prompt(qkv_p[g], bias_p, g) for g in range(N_GROUPS)])
    a_p = _combine(outs, lses, seq, 512)
    xp = _matmul_residual(a_p, w_o[0], None, xp, mod_p, 2, tm=1024, name="attn_out_prompt")
    xp = _ffn(xp, mod_p, g_ffn[0], w_gate[0], w_up[0], w_down[0], tm_up=512, tm_down=512, tag="prompt0")

    qkv_s4 = qkv_s.reshape(dec_batch, 3 * N_GROUPS, HEADS, HEAD_DIM)
    bias_s4 = jnp.stack([bias_s[g, :, g * HEADS:(g + 1) * HEADS] for g in range(N_GROUPS)])[..., None]
    a_s = _attn_sample(qkv_s4, caches, 0, bias_s4).reshape(dec_batch, d).astype(BF16)
    xs = _matmul_residual(a_s, w_o[0], None, xs, mod_s, 2, tm=dec_batch, name="attn_out_sample")
    xs = _ffn(xs, mod_s, g_ffn[0], w_gate[0], w_up[0], w_down[0], tm_up=dec_batch, tm_down=dec_batch,
              tag="sample0")

    mod_p = _Mod(mod_all, 1, False, seq)
    mod_s = _Mod(mod_all, 1, True, 1)
    b_pw1_row = b_pw1[0].reshape(1, 2 * d)

    def glu(x, mod, tm, tag):
        return _call_norm_matmul("glu", x, g_mix[1], mod, (0, 1), [(w_pw1[0], 0), (w_pw1[0], 1)],
                                 [(b_pw1_row, 0), (b_pw1_row, 1)],
                                 tm=tm, tn=d, n_cols=d, out_dtype=F32, name=f"glu_{tag}")

    u_p = glu(xp, mod_p, 1024, "prompt")
    a_p = _conv_prompt(u_p, w_dw[0], b_dw[0], ln_g[0], ln_b[0], seq, 256)
    xp = _matmul_residual(a_p, w_pw2[0], b_pw2[0], xp, mod_p, 2, tm=1024, name="conv_out_prompt")
    y_p = _ffn(xp, mod_p, g_ffn[1], w_gate[1], w_up[1], w_down[1], tm_up=512, tm_down=512, tag="prompt1",
               g_final=g_final)

    u_s = glu(xs, mod_s, dec_batch, "sample")
    a_s = _conv_sample(u_s, state_conv[0], w_dw[0], b_dw[0], ln_g[0], ln_b[0])
    xs = _matmul_residual(a_s, w_pw2[0], b_pw2[0], xs, mod_s, 2, tm=dec_batch, name="conv_out_sample")
    y_s = _ffn(xs, mod_s, g_ffn[1], w_gate[1], w_up[1], w_down[1], tm_up=dec_batch, tm_down=dec_batch,
               tag="sample1", g_final=g_final)

    conv_p = u_p.reshape(batch, seq, d)[:, seq - (CONV_WIDTH - 1):][None]
    conv_s = jnp.concatenate([state_conv[0][:, 1:], u_s[:, None, :]], axis=1)[None]
    kv_sample = [qkv_s4[:, 3 * g + 1:3 * g + 3].reshape(1, dec_batch, 2, 1, HEADS, HEAD_DIM)
                 for g in range(N_GROUPS)]
    return (y_p.reshape(batch, seq, d), y_s.reshape(dec_batch, 1, d),
            kv_prompt[0], kv_prompt[1], kv_prompt[2], conv_p,
            kv_sample[0], kv_sample[1], kv_sample[2], conv_s)
```

```python
import functools
import math

import numpy as np
import jax
import jax.numpy as jnp
from jax import lax
from jax.experimental import pallas as pl
from jax.experimental.pallas import tpu as pltpu

D_MODEL = 1024
DIL_GROUPS = ((128, 1), (512, 4), (2048, 16))
N_GROUPS = len(DIL_GROUPS)
HEADS = 8
HEAD_DIM = 128
SPAN = 128
N_BUCKETS = 32
MAX_DISTANCE = 2048
CONV_WIDTH = 31
FFN_HIDDEN = 2816
EPS = 1e-6
NEG_INF = -1e30
SCALE = HEAD_DIM ** -0.5
LOG2E = math.log2(math.e)
LN2 = math.log(2.0)

F32 = jnp.float32
BF16 = jnp.bfloat16

VMEM_LIMIT_BYTES = 56 * 1024 * 1024


def _params(n_axes, vmem=VMEM_LIMIT_BYTES):
    return pltpu.CompilerParams(dimension_semantics=("arbitrary",) * n_axes, vmem_limit_bytes=vmem)


def _t5_bucket_np(dist):
    max_exact = N_BUCKETS // 2
    n = np.maximum(dist, 1).astype(np.float32)
    large = max_exact + (np.log(n / np.float32(max_exact)) / np.float32(math.log(MAX_DISTANCE / max_exact))
                         * np.float32(N_BUCKETS - max_exact)).astype(np.int32)
    large = np.minimum(large, N_BUCKETS - 1)
    return np.where(dist < max_exact, dist, large).astype(np.int32)


def _bucket_tables():
    qi = np.arange(SPAN)[:, None]
    ki = np.arange(2 * SPAN)[None, :]
    delta = qi + SPAN - ki
    in_band = (delta >= 0) & (delta <= SPAN)
    prompt, sample = [], []
    for _, dilation in DIL_GROUPS:
        b = _t5_bucket_np(np.clip(delta, 0, SPAN) * dilation)
        prompt.append(np.where(in_band, b, -1))
        j = SPAN - np.arange(136)
        sample.append(_t5_bucket_np(np.maximum(j, 0) * dilation)[:, None])
    return np.stack(prompt).astype(np.int32), np.stack(sample).astype(np.int32)


def _bias_kernel(rb_smem, rb_ref, bp_ref, bs_ref, op_ref, os_ref):
    g = pl.program_id(0)
    bp = bp_ref[...]
    in_prev_block = lax.broadcasted_iota(jnp.int32, bp.shape, 1) < SPAN
    for h in range(HEADS):
        acc = jnp.full(bp.shape, NEG_INF, F32)
        for b in range(N_BUCKETS):
            acc = jnp.where(bp == b, rb_smem[b, g * HEADS + h] * LOG2E, acc)
        op_ref[0, h] = acc
        op_ref[1, h] = jnp.where(in_prev_block, NEG_INF, acc)
    bs = bs_ref[...]
    acc = jnp.zeros((bs.shape[0], rb_ref.shape[1]), F32)
    for b in range(N_BUCKETS):
        acc = jnp.where(bs == b, rb_ref[b:b + 1, :], acc)
    os_ref[...] = acc


def _bias_tables(rel_bias):
    bp, bs = _bucket_tables()
    n_cols = rel_bias.shape[1]
    return pl.pallas_call(
        _bias_kernel,
        grid=(N_GROUPS,),
        in_specs=[
            pl.BlockSpec(memory_space=pltpu.SMEM),
            pl.BlockSpec((N_BUCKETS, n_cols), lambda g: (0, 0)),
            pl.BlockSpec((None, SPAN, 2 * SPAN), lambda g: (g, 0, 0)),
            pl.BlockSpec((None, 136, 1), lambda g: (g, 0, 0)),
        ],
        out_specs=[
            pl.BlockSpec((None, 2, HEADS, SPAN, 2 * SPAN), lambda g: (g, 0, 0, 0, 0)),
            pl.BlockSpec((None, 136, n_cols), lambda g: (g, 0, 0)),
        ],
        out_shape=[
            jax.ShapeDtypeStruct((N_GROUPS, 2, HEADS, SPAN, 2 * SPAN), F32),
            jax.ShapeDtypeStruct((N_GROUPS, 136, n_cols), F32),
        ],
        compiler_params=_params(1),
        name="bias_tables",
    )(rel_bias, rel_bias, jnp.asarray(bp), jnp.asarray(bs))


def _mod_kernel(c_ref, w_ref, b_ref, o_ref):
    c = c_ref[...]
    a = (c * jax.nn.sigmoid(c)).astype(BF16)
    o_ref[...] = jnp.dot(a, w_ref[...].astype(BF16), preferred_element_type=F32) + b_ref[...]


def _modulation(c_all, w_mod, b_mod):
    depth, d, n = w_mod.shape
    rows = c_all.shape[0]
    tn = 1536
    return pl.pallas_call(
        _mod_kernel,
        grid=(depth, n // tn),
        in_specs=[
            pl.BlockSpec((rows, d), lambda l, j: (0, 0)),
            pl.BlockSpec((None, d, tn), lambda l, j: (l, 0, j)),
            pl.BlockSpec((None, 1, tn), lambda l, j: (l, 0, j)),
        ],
        out_specs=pl.BlockSpec((None, rows, tn), lambda l, j: (l, 0, j)),
        out_shape=jax.ShapeDtypeStruct((depth, rows, n), F32),
        compiler_params=_params(2),
        name="modulation",
    )(c_all, w_mod, b_mod.reshape(depth, 1, n))


def _normed(x_ref, g_ref, sc_ref, sh_ref):
    x = x_ref[...]
    y = x * lax.rsqrt(jnp.mean(x * x, axis=-1, keepdims=True) + EPS)
    return ((y * g_ref[...]) * (1.0 + sc_ref[...]) + sh_ref[...]).astype(BF16)


def _norm_mm_kernel(x_ref, g_ref, sc_ref, sh_ref, w_ref, o_ref, h_ref):
    @pl.when(pl.program_id(1) == 0)
    def _():
        h_ref[...] = _normed(x_ref, g_ref, sc_ref, sh_ref)
    o_ref[...] = jnp.dot(h_ref[...], w_ref[...], preferred_element_type=F32).astype(o_ref.dtype)


def _norm_glu_kernel(x_ref, g_ref, sc_ref, sh_ref, wa_ref, wg_ref, ba_ref, bg_ref, o_ref, h_ref):
    @pl.when(pl.program_id(1) == 0)
    def _():
        h_ref[...] = _normed(x_ref, g_ref, sc_ref, sh_ref)
    h = h_ref[...]
    a = jnp.dot(h, wa_ref[...], preferred_element_type=F32) + ba_ref[...]
    gate = jnp.dot(h, wg_ref[...], preferred_element_type=F32) + bg_ref[...]
    o_ref[...] = a * jax.nn.sigmoid(gate)


def _norm_swiglu_kernel(x_ref, g_ref, sc_ref, sh_ref, wg_ref, wu_ref, o_ref, h_ref):
    @pl.when(pl.program_id(1) == 0)
    def _():
        h_ref[...] = _normed(x_ref, g_ref, sc_ref, sh_ref)
    h = h_ref[...]
    gate = jnp.dot(h, wg_ref[...], preferred_element_type=F32)
    up = jnp.dot(h, wu_ref[...], preferred_element_type=F32)
    o_ref[...] = ((gate * jax.nn.sigmoid(gate)) * up).astype(o_ref.dtype)


class _LayerWeight:
    def __init__(self, stacked, layer):
        self.arr = stacked
        self.layer = layer

    def spec(self, block, index, **kwargs):
        layer = self.layer
        return pl.BlockSpec((None,) + tuple(block), lambda *i: (layer,) + tuple(index(*i)), **kwargs)


class _Mod:
    def __init__(self, mod_all, layer, per_token, rows_per_seq):
        self.layer = layer
        self.per_token = per_token
        self.rows_per_seq = rows_per_seq
        depth, rows, n = mod_all.shape
        self.n_sample = rows - 8
        self.arr = mod_all if per_token else mod_all.reshape(depth, rows, 1, n)

    def spec(self, chunk, tm, row_of):
        layer = self.layer
        if self.per_token:
            return pl.BlockSpec((None, tm, D_MODEL), lambda *i: (layer, row_of(*i) // tm, chunk))
        first, per = self.n_sample, self.rows_per_seq
        return pl.BlockSpec((None, None, 1, D_MODEL), lambda *i: (layer, first + row_of(*i) // per, 0, chunk))


def _call_norm_matmul(kind, x, gain, mod, chunks, weights, biases, *, tm, tn, n_cols, out_dtype, name):
    t = x.shape[0]
    row_of = lambda i, j: i * tm
    kernel = {"mm": _norm_mm_kernel, "glu": _norm_glu_kernel, "swiglu": _norm_swiglu_kernel}[kind]
    chunk_shift, chunk_scale = chunks
    in_specs = [
        pl.BlockSpec((tm, D_MODEL), lambda i, j: (i, 0)),
        pl.BlockSpec((1, D_MODEL), lambda i, j: (0, 0)),
        mod.spec(chunk_scale, tm, row_of),
        mod.spec(chunk_shift, tm, row_of),
    ]
    args = [x, gain.reshape(1, D_MODEL), mod.arr, mod.arr]
    resident = dict(pipeline_mode=pl.Buffered(1)) if n_cols == tn else {}
    for w, col0 in weights:
        in_specs.append(w.spec((D_MODEL, tn), lambda i, j, col0=col0: (0, col0 + j), **resident))
        args.append(w.arr)
    for b, col0 in biases:
        in_specs.append(pl.BlockSpec((1, tn), lambda i, j, col0=col0: (0, col0 + j)))
        args.append(b)
    return pl.pallas_call(
        kernel,
        grid=(t // tm, n_cols // tn),
        in_specs=in_specs,
        out_specs=pl.BlockSpec((tm, tn), lambda i, j: (i, j)),
        out_shape=jax.ShapeDtypeStruct((t, n_cols), out_dtype),
        scratch_shapes=[pltpu.VMEM((tm, D_MODEL), BF16)],
        compiler_params=_params(2),
        name=name,
    )(*args)


def _mm_kernel(a_ref, w_ref, o_ref):
    o_ref[...] = jnp.dot(a_ref[...], w_ref[...], preferred_element_type=F32)


def _kv_tail(h, w_qkv, g, batch, seq, keep):
    tm = min(keep, 1024)
    per_seq = keep // tm
    first = (seq - keep) // tm
    return pl.pallas_call(
        _mm_kernel,
        grid=(batch * per_seq, 2),
        in_specs=[
            pl.BlockSpec((tm, D_MODEL), lambda i, j: ((i // per_seq) * (seq // tm) + first + i % per_seq, 0)),
            w_qkv.spec((D_MODEL, D_MODEL), lambda i, j: (0, 3 * g + 1 + j)),
        ],
        out_specs=pl.BlockSpec((None, None, tm, D_MODEL), lambda i, j: (i // per_seq, j, i % per_seq, 0)),
        out_shape=jax.ShapeDtypeStruct((batch, 2, keep, D_MODEL), F32),
        compiler_params=_params(2),
        name=f"kv_tail_g{g}",
    )(h, w_qkv.arr)


LANES = 128
N_LANE_SLABS = D_MODEL // LANES


def _qkv_prompt_kernel(x_ref, g_ref, sc_ref, sh_ref, w_ref, o0_ref, o1_ref, o2_ref, hn_ref, h_ref, slab_ref):
    tm = x_ref.shape[0]
    n = pl.program_id(1)
    o_refs = (o0_ref, o1_ref, o2_ref)

    @pl.when(n == 0)
    def _():
        x = x_ref[...]
        y = x * lax.rsqrt(jnp.mean(x * x, axis=-1, keepdims=True) + EPS)
        h = (y * g_ref[...]) * (1.0 + sc_ref[...]) + sh_ref[...]
        hn_ref[...] = h.astype(BF16)
        h_ref[0] = h.astype(BF16)
        quarter, sixteenth = tm // 4, tm // 16
        for c in range(N_LANE_SLABS):
            lanes = slice(c * LANES, (c + 1) * LANES)
            slab_ref[0, c] = h[:, lanes]
            for b in range(4):
                part = slab_ref[0, c, pl.ds(b, quarter, stride=4), :]
                slab_ref[1, c, b * quarter:(b + 1) * quarter, :] = part
                h_ref[1, b * quarter:(b + 1) * quarter, lanes] = part.astype(BF16)
            for b in range(4):
                for a in range(4):
                    r = 4 * a + b
                    part = slab_ref[1, c, pl.ds(b * quarter + a, sixteenth, stride=4), :]
                    h_ref[2, r * sixteenth:(r + 1) * sixteenth, lanes] = part.astype(BF16)

    for gi, (_, dil) in enumerate(DIL_GROUPS):
        @pl.when(n // 3 == gi)
        def _(gi=gi, dil=dil):
            acc = jnp.dot(h_ref[gi], w_ref[...], preferred_element_type=F32).astype(BF16)
            rows = tm // dil
            for r in range(dil):
                o_refs[gi][r] = acc[r * rows:(r + 1) * rows, :]


def _qkv_prompt(x, gain, mod, w, batch, seq, tm):
    assert [dil for _, dil in DIL_GROUPS] == [1, 4, 16]
    tiles_per_seq = seq // tm
    row_of = lambda i, n: i * tm
    out_specs, out_shapes = [], []
    for g, (_, dil) in enumerate(DIL_GROUPS):
        out_specs.append(pl.BlockSpec(
            (None, dil, tm // dil, D_MODEL),
            lambda i, n, g=g: (i // tiles_per_seq, 0, i % tiles_per_seq, jnp.clip(n - 3 * g, 0, 2))))
        out_shapes.append(jax.ShapeDtypeStruct((batch, dil, seq // dil, 3 * D_MODEL), BF16))
    out_specs.append(pl.BlockSpec((tm, D_MODEL), lambda i, n: (i, 0)))
    out_shapes.append(jax.ShapeDtypeStruct((batch * seq, D_MODEL), BF16))
    return pl.pallas_call(
        _qkv_prompt_kernel,
        grid=(batch * tiles_per_seq, 3 * N_GROUPS),
        in_specs=[
            pl.BlockSpec((tm, D_MODEL), lambda i, n: (i, 0)),
            pl.BlockSpec((1, D_MODEL), lambda i, n: (0, 0)),
            mod.spec(1, tm, row_of),
            mod.spec(0, tm, row_of),
            w.spec((D_MODEL, D_MODEL), lambda i, n: (0, n)),
        ],
        out_specs=out_specs,
        out_shape=out_shapes,
        scratch_shapes=[pltpu.VMEM((N_GROUPS, tm, D_MODEL), BF16),
                        pltpu.VMEM((2, N_LANE_SLABS, tm, LANES), F32)],
        compiler_params=_params(2),
        name="qkv_prompt",
    )(x, gain.reshape(1, D_MODEL), mod.arr, mod.arr, w.arr)


def _mm_res_kernel(*refs, has_bias, final_norm):
    a_ref, w_ref = refs[0], refs[1]
    k = 2
    b_ref = None
    if has_bias:
        b_ref = refs[k]
        k += 1
    x_ref, gate_ref = refs[k], refs[k + 1]
    k += 2
    gf_ref = None
    if final_norm:
        gf_ref = refs[k]
        k += 1
    o_ref = refs[k]
    out = jnp.dot(a_ref[...], w_ref[...], preferred_element_type=F32)
    if has_bias:
        out = out + b_ref[...]
    x = x_ref[...] + gate_ref[...] * out
    if final_norm:
        x = (x * lax.rsqrt(jnp.mean(x * x, axis=-1, keepdims=True) + EPS)) * gf_ref[...]
    o_ref[...] = x


def _matmul_residual(a, w, bias, x, mod, chunk_gate, *, tm, name, g_final=None):
    t, k = a.shape
    row_of = lambda i: i * tm
    in_specs = [
        pl.BlockSpec((tm, k), lambda i: (i, 0)),
        w.spec((k, D_MODEL), lambda i: (0, 0), pipeline_mode=pl.Buffered(1)),
    ]
    args = [a, w.arr]
    if bias is not None:
        in_specs.append(pl.BlockSpec((1, D_MODEL), lambda i: (0, 0)))
        args.append(bias.reshape(1, D_MODEL))
    in_specs += [pl.BlockSpec((tm, D_MODEL), lambda i: (i, 0)), mod.spec(chunk_gate, tm, row_of)]
    args += [x, mod.arr]
    if g_final is not None:
        in_specs.append(pl.BlockSpec((1, D_MODEL), lambda i: (0, 0)))
        args.append(g_final.reshape(1, D_MODEL))
    return pl.pallas_call(
        functools.partial(_mm_res_kernel, has_bias=bias is not None, final_norm=g_final is not None),
        grid=(t // tm,),
        in_specs=in_specs,
        out_specs=pl.BlockSpec((tm, D_MODEL), lambda i: (i, 0)),
        out_shape=jax.ShapeDtypeStruct((t, D_MODEL), F32),
        compiler_params=_params(1),
        name=name,
    )(*args)


ATTN_TILES = 2
ATTN_ROWS = 1024


def _attn_prompt_kernel(q_ref, k_ref, v_ref, bias_ref, o_ref, lse_ref, kcat_ref, vcat_ref):
    step = pl.program_id(2)
    n_res, nq, _ = q_ref.shape
    lane = lax.broadcasted_iota(jnp.int32, (SPAN, HEAD_DIM), 1)
    dims = (((1,), (1,)), ((), ()))
    work = [(j, h) for j in range(ATTN_TILES) for h in range(HEADS)]
    cols = lambda h: slice(h * HEAD_DIM, (h + 1) * HEAD_DIM)

    for res in range(n_res):
        @pl.when(step == 0)
        def _():
            kcat_ref[:SPAN] = jnp.zeros((SPAN, D_MODEL), BF16)
            vcat_ref[:SPAN] = jnp.zeros((SPAN, D_MODEL), BF16)

        @pl.when(step > 0)
        def _():
            kcat_ref[:SPAN] = kcat_ref[nq:]
            vcat_ref[:SPAN] = vcat_ref[nq:]

        kcat_ref[SPAN:] = k_ref[res]
        vcat_ref[SPAN:] = v_ref[res]

        def tiles(it, carry, res=res):
            rows = lambda j: pl.ds(pl.multiple_of((it * ATTN_TILES + j) * SPAN, SPAN), SPAN)
            keys = lambda j: pl.ds(pl.multiple_of((it * ATTN_TILES + j) * SPAN, SPAN), 2 * SPAN)
            starts_seq = jnp.where((step == 0) & (it == 0), 1, 0)
            s, m, p, l, o = {}, {}, {}, {}, {}
            for j, h in work:
                bias = bias_ref[starts_seq if j == 0 else 0, h]
                qk = lax.dot_general(q_ref[res, rows(j), cols(h)], kcat_ref[keys(j), cols(h)], dims,
                                     preferred_element_type=F32)
                s[j, h] = qk * (SCALE * LOG2E) + bias
            for j, h in work:
                m[j, h] = jnp.max(jnp.maximum(s[j, h][:, :SPAN], s[j, h][:, SPAN:]), axis=-1, keepdims=True)
            for j, h in work:
                e = jnp.exp2(s[j, h] - m[j, h])
                l[j, h] = jnp.sum(e[:, :SPAN] + e[:, SPAN:], axis=-1, keepdims=True)
                p[j, h] = e.astype(BF16)
            for j, h in work:
                o[j, h] = jnp.dot(p[j, h], vcat_ref[keys(j), cols(h)], preferred_element_type=F32)
            for j in range(ATTN_TILES):
                lse_all = jnp.zeros((SPAN, HEAD_DIM), F32)
                for h in range(HEADS):
                    o_ref[res, rows(j), cols(h)] = o[j, h] / l[j, h]
                    lse_all = jnp.where(lane == h, (m[j, h] + jnp.log2(l[j, h])) * LN2, lse_all)
                lse_ref[res, rows(j), :] = lse_all
            return carry

        lax.fori_loop(0, nq // (ATTN_TILES * SPAN), tiles, 0)


def _attn_prompt(qkv_g, bias_p, g):
    batch, dil, sub, _ = qkv_g.shape
    nq = min(sub, ATTN_ROWS)
    n_res = min(dil, ATTN_ROWS // nq)
    assert n_res == 1 or nq == sub
    blk = (None, n_res, nq, D_MODEL)
    return pl.pallas_call(
        _attn_prompt_kernel,
        grid=(batch, dil // n_res, sub // nq),
        in_specs=[
            pl.BlockSpec(blk, lambda b, r, t: (b, r, t, 0)),
            pl.BlockSpec(blk, lambda b, r, t: (b, r, t, 1)),
            pl.BlockSpec(blk, lambda b, r, t: (b, r, t, 2)),
            pl.BlockSpec((None, 2, HEADS, SPAN, 2 * SPAN), lambda b, r, t: (g, 0, 0, 0, 0)),
        ],
        out_specs=[
            pl.BlockSpec(blk, lambda b, r, t: (b, r, t, 0)),
            pl.BlockSpec((None, n_res, nq, HEAD_DIM), lambda b, r, t: (b, r, t, 0)),
        ],
        out_shape=[
            jax.ShapeDtypeStruct((batch, dil, sub, D_MODEL), F32),
            jax.ShapeDtypeStruct((batch, dil, sub, HEAD_DIM), F32),
        ],
        scratch_shapes=[pltpu.VMEM((SPAN + nq, D_MODEL), BF16), pltpu.VMEM((SPAN + nq, D_MODEL), BF16)],
        compiler_params=_params(3),
        name=f"attn_prompt_g{g}",
    )(qkv_g, qkv_g, qkv_g, bias_p)


def _attn_sample_kernel(qkv_ref, k0_ref, v0_ref, k1_ref, v1_ref, k2_ref, v2_ref, bias_ref, a_ref, *, bb):
    kv_refs = ((k0_ref, v0_ref), (k1_ref, v1_ref), (k2_ref, v2_ref))
    for bi in range(bb):
        outs, lses = [], []
        for g, (k_ref, v_ref) in enumerate(kv_refs):
            q, k_new, v_new = qkv_ref[bi, 3 * g], qkv_ref[bi, 3 * g + 1], qkv_ref[bi, 3 * g + 2]
            s = jnp.sum(k_ref[bi] * q[None], axis=-1, keepdims=True) * SCALE + bias_ref[g, :SPAN]
            s_new = jnp.sum(k_new * q, axis=-1, keepdims=True) * SCALE + bias_ref[g, SPAN]
            m = jnp.maximum(jnp.max(s, axis=0), s_new)
            e = jnp.exp(s - m[None])
            e_new = jnp.exp(s_new - m)
            l = jnp.sum(e, axis=0) + e_new
            outs.append((jnp.sum(e * v_ref[bi], axis=0) + e_new * v_new) / l)
            lses.append(m + jnp.log(l))
        top = jnp.maximum(jnp.maximum(lses[0], lses[1]), lses[2])
        ws = [jnp.exp(lse - top) for lse in lses]
        den = ws[0] + ws[1] + ws[2]
        a_ref[bi] = (outs[0] * (ws[0] / den) + outs[1] * (ws[1] / den)) + outs[2] * (ws[2] / den)


def _attn_sample(qkv, caches, layer, bias_s):
    b = qkv.shape[0]
    bb = 4
    in_specs = [pl.BlockSpec((bb,) + qkv.shape[1:], lambda i: (i, 0, 0, 0))]
    args = [qkv]
    for g, (_, dil) in enumerate(DIL_GROUPS):
        cache = caches[g]
        assert cache.shape[3] == SPAN * dil
        view = cache.reshape(cache.shape[:3] + (SPAN, dil, HEADS, HEAD_DIM))
        for kv in range(2):
            in_specs.append(pl.BlockSpec((None, bb, None, SPAN, None, HEADS, HEAD_DIM),
                                         lambda i, kv=kv: (layer, i, kv, 0, 0, 0, 0)))
            args.append(view)
    in_specs.append(pl.BlockSpec(bias_s.shape, lambda i: (0, 0, 0, 0)))
    args.append(bias_s)
    return pl.pallas_call(
        functools.partial(_attn_sample_kernel, bb=bb),
        grid=(b // bb,),
        in_specs=in_specs,
        out_specs=pl.BlockSpec((bb, HEADS, HEAD_DIM), lambda i: (i, 0, 0)),
        out_shape=jax.ShapeDtypeStruct((b, HEADS, HEAD_DIM), F32),
        compiler_params=_params(1),
        name="attn_sample",
    )(*args)


def _combine_kernel(o0_ref, o1_ref, o2_ref, l0_ref, l1_ref, l2_ref, a_ref, os_ref, ls_ref):
    tm = a_ref.shape[0]
    for gi, o_ref, l_ref in ((1, o1_ref, l1_ref), (2, o2_ref, l2_ref)):
        dil = DIL_GROUPS[gi][1]
        rows = tm // dil
        for r in range(dil):
            ls_ref[gi - 1, pl.ds(r, rows, stride=dil), :] = l_ref[r]
            for h in range(HEADS):
                os_ref[gi - 1, h, pl.ds(r, rows, stride=dil), :] = o_ref[r, :, h * HEAD_DIM:(h + 1) * HEAD_DIM]
    l0, l1, l2 = l0_ref[0], ls_ref[0], ls_ref[1]
    m = jnp.maximum(jnp.maximum(l0, l1), l2)
    e0, e1, e2 = jnp.exp(l0 - m), jnp.exp(l1 - m), jnp.exp(l2 - m)
    den = e0 + e1 + e2
    w0, w1, w2 = e0 / den, e1 / den, e2 / den
    for h in range(HEADS):
        sl = slice(h * HEAD_DIM, (h + 1) * HEAD_DIM)
        o = (o0_ref[0, :, sl] * w0[:, h:h + 1] + os_ref[0, h] * w1[:, h:h + 1]) + os_ref[1, h] * w2[:, h:h + 1]
        a_ref[:, sl] = o.astype(a_ref.dtype)


def _combine(outs, lses, seq, tm):
    batch = outs[0].shape[0]
    tiles_per_seq = seq // tm
    in_specs = []
    for width in (D_MODEL, HEAD_DIM):
        for _, dil in DIL_GROUPS:
            in_specs.append(pl.BlockSpec((None, dil, tm // dil, width),
                                         lambda i: (i // tiles_per_seq, 0, i % tiles_per_seq, 0)))
    return pl.pallas_call(
        _combine_kernel,
        grid=(batch * tiles_per_seq,),
        in_specs=in_specs,
        out_specs=pl.BlockSpec((tm, D_MODEL), lambda i: (i, 0)),
        out_shape=jax.ShapeDtypeStruct((batch * seq, D_MODEL), BF16),
        scratch_shapes=[pltpu.VMEM((N_GROUPS - 1, HEADS, tm, HEAD_DIM), F32),
                        pltpu.VMEM((N_GROUPS - 1, tm, HEAD_DIM), F32)],
        compiler_params=_params(1),
        name="combine_groups",
    )(*outs, *lses)


def _ln_silu(z, g_ref, b_ref):
    mu = jnp.mean(z, axis=-1, keepdims=True)
    zc = z - mu
    var = jnp.mean(zc * zc, axis=-1, keepdims=True)
    y = (zc * lax.rsqrt(var + EPS)) * g_ref[...] + b_ref[...]
    return y * jax.nn.sigmoid(y)


SUBLANES = 8
HALO = 32
CONV_ROWS = 64
CONV_LANES = 256


def _conv_prompt_kernel(u_ref, prev_ref, w_ref, bdw_ref, g_ref, b_ref, a_ref, ext_ref, z_ref, *, tiles_per_seq):
    tm = u_ref.shape[0]
    starts_seq = pl.program_id(0) % tiles_per_seq == 0
    ext_ref[0, :HALO, :] = jnp.where(starts_seq, 0.0, prev_ref[...])
    ext_ref[0, HALO:, :] = u_ref[...]
    n_shifted = tm + HALO - SUBLANES
    for s in range(1, SUBLANES):
        for c0 in range(0, D_MODEL, CONV_LANES):
            cs = slice(c0, c0 + CONV_LANES)
            ext_ref[s, :n_shifted, cs] = ext_ref[0, s:s + n_shifted, cs]
    off = HALO - (CONV_WIDTH - 1)
    for r0 in range(0, tm, CONV_ROWS):
        for c0 in range(0, D_MODEL, CONV_LANES):
            cs = slice(c0, c0 + CONV_LANES)
            acc = jnp.zeros((CONV_ROWS, CONV_LANES), F32)
            for k in range(CONV_WIDTH):
                shift = (r0 + off + k) % SUBLANES
                base = r0 + off + k - shift
                acc = acc + ext_ref[shift, base:base + CONV_ROWS, cs] * w_ref[k:k + 1, cs]
            z_ref[r0:r0 + CONV_ROWS, cs] = acc + bdw_ref[:, cs]
    a_ref[...] = _ln_silu(z_ref[...], g_ref, b_ref).astype(a_ref.dtype)


def _conv_prompt(u, w_dw, b_dw, ln_g, ln_b, seq, tm):
    t = u.shape[0]
    row = lambda v: v.reshape(1, D_MODEL)
    vec = pl.BlockSpec((1, D_MODEL), lambda i: (0, 0))
    return pl.pallas_call(
        functools.partial(_conv_prompt_kernel, tiles_per_seq=seq // tm),
        grid=(t // tm,),
        in_specs=[
            pl.BlockSpec((tm, D_MODEL), lambda i: (i, 0)),
            pl.BlockSpec((HALO, D_MODEL), lambda i: (jnp.maximum(i * (tm // HALO) - 1, 0), 0)),
            pl.BlockSpec((CONV_WIDTH, D_MODEL), lambda i: (0, 0)),
            vec, vec, vec,
        ],
        out_specs=pl.BlockSpec((tm, D_MODEL), lambda i: (i, 0)),
        out_shape=jax.ShapeDtypeStruct((t, D_MODEL), BF16),
        scratch_shapes=[pltpu.VMEM((SUBLANES, tm + HALO, D_MODEL), F32), pltpu.VMEM((tm, D_MODEL), F32)],
        compiler_params=_params(1),
        name="conv_prompt",
    )(u, u, w_dw, row(b_dw), row(ln_g), row(ln_b))


def _conv_sample_kernel(u_ref, st_ref, w_ref, bdw_ref, g_ref, b_ref, a_ref):
    n_state = st_ref.shape[1]
    z = jnp.sum(st_ref[...] * w_ref[:n_state, :][None], axis=1)
    z = z + u_ref[...] * w_ref[n_state:n_state + 1, :] + bdw_ref[...]
    a_ref[...] = _ln_silu(z, g_ref, b_ref).astype(a_ref.dtype)


def _conv_sample(u, state, w_dw, b_dw, ln_g, ln_b):
    b, n_state, _ = state.shape
    bb = 16
    row = lambda v: v.reshape(1, D_MODEL)
    vec = pl.BlockSpec((1, D_MODEL), lambda i: (0, 0))
    return pl.pallas_call(
        _conv_sample_kernel,
        grid=(b // bb,),
        in_specs=[
            pl.BlockSpec((bb, D_MODEL), lambda i: (i, 0)),
            pl.BlockSpec((bb, n_state, D_MODEL), lambda i: (i, 0, 0)),
            pl.BlockSpec((CONV_WIDTH, D_MODEL), lambda i: (0, 0)),
            vec, vec, vec,
        ],
        out_specs=pl.BlockSpec((bb, D_MODEL), lambda i: (i, 0)),
        out_shape=jax.ShapeDtypeStruct((b, D_MODEL), BF16),
        compiler_params=_params(1),
        name="conv_sample",
    )(u, state, w_dw, row(b_dw), row(ln_g), row(ln_b))


def _ffn(x, mod, g_ffn, w_gate, w_up, w_down, *, tm_up, tm_down, tag, g_final=None):
    u = _call_norm_matmul("swiglu", x, g_ffn, mod, (3, 4), [(w_gate, 0), (w_up, 0)], [],
                          tm=tm_up, tn=FFN_HIDDEN, n_cols=FFN_HIDDEN, out_dtype=BF16,
                          name=f"ffn_up_{tag}")
    return _matmul_residual(u, w_down, None, x, mod, 5, tm=tm_down, name=f"ffn_down_{tag}", g_final=g_final)


def kernel(x_prompt, x_sample, cache_kv_w128, cache_kv_w512, cache_kv_w2048, state_conv, c_prompt, c_sample,
           w_mod, b_mod, g_mix, g_ffn, g_final, w_qkv, w_o, rel_bias, w_pw1, b_pw1, w_dw, b_dw, ln_g, ln_b,
           w_pw2, b_pw2, w_gate, w_up, w_down):
    batch, seq, d = x_prompt.shape
    dec_batch = x_sample.shape[0]
    assert d == D_MODEL and x_sample.shape[1] == 1
    caches = (cache_kv_w128, cache_kv_w512, cache_kv_w2048)

    def layers(w):
        w = w.astype(BF16)
        return [_LayerWeight(w, layer) for layer in range(w.shape[0])]

    w_qkv, w_o, w_pw1, w_pw2, w_gate, w_up, w_down = map(layers, (w_qkv, w_o, w_pw1, w_pw2, w_gate, w_up, w_down))

    c_all = jnp.concatenate([c_sample, c_prompt, jnp.zeros((8 - batch, d), F32)], axis=0)
    mod_all = _modulation(c_all, w_mod, b_mod)
    bias_p, bias_s = _bias_tables(rel_bias)

    xp = x_prompt.reshape(batch * seq, d)
    xs = x_sample.reshape(dec_batch, d)
    n_qkv = 3 * N_GROUPS * d

    mod_p = _Mod(mod_all, 0, False, seq)
    mod_s = _Mod(mod_all, 0, True, 1)

    qkv_p = _qkv_prompt(xp, g_mix[0], mod_p, w_qkv[0], batch, seq, 1024)
    qkv_s = _call_norm_matmul("mm", xs, g_mix[0], mod_s, (0, 1), [(w_qkv[0], 0)], [],
                              tm=dec_batch, tn=1536, n_cols=n_qkv, out_dtype=F32, name="qkv_sample")

    kv_prompt = [_kv_tail(qkv_p[N_GROUPS], w_qkv[0], g, batch, seq, min(window, seq))
                 .reshape(1, batch, 2, min(window, seq), HEADS, HEAD_DIM) for g, (window, _) in enumerate(DIL_GROUPS)]

    outs, lses = zip(*[_attn_prompt(qkv_p[g], bias_p, g) for g in range(N_GROUPS)])
    a_p = _combine(outs, lses, seq, 1024)
    xp = _matmul_residual(a_p, w_o[0], None, xp, mod_p, 2, tm=1024, name="attn_out_prompt")
    xp = _ffn(xp, mod_p, g_ffn[0], w_gate[0], w_up[0], w_down[0], tm_up=512, tm_down=1024, tag="prompt0")

    qkv_s4 = qkv_s.reshape(dec_batch, 3 * N_GROUPS, HEADS, HEAD_DIM)
    bias_s4 = jnp.stack([bias_s[g, :, g * HEADS:(g + 1) * HEADS] for g in range(N_GROUPS)])[..., None]
    a_s = _attn_sample(qkv_s4, caches, 0, bias_s4).reshape(dec_batch, d).astype(BF16)
    xs = _matmul_residual(a_s, w_o[0], None, xs, mod_s, 2, tm=dec_batch, name="attn_out_sample")
    xs = _ffn(xs, mod_s, g_ffn[0], w_gate[0], w_up[0], w_down[0], tm_up=dec_batch, tm_down=dec_batch,
              tag="sample0")

    mod_p = _Mod(mod_all, 1, False, seq)
    mod_s = _Mod(mod_all, 1, True, 1)
    b_pw1_row = b_pw1[0].reshape(1, 2 * d)

    def glu(x, mod, tm, tag):
        return _call_norm_matmul("glu", x, g_mix[1], mod, (0, 1), [(w_pw1[0], 0), (w_pw1[0], 1)],
                                 [(b_pw1_row, 0), (b_pw1_row, 1)],
                                 tm=tm, tn=d, n_cols=d, out_dtype=F32, name=f"glu_{tag}")

    u_p = glu(xp, mod_p, 1024, "prompt")
    a_p = _conv_prompt(u_p, w_dw[0], b_dw[0], ln_g[0], ln_b[0], seq, 512)
    xp = _matmul_residual(a_p, w_pw2[0], b_pw2[0], xp, mod_p, 2, tm=1024, name="conv_out_prompt")
    y_p = _ffn(xp, mod_p, g_ffn[1], w_gate[1], w_up[1], w_down[1], tm_up=512, tm_down=1024, tag="prompt1",
               g_final=g_final)

    u_s = glu(xs, mod_s, dec_batch, "sample")
    a_s = _conv_sample(u_s, state_conv[0], w_dw[0], b_dw[0], ln_g[0], ln_b[0])
    xs = _matmul_residual(a_s, w_pw2[0], b_pw2[0], xs, mod_s, 2, tm=dec_batch, name="conv_out_sample")
    y_s = _ffn(xs, mod_s, g_ffn[1], w_gate[1], w_up[1], w_down[1], tm_up=dec_batch, tm_down=dec_batch,
               tag="sample1", g_final=g_final)

    conv_p = u_p.reshape(batch, seq, d)[:, seq - (CONV_WIDTH - 1):][None]
    conv_s = jnp.concatenate([state_conv[0][:, 1:], u_s[:, None, :]], axis=1)[None]
    kv_sample = [qkv_s4[:, 3 * g + 1:3 * g + 3].reshape(1, dec_batch, 2, 1, HEADS, HEAD_DIM)
                 for g in range(N_GROUPS)]
    return (y_p.reshape(batch, seq, d), y_s.reshape(dec_batch, 1, d),
            kv_prompt[0], kv_prompt[1], kv_prompt[2], conv_p,
            kv_sample[0], kv_sample[1], kv_sample[2], conv_s)
```

```python
import functools
import math

import numpy as np
import jax
import jax.numpy as jnp
from jax import lax
from jax.experimental import pallas as pl
from jax.experimental.pallas import tpu as pltpu

D_MODEL = 1024
DIL_GROUPS = ((128, 1), (512, 4), (2048, 16))
N_GROUPS = len(DIL_GROUPS)
HEADS = 8
HEAD_DIM = 128
SPAN = 128
N_BUCKETS = 32
MAX_DISTANCE = 2048
CONV_WIDTH = 31
FFN_HIDDEN = 2816
EPS = 1e-6
NEG_INF = -1e30
SCALE = HEAD_DIM ** -0.5
LOG2E = math.log2(math.e)
LN2 = math.log(2.0)

F32 = jnp.float32
BF16 = jnp.bfloat16

VMEM_LIMIT_BYTES = 56 * 1024 * 1024


def _params(n_axes, vmem=VMEM_LIMIT_BYTES):
    return pltpu.CompilerParams(dimension_semantics=("arbitrary",) * n_axes, vmem_limit_bytes=vmem)


def _t5_bucket_np(dist):
    max_exact = N_BUCKETS // 2
    n = np.maximum(dist, 1).astype(np.float32)
    large = max_exact + (np.log(n / np.float32(max_exact)) / np.float32(math.log(MAX_DISTANCE / max_exact))
                         * np.float32(N_BUCKETS - max_exact)).astype(np.int32)
    large = np.minimum(large, N_BUCKETS - 1)
    return np.where(dist < max_exact, dist, large).astype(np.int32)


def _bucket_tables():
    qi = np.arange(SPAN)[:, None]
    ki = np.arange(2 * SPAN)[None, :]
    delta = qi + SPAN - ki
    in_band = (delta >= 0) & (delta <= SPAN)
    prompt, sample = [], []
    for _, dilation in DIL_GROUPS:
        b = _t5_bucket_np(np.clip(delta, 0, SPAN) * dilation)
        prompt.append(np.where(in_band, b, -1))
        j = SPAN - np.arange(136)
        sample.append(_t5_bucket_np(np.maximum(j, 0) * dilation)[:, None])
    return np.stack(prompt).astype(np.int32), np.stack(sample).astype(np.int32)


def _bias_kernel(rb_smem, rb_ref, bp_ref, bs_ref, op_ref, os_ref):
    g = pl.program_id(0)
    bp = bp_ref[...]
    in_prev_block = lax.broadcasted_iota(jnp.int32, bp.shape, 1) < SPAN
    for h in range(HEADS):
        acc = jnp.full(bp.shape, NEG_INF, F32)
        for b in range(N_BUCKETS):
            acc = jnp.where(bp == b, rb_smem[b, g * HEADS + h] * LOG2E, acc)
        op_ref[0, h] = acc
        op_ref[1, h] = jnp.where(in_prev_block, NEG_INF, acc)
    bs = bs_ref[...]
    acc = jnp.zeros((bs.shape[0], rb_ref.shape[1]), F32)
    for b in range(N_BUCKETS):
        acc = jnp.where(bs == b, rb_ref[b:b + 1, :], acc)
    os_ref[...] = acc


def _bias_tables(rel_bias):
    bp, bs = _bucket_tables()
    n_cols = rel_bias.shape[1]
    return pl.pallas_call(
        _bias_kernel,
        grid=(N_GROUPS,),
        in_specs=[
            pl.BlockSpec(memory_space=pltpu.SMEM),
            pl.BlockSpec((N_BUCKETS, n_cols), lambda g: (0, 0)),
            pl.BlockSpec((None, SPAN, 2 * SPAN), lambda g: (g, 0, 0)),
            pl.BlockSpec((None, 136, 1), lambda g: (g, 0, 0)),
        ],
        out_specs=[
            pl.BlockSpec((None, 2, HEADS, SPAN, 2 * SPAN), lambda g: (g, 0, 0, 0, 0)),
            pl.BlockSpec((None, 136, n_cols), lambda g: (g, 0, 0)),
        ],
        out_shape=[
            jax.ShapeDtypeStruct((N_GROUPS, 2, HEADS, SPAN, 2 * SPAN), F32),
            jax.ShapeDtypeStruct((N_GROUPS, 136, n_cols), F32),
        ],
        compiler_params=_params(1),
        name="bias_tables",
    )(rel_bias, rel_bias, jnp.asarray(bp), jnp.asarray(bs))


def _mod_kernel(c_ref, w_ref, b_ref, o_ref):
    c = c_ref[...]
    a = (c * jax.nn.sigmoid(c)).astype(BF16)
    o_ref[...] = jnp.dot(a, w_ref[...].astype(BF16), preferred_element_type=F32) + b_ref[...]


def _modulation(c_all, w_mod, b_mod):
    depth, d, n = w_mod.shape
    rows = c_all.shape[0]
    tn = 1536
    return pl.pallas_call(
        _mod_kernel,
        grid=(depth, n // tn),
        in_specs=[
            pl.BlockSpec((rows, d), lambda l, j: (0, 0)),
            pl.BlockSpec((None, d, tn), lambda l, j: (l, 0, j)),
            pl.BlockSpec((None, 1, tn), lambda l, j: (l, 0, j)),
        ],
        out_specs=pl.BlockSpec((None, rows, tn), lambda l, j: (l, 0, j)),
        out_shape=jax.ShapeDtypeStruct((depth, rows, n), F32),
        compiler_params=_params(2),
        name="modulation",
    )(c_all, w_mod, b_mod.reshape(depth, 1, n))


MOD_ROW_BLOCK = 8


def _mod_rows(ref, tiles_per_seq):
    rows = ref[...]
    if tiles_per_seq is None:
        return rows
    seq_id = pl.program_id(0) // tiles_per_seq
    pick = lax.broadcasted_iota(jnp.int32, rows.shape, 0) == seq_id
    return jnp.sum(jnp.where(pick, rows, 0.0), axis=0, keepdims=True)


def _normed_f32(x_ref, g_ref, sc_ref, sh_ref, tiles_per_seq):
    x = x_ref[...]
    y = x * lax.rsqrt(jnp.mean(x * x, axis=-1, keepdims=True) + EPS)
    return (y * g_ref[...]) * (1.0 + _mod_rows(sc_ref, tiles_per_seq)) + _mod_rows(sh_ref, tiles_per_seq)


def _normed(x_ref, g_ref, sc_ref, sh_ref, tiles_per_seq):
    return _normed_f32(x_ref, g_ref, sc_ref, sh_ref, tiles_per_seq).astype(BF16)


def _norm_mm_kernel(x_ref, g_ref, sc_ref, sh_ref, w_ref, o_ref, h_ref, *, tiles_per_seq):
    @pl.when(pl.program_id(1) == 0)
    def _():
        h_ref[...] = _normed(x_ref, g_ref, sc_ref, sh_ref, tiles_per_seq)
    o_ref[...] = jnp.dot(h_ref[...], w_ref[...], preferred_element_type=F32).astype(o_ref.dtype)


def _norm_glu_kernel(x_ref, g_ref, sc_ref, sh_ref, wa_ref, wg_ref, ba_ref, bg_ref, o_ref, h_ref, *, tiles_per_seq):
    @pl.when(pl.program_id(1) == 0)
    def _():
        h_ref[...] = _normed(x_ref, g_ref, sc_ref, sh_ref, tiles_per_seq)
    h = h_ref[...]
    a = jnp.dot(h, wa_ref[...], preferred_element_type=F32) + ba_ref[...]
    gate = jnp.dot(h, wg_ref[...], preferred_element_type=F32) + bg_ref[...]
    o_ref[...] = a * jax.nn.sigmoid(gate)


def _norm_swiglu_kernel(x_ref, g_ref, sc_ref, sh_ref, wg_ref, wu_ref, o_ref, h_ref, *, tiles_per_seq):
    @pl.when(pl.program_id(1) == 0)
    def _():
        h_ref[...] = _normed(x_ref, g_ref, sc_ref, sh_ref, tiles_per_seq)
    h = h_ref[...]
    gate = jnp.dot(h, wg_ref[...], preferred_element_type=F32)
    up = jnp.dot(h, wu_ref[...], preferred_element_type=F32)
    o_ref[...] = ((gate * jax.nn.sigmoid(gate)) * up).astype(o_ref.dtype)


class _LayerWeight:
    def __init__(self, stacked, layer):
        self.arr = stacked
        self.layer = layer

    def spec(self, block, index, **kwargs):
        layer = self.layer
        return pl.BlockSpec((None,) + tuple(block), lambda *i: (layer,) + tuple(index(*i)), **kwargs)


class _Mod:
    def __init__(self, mod_all, layer, per_token, rows_per_seq):
        self.layer = layer
        self.per_token = per_token
        self.rows_per_seq = rows_per_seq
        self.arr = mod_all
        self.first_seq_block = (mod_all.shape[1] - MOD_ROW_BLOCK) // MOD_ROW_BLOCK

    def spec(self, chunk, tm):
        layer = self.layer
        if self.per_token:
            return pl.BlockSpec((None, tm, D_MODEL), lambda *i: (layer, i[0], chunk))
        block = self.first_seq_block
        return pl.BlockSpec((None, MOD_ROW_BLOCK, D_MODEL), lambda *i: (layer, block, chunk))

    def tiles_per_seq(self, tm):
        return None if self.per_token else self.rows_per_seq // tm


def _call_norm_matmul(kind, x, gain, mod, chunks, weights, biases, *, tm, tn, n_cols, out_dtype, name):
    t = x.shape[0]
    kernel = {"mm": _norm_mm_kernel, "glu": _norm_glu_kernel, "swiglu": _norm_swiglu_kernel}[kind]
    kernel = functools.partial(kernel, tiles_per_seq=mod.tiles_per_seq(tm))
    chunk_shift, chunk_scale = chunks
    in_specs = [
        pl.BlockSpec((tm, D_MODEL), lambda i, j: (i, 0)),
        pl.BlockSpec((1, D_MODEL), lambda i, j: (0, 0)),
        mod.spec(chunk_scale, tm),
        mod.spec(chunk_shift, tm),
    ]
    args = [x, gain.reshape(1, D_MODEL), mod.arr, mod.arr]
    resident = dict(pipeline_mode=pl.Buffered(1)) if n_cols == tn else {}
    for w, col0 in weights:
        in_specs.append(w.spec((D_MODEL, tn), lambda i, j, col0=col0: (0, col0 + j), **resident))
        args.append(w.arr)
    for b, col0 in biases:
        in_specs.append(pl.BlockSpec((1, tn), lambda i, j, col0=col0: (0, col0 + j)))
        args.append(b)
    return pl.pallas_call(
        kernel,
        grid=(t // tm, n_cols // tn),
        in_specs=in_specs,
        out_specs=pl.BlockSpec((tm, tn), lambda i, j: (i, j)),
        out_shape=jax.ShapeDtypeStruct((t, n_cols), out_dtype),
        scratch_shapes=[pltpu.VMEM((tm, D_MODEL), BF16)],
        compiler_params=_params(2),
        name=name,
    )(*args)


def _mm_kernel(a_ref, w_ref, o_ref):
    o_ref[...] = jnp.dot(a_ref[...], w_ref[...], preferred_element_type=F32)


def _kv_tail(h, w_qkv, g, batch, seq, keep):
    tm = min(keep, 1024)
    per_seq = keep // tm
    first = (seq - keep) // tm
    return pl.pallas_call(
        _mm_kernel,
        grid=(batch * per_seq, 2),
        in_specs=[
            pl.BlockSpec((tm, D_MODEL), lambda i, j: ((i // per_seq) * (seq // tm) + first + i % per_seq, 0)),
            w_qkv.spec((D_MODEL, D_MODEL), lambda i, j: (0, 3 * g + 1 + j)),
        ],
        out_specs=pl.BlockSpec((None, None, tm, D_MODEL), lambda i, j: (i // per_seq, j, i % per_seq, 0)),
        out_shape=jax.ShapeDtypeStruct((batch, 2, keep, D_MODEL), F32),
        compiler_params=_params(2),
        name=f"kv_tail_g{g}",
    )(h, w_qkv.arr)


LANES = 128
N_LANE_SLABS = D_MODEL // LANES


def _qkv_prompt_kernel(x_ref, g_ref, sc_ref, sh_ref, w_ref, o0_ref, o1_ref, o2_ref, hn_ref, h_ref, slab_ref, *,
                       tiles_per_seq):
    tm = x_ref.shape[0]
    n = pl.program_id(1)
    o_refs = (o0_ref, o1_ref, o2_ref)

    @pl.when(n == 0)
    def _():
        h = _normed_f32(x_ref, g_ref, sc_ref, sh_ref, tiles_per_seq)
        hn_ref[...] = h.astype(BF16)
        h_ref[0] = h.astype(BF16)
        quarter, sixteenth = tm // 4, tm // 16
        for c in range(N_LANE_SLABS):
            lanes = slice(c * LANES, (c + 1) * LANES)
            slab_ref[0, c] = h[:, lanes]
            for b in range(4):
                part = slab_ref[0, c, pl.ds(b, quarter, stride=4), :]
                slab_ref[1, c, b * quarter:(b + 1) * quarter, :] = part
                h_ref[1, b * quarter:(b + 1) * quarter, lanes] = part.astype(BF16)
            for b in range(4):
                for a in range(4):
                    r = 4 * a + b
                    part = slab_ref[1, c, pl.ds(b * quarter + a, sixteenth, stride=4), :]
                    h_ref[2, r * sixteenth:(r + 1) * sixteenth, lanes] = part.astype(BF16)

    for gi, (_, dil) in enumerate(DIL_GROUPS):
        @pl.when(n == gi)
        def _(gi=gi, dil=dil):
            acc = jnp.dot(h_ref[gi], w_ref[...], preferred_element_type=F32).astype(BF16)
            rows = tm // dil
            for r in range(dil):
                o_refs[gi][r] = acc[r * rows:(r + 1) * rows, :]


def _qkv_prompt(x, gain, mod, w, batch, seq, tm):
    assert [dil for _, dil in DIL_GROUPS] == [1, 4, 16]
    tiles_per_seq = seq // tm
    out_specs, out_shapes = [], []
    for g, (_, dil) in enumerate(DIL_GROUPS):
        out_specs.append(pl.BlockSpec((None, dil, tm // dil, 3 * D_MODEL),
                                      lambda i, n: (i // tiles_per_seq, 0, i % tiles_per_seq, 0)))
        out_shapes.append(jax.ShapeDtypeStruct((batch, dil, seq // dil, 3 * D_MODEL), BF16))
    out_specs.append(pl.BlockSpec((tm, D_MODEL), lambda i, n: (i, 0)))
    out_shapes.append(jax.ShapeDtypeStruct((batch * seq, D_MODEL), BF16))
    return pl.pallas_call(
        functools.partial(_qkv_prompt_kernel, tiles_per_seq=tiles_per_seq),
        grid=(batch * tiles_per_seq, N_GROUPS),
        in_specs=[
            pl.BlockSpec((tm, D_MODEL), lambda i, n: (i, 0)),
            pl.BlockSpec((1, D_MODEL), lambda i, n: (0, 0)),
            mod.spec(1, tm),
            mod.spec(0, tm),
            w.spec((D_MODEL, 3 * D_MODEL), lambda i, n: (0, n)),
        ],
        out_specs=out_specs,
        out_shape=out_shapes,
        scratch_shapes=[pltpu.VMEM((N_GROUPS, tm, D_MODEL), BF16),
                        pltpu.VMEM((2, N_LANE_SLABS, tm, LANES), F32)],
        compiler_params=_params(2),
        name="qkv_prompt",
    )(x, gain.reshape(1, D_MODEL), mod.arr, mod.arr, w.arr)


def _mm_res_kernel(*refs, has_bias, final_norm, tiles_per_seq):
    a_ref, w_ref = refs[0], refs[1]
    k = 2
    b_ref = None
    if has_bias:
        b_ref = refs[k]
        k += 1
    x_ref, gate_ref = refs[k], refs[k + 1]
    k += 2
    gf_ref = None
    if final_norm:
        gf_ref = refs[k]
        k += 1
    o_ref = refs[k]
    out = jnp.dot(a_ref[...], w_ref[...], preferred_element_type=F32)
    if has_bias:
        out = out + b_ref[...]
    x = x_ref[...] + _mod_rows(gate_ref, tiles_per_seq) * out
    if final_norm:
        x = (x * lax.rsqrt(jnp.mean(x * x, axis=-1, keepdims=True) + EPS)) * gf_ref[...]
    o_ref[...] = x


def _matmul_residual(a, w, bias, x, mod, chunk_gate, *, tm, name, g_final=None):
    t, k = a.shape
    in_specs = [
        pl.BlockSpec((tm, k), lambda i: (i, 0)),
        w.spec((k, D_MODEL), lambda i: (0, 0), pipeline_mode=pl.Buffered(1)),
    ]
    args = [a, w.arr]
    if bias is not None:
        in_specs.append(pl.BlockSpec((1, D_MODEL), lambda i: (0, 0)))
        args.append(bias.reshape(1, D_MODEL))
    in_specs += [pl.BlockSpec((tm, D_MODEL), lambda i: (i, 0)), mod.spec(chunk_gate, tm)]
    args += [x, mod.arr]
    if g_final is not None:
        in_specs.append(pl.BlockSpec((1, D_MODEL), lambda i: (0, 0)))
        args.append(g_final.reshape(1, D_MODEL))
    return pl.pallas_call(
        functools.partial(_mm_res_kernel, has_bias=bias is not None, final_norm=g_final is not None,
                          tiles_per_seq=mod.tiles_per_seq(tm)),
        grid=(t // tm,),
        in_specs=in_specs,
        out_specs=pl.BlockSpec((tm, D_MODEL), lambda i: (i, 0)),
        out_shape=jax.ShapeDtypeStruct((t, D_MODEL), F32),
        compiler_params=_params(1),
        name=name,
    )(*args)


ATTN_TILES = 2
ATTN_ROWS = 1024


def _attn_prompt_kernel(q_ref, k_ref, v_ref, bias_ref, o_ref, lse_ref, kcat_ref, vcat_ref):
    step = pl.program_id(2)
    n_res, nq, _ = q_ref.shape
    lane = lax.broadcasted_iota(jnp.int32, (SPAN, HEAD_DIM), 1)
    dims = (((1,), (1,)), ((), ()))
    work = [(j, h) for j in range(ATTN_TILES) for h in range(HEADS)]
    cols = lambda h: slice(h * HEAD_DIM, (h + 1) * HEAD_DIM)

    for res in range(n_res):
        @pl.when(step == 0)
        def _():
            kcat_ref[:SPAN] = jnp.zeros((SPAN, D_MODEL), BF16)
            vcat_ref[:SPAN] = jnp.zeros((SPAN, D_MODEL), BF16)

        @pl.when(step > 0)
        def _():
            kcat_ref[:SPAN] = kcat_ref[nq:]
            vcat_ref[:SPAN] = vcat_ref[nq:]

        kcat_ref[SPAN:] = k_ref[res]
        vcat_ref[SPAN:] = v_ref[res]

        def tiles(it, carry, res=res):
            rows = lambda j: pl.ds(pl.multiple_of((it * ATTN_TILES + j) * SPAN, SPAN), SPAN)
            keys = lambda j: pl.ds(pl.multiple_of((it * ATTN_TILES + j) * SPAN, SPAN), 2 * SPAN)
            starts_seq = jnp.where((step == 0) & (it == 0), 1, 0)
            s, m, p, l, o = {}, {}, {}, {}, {}
            for j, h in work:
                bias = bias_ref[starts_seq if j == 0 else 0, h]
                qk = lax.dot_general(q_ref[res, rows(j), cols(h)], kcat_ref[keys(j), cols(h)], dims,
                                     preferred_element_type=F32)
                s[j, h] = qk * (SCALE * LOG2E) + bias
            for j, h in work:
                m[j, h] = jnp.max(jnp.maximum(s[j, h][:, :SPAN], s[j, h][:, SPAN:]), axis=-1, keepdims=True)
            for j, h in work:
                e = jnp.exp2(s[j, h] - m[j, h])
                l[j, h] = jnp.sum(e[:, :SPAN] + e[:, SPAN:], axis=-1, keepdims=True)
                p[j, h] = e.astype(BF16)
            for j, h in work:
                o[j, h] = jnp.dot(p[j, h], vcat_ref[keys(j), cols(h)], preferred_element_type=F32)
            for j in range(ATTN_TILES):
                lse_all = jnp.zeros((SPAN, HEAD_DIM), F32)
                for h in range(HEADS):
                    o_ref[res, rows(j), cols(h)] = o[j, h] / l[j, h]
                    lse_all = jnp.where(lane == h, (m[j, h] + jnp.log2(l[j, h])) * LN2, lse_all)
                lse_ref[res, rows(j), :] = lse_all
            return carry

        lax.fori_loop(0, nq // (ATTN_TILES * SPAN), tiles, 0)


def _attn_prompt(qkv_g, bias_p, g):
    batch, dil, sub, _ = qkv_g.shape
    nq = min(sub, ATTN_ROWS)
    n_res = min(dil, ATTN_ROWS // nq)
    assert n_res == 1 or nq == sub
    blk = (None, n_res, nq, D_MODEL)
    return pl.pallas_call(
        _attn_prompt_kernel,
        grid=(batch, dil // n_res, sub // nq),
        in_specs=[
            pl.BlockSpec(blk, lambda b, r, t: (b, r, t, 0)),
            pl.BlockSpec(blk, lambda b, r, t: (b, r, t, 1)),
            pl.BlockSpec(blk, lambda b, r, t: (b, r, t, 2)),
            pl.BlockSpec((None, 2, HEADS, SPAN, 2 * SPAN), lambda b, r, t: (g, 0, 0, 0, 0)),
        ],
        out_specs=[
            pl.BlockSpec(blk, lambda b, r, t: (b, r, t, 0)),
            pl.BlockSpec((None, n_res, nq, HEAD_DIM), lambda b, r, t: (b, r, t, 0)),
        ],
        out_shape=[
            jax.ShapeDtypeStruct((batch, dil, sub, D_MODEL), F32),
            jax.ShapeDtypeStruct((batch, dil, sub, HEAD_DIM), F32),
        ],
        scratch_shapes=[pltpu.VMEM((SPAN + nq, D_MODEL), BF16), pltpu.VMEM((SPAN + nq, D_MODEL), BF16)],
        compiler_params=_params(3),
        name=f"attn_prompt_g{g}",
    )(qkv_g, qkv_g, qkv_g, bias_p)


def _attn_sample_kernel(qkv_ref, k0_ref, v0_ref, k1_ref, v1_ref, k2_ref, v2_ref, bias_ref, a_ref, *, bb):
    kv_refs = ((k0_ref, v0_ref), (k1_ref, v1_ref), (k2_ref, v2_ref))
    for bi in range(bb):
        outs, lses = [], []
        for g, (k_ref, v_ref) in enumerate(kv_refs):
            q, k_new, v_new = qkv_ref[bi, 3 * g], qkv_ref[bi, 3 * g + 1], qkv_ref[bi, 3 * g + 2]
            s = jnp.sum(k_ref[bi] * q[None], axis=-1, keepdims=True) * SCALE + bias_ref[g, :SPAN]
            s_new = jnp.sum(k_new * q, axis=-1, keepdims=True) * SCALE + bias_ref[g, SPAN]
            m = jnp.maximum(jnp.max(s, axis=0), s_new)
            e = jnp.exp(s - m[None])
            e_new = jnp.exp(s_new - m)
            l = jnp.sum(e, axis=0) + e_new
            outs.append((jnp.sum(e * v_ref[bi], axis=0) + e_new * v_new) / l)
            lses.append(m + jnp.log(l))
        top = jnp.maximum(jnp.maximum(lses[0], lses[1]), lses[2])
        ws = [jnp.exp(lse - top) for lse in lses]
        den = ws[0] + ws[1] + ws[2]
        a_ref[bi] = (outs[0] * (ws[0] / den) + outs[1] * (ws[1] / den)) + outs[2] * (ws[2] / den)


def _attn_sample(qkv, caches, layer, bias_s):
    b = qkv.shape[0]
    bb = 4
    in_specs = [pl.BlockSpec((bb,) + qkv.shape[1:], lambda i: (i, 0, 0, 0))]
    args = [qkv]
    for g, (_, dil) in enumerate(DIL_GROUPS):
        cache = caches[g]
        assert cache.shape[3] == SPAN * dil
        view = cache.reshape(cache.shape[:3] + (SPAN, dil, HEADS, HEAD_DIM))
        for kv in range(2):
            in_specs.append(pl.BlockSpec((None, bb, None, SPAN, None, HEADS, HEAD_DIM),
                                         lambda i, kv=kv: (layer, i, kv, 0, 0, 0, 0)))
            args.append(view)
    in_specs.append(pl.BlockSpec(bias_s.shape, lambda i: (0, 0, 0, 0)))
    args.append(bias_s)
    return pl.pallas_call(
        functools.partial(_attn_sample_kernel, bb=bb),
        grid=(b // bb,),
        in_specs=in_specs,
        out_specs=pl.BlockSpec((bb, HEADS, HEAD_DIM), lambda i: (i, 0, 0)),
        out_shape=jax.ShapeDtypeStruct((b, HEADS, HEAD_DIM), F32),
        compiler_params=_params(1),
        name="attn_sample",
    )(*args)


def _combine_kernel(o0_ref, o1_ref, o2_ref, l0_ref, l1_ref, l2_ref, a_ref, os_ref, ls_ref):
    tm = a_ref.shape[0]
    for gi, o_ref, l_ref in ((1, o1_ref, l1_ref), (2, o2_ref, l2_ref)):
        dil = DIL_GROUPS[gi][1]
        rows = tm // dil
        for r in range(dil):
            ls_ref[gi - 1, pl.ds(r, rows, stride=dil), :] = l_ref[r]
            for h in range(HEADS):
                os_ref[gi - 1, h, pl.ds(r, rows, stride=dil), :] = o_ref[r, :, h * HEAD_DIM:(h + 1) * HEAD_DIM]
    l0, l1, l2 = l0_ref[0], ls_ref[0], ls_ref[1]
    m = jnp.maximum(jnp.maximum(l0, l1), l2)
    e0, e1, e2 = jnp.exp(l0 - m), jnp.exp(l1 - m), jnp.exp(l2 - m)
    den = e0 + e1 + e2
    w1, w2 = e1 / den, e2 / den
    for h in range(HEADS):
        sl = slice(h * HEAD_DIM, (h + 1) * HEAD_DIM)
        o0 = o0_ref[0, :, sl]
        o = o0 + (os_ref[0, h] - o0) * w1[:, h:h + 1] + (os_ref[1, h] - o0) * w2[:, h:h + 1]
        a_ref[:, sl] = o.astype(a_ref.dtype)


def _combine(outs, lses, seq, tm):
    batch = outs[0].shape[0]
    tiles_per_seq = seq // tm
    in_specs = []
    for width in (D_MODEL, HEAD_DIM):
        for _, dil in DIL_GROUPS:
            in_specs.append(pl.BlockSpec((None, dil, tm // dil, width),
                                         lambda i: (i // tiles_per_seq, 0, i % tiles_per_seq, 0)))
    return pl.pallas_call(
        _combine_kernel,
        grid=(batch * tiles_per_seq,),
        in_specs=in_specs,
        out_specs=pl.BlockSpec((tm, D_MODEL), lambda i: (i, 0)),
        out_shape=jax.ShapeDtypeStruct((batch * seq, D_MODEL), BF16),
        scratch_shapes=[pltpu.VMEM((N_GROUPS - 1, HEADS, tm, HEAD_DIM), F32),
                        pltpu.VMEM((N_GROUPS - 1, tm, HEAD_DIM), F32)],
        compiler_params=_params(1),
        name="combine_groups",
    )(*outs, *lses)


def _ln_silu(z, g_ref, b_ref):
    mu = jnp.mean(z, axis=-1, keepdims=True)
    zc = z - mu
    var = jnp.mean(zc * zc, axis=-1, keepdims=True)
    y = (zc * lax.rsqrt(var + EPS)) * g_ref[...] + b_ref[...]
    return y * jax.nn.sigmoid(y)


SUBLANES = 8
HALO = 32
CONV_ROWS = 128
CONV_CHAINS = 2


def _conv_prompt_kernel(u_ref, prev_ref, w_ref, bdw_ref, g_ref, b_ref, a_ref, ext_ref, z_ref, *, tiles_per_seq):
    tm = u_ref.shape[0]
    starts_seq = pl.program_id(0) % tiles_per_seq == 0
    n_shifted = tm + HALO - SUBLANES
    for c in range(N_LANE_SLABS):
        lanes = slice(c * LANES, (c + 1) * LANES)
        ext_ref[0, c, :HALO, :] = jnp.where(starts_seq, 0.0, prev_ref[:, lanes])
        ext_ref[0, c, HALO:, :] = u_ref[:, lanes]
        for s in range(1, SUBLANES):
            ext_ref[s, c, :n_shifted, :] = ext_ref[0, c, s:s + n_shifted, :]
    off = HALO - (CONV_WIDTH - 1)

    def strip(idx, carry):
        c = idx % N_LANE_SLABS
        r0 = pl.multiple_of((idx // N_LANE_SLABS) * CONV_ROWS, CONV_ROWS)
        accs = [jnp.zeros((CONV_ROWS, LANES), F32) for _ in range(CONV_CHAINS)]
        for shift in range(SUBLANES):
            taps = [k for k in range(CONV_WIDTH) if (off + k) % SUBLANES == shift]
            steps = [(off + k - shift) // SUBLANES for k in taps]
            n_window = SUBLANES * (steps[-1] - steps[0]) + CONV_ROWS
            window = ext_ref[shift, c, pl.ds(r0 + SUBLANES * steps[0], n_window), :]
            for k, q in zip(taps, steps):
                first = SUBLANES * (q - steps[0])
                accs[k % CONV_CHAINS] = accs[k % CONV_CHAINS] + window[first:first + CONV_ROWS] * w_ref[c, k:k + 1, :]
        z_ref[c, pl.ds(r0, CONV_ROWS), :] = sum(accs[1:], accs[0]) + bdw_ref[c]
        return carry

    lax.fori_loop(0, (tm // CONV_ROWS) * N_LANE_SLABS, strip, 0)
    z = jnp.concatenate([z_ref[c] for c in range(N_LANE_SLABS)], axis=-1)
    a_ref[...] = _ln_silu(z, g_ref, b_ref).astype(a_ref.dtype)


def _conv_prompt(u, w_dw, b_dw, ln_g, ln_b, seq, tm):
    t = u.shape[0]
    row = lambda v: v.reshape(1, D_MODEL)
    vec = pl.BlockSpec((1, D_MODEL), lambda i: (0, 0))
    slabs = lambda v: jnp.transpose(v.reshape(-1, N_LANE_SLABS, LANES), (1, 0, 2))
    return pl.pallas_call(
        functools.partial(_conv_prompt_kernel, tiles_per_seq=seq // tm),
        grid=(t // tm,),
        in_specs=[
            pl.BlockSpec((tm, D_MODEL), lambda i: (i, 0)),
            pl.BlockSpec((HALO, D_MODEL), lambda i: (jnp.maximum(i * (tm // HALO) - 1, 0), 0)),
            pl.BlockSpec((N_LANE_SLABS, CONV_WIDTH, LANES), lambda i: (0, 0, 0)),
            pl.BlockSpec((N_LANE_SLABS, 1, LANES), lambda i: (0, 0, 0)),
            vec, vec,
        ],
        out_specs=pl.BlockSpec((tm, D_MODEL), lambda i: (i, 0)),
        out_shape=jax.ShapeDtypeStruct((t, D_MODEL), BF16),
        scratch_shapes=[pltpu.VMEM((SUBLANES, N_LANE_SLABS, tm + HALO, LANES), F32),
                        pltpu.VMEM((N_LANE_SLABS, tm, LANES), F32)],
        compiler_params=_params(1),
        name="conv_prompt",
    )(u, u, slabs(w_dw), slabs(b_dw), row(ln_g), row(ln_b))


def _conv_sample_kernel(u_ref, st_ref, w_ref, bdw_ref, g_ref, b_ref, a_ref, ns_ref):
    bb, n_state, _ = st_ref.shape
    st = st_ref[...]
    z = jnp.sum(st * w_ref[:n_state, :][None], axis=1)
    z = z + u_ref[...] * w_ref[n_state:n_state + 1, :] + bdw_ref[...]
    a_ref[...] = _ln_silu(z, g_ref, b_ref).astype(a_ref.dtype)
    ns_ref[:, :n_state - 1, :] = st[:, 1:, :]
    for bi in range(bb):
        ns_ref[bi, n_state - 1:n_state, :] = u_ref[bi:bi + 1, :]


def _conv_sample(u, state, layer, w_dw, b_dw, ln_g, ln_b):
    _, b, n_state, _ = state.shape
    bb = 16
    row = lambda v: v.reshape(1, D_MODEL)
    vec = pl.BlockSpec((1, D_MODEL), lambda i: (0, 0))
    return pl.pallas_call(
        _conv_sample_kernel,
        grid=(b // bb,),
        in_specs=[
            pl.BlockSpec((bb, D_MODEL), lambda i: (i, 0)),
            pl.BlockSpec((None, bb, n_state, D_MODEL), lambda i: (layer, i, 0, 0)),
            pl.BlockSpec((CONV_WIDTH, D_MODEL), lambda i: (0, 0)),
            vec, vec, vec,
        ],
        out_specs=[
            pl.BlockSpec((bb, D_MODEL), lambda i: (i, 0)),
            pl.BlockSpec((None, bb, n_state, D_MODEL), lambda i: (0, i, 0, 0)),
        ],
        out_shape=[
            jax.ShapeDtypeStruct((b, D_MODEL), BF16),
            jax.ShapeDtypeStruct((1, b, n_state, D_MODEL), F32),
        ],
        compiler_params=_params(1),
        name="conv_sample",
    )(u, state, w_dw, row(b_dw), row(ln_g), row(ln_b))


def _ffn(x, mod, g_ffn, w_gate, w_up, w_down, *, tm_up, tm_down, tag, g_final=None):
    u = _call_norm_matmul("swiglu", x, g_ffn, mod, (3, 4), [(w_gate, 0), (w_up, 0)], [],
                          tm=tm_up, tn=FFN_HIDDEN, n_cols=FFN_HIDDEN, out_dtype=BF16,
                          name=f"ffn_up_{tag}")
    return _matmul_residual(u, w_down, None, x, mod, 5, tm=tm_down, name=f"ffn_down_{tag}", g_final=g_final)


def kernel(x_prompt, x_sample, cache_kv_w128, cache_kv_w512, cache_kv_w2048, state_conv, c_prompt, c_sample,
           w_mod, b_mod, g_mix, g_ffn, g_final, w_qkv, w_o, rel_bias, w_pw1, b_pw1, w_dw, b_dw, ln_g, ln_b,
           w_pw2, b_pw2, w_gate, w_up, w_down):
    batch, seq, d = x_prompt.shape
    dec_batch = x_sample.shape[0]
    assert d == D_MODEL and x_sample.shape[1] == 1
    caches = (cache_kv_w128, cache_kv_w512, cache_kv_w2048)

    def layers(w):
        w = w.astype(BF16)
        return [_LayerWeight(w, layer) for layer in range(w.shape[0])]

    w_qkv, w_o, w_pw1, w_pw2, w_gate, w_up, w_down = map(layers, (w_qkv, w_o, w_pw1, w_pw2, w_gate, w_up, w_down))

    c_all = jnp.concatenate([c_sample, c_prompt, jnp.zeros((8 - batch, d), F32)], axis=0)
    mod_all = _modulation(c_all, w_mod, b_mod)
    bias_p, bias_s = _bias_tables(rel_bias)

    xp = x_prompt.reshape(batch * seq, d)
    xs = x_sample.reshape(dec_batch, d)
    n_qkv = 3 * N_GROUPS * d

    mod_p = _Mod(mod_all, 0, False, seq)
    mod_s = _Mod(mod_all, 0, True, 1)

    qkv_p = _qkv_prompt(xp, g_mix[0], mod_p, w_qkv[0], batch, seq, 512)
    qkv_s = _call_norm_matmul("mm", xs, g_mix[0], mod_s, (0, 1), [(w_qkv[0], 0)], [],
                              tm=dec_batch, tn=1536, n_cols=n_qkv, out_dtype=F32, name="qkv_sample")

    kv_prompt = [_kv_tail(qkv_p[N_GROUPS], w_qkv[0], g, batch, seq, min(window, seq))
                 .reshape(1, batch, 2, min(window, seq), HEADS, HEAD_DIM) for g, (window, _) in enumerate(DIL_GROUPS)]

    outs, lses = zip(*[_attn_prompt(qkv_p[g], bias_p, g) for g in range(N_GROUPS)])
    a_p = _combine(outs, lses, seq, 1024)
    xp = _matmul_residual(a_p, w_o[0], None, xp, mod_p, 2, tm=1024, name="attn_out_prompt")
    xp = _ffn(xp, mod_p, g_ffn[0], w_gate[0], w_up[0], w_down[0], tm_up=512, tm_down=1024, tag="prompt0")

    qkv_s4 = qkv_s.reshape(dec_batch, 3 * N_GROUPS, HEADS, HEAD_DIM)
    bias_s4 = jnp.stack([bias_s[g, :, g * HEADS:(g + 1) * HEADS] for g in range(N_GROUPS)])[..., None]
    a_s = _attn_sample(qkv_s4, caches, 0, bias_s4).reshape(dec_batch, d).astype(BF16)
    xs = _matmul_residual(a_s, w_o[0], None, xs, mod_s, 2, tm=dec_batch, name="attn_out_sample")
    xs = _ffn(xs, mod_s, g_ffn[0], w_gate[0], w_up[0], w_down[0], tm_up=dec_batch, tm_down=dec_batch,
              tag="sample0")

    mod_p = _Mod(mod_all, 1, False, seq)
    mod_s = _Mod(mod_all, 1, True, 1)
    b_pw1_row = b_pw1[0].reshape(1, 2 * d)

    def glu(x, mod, tm, tag):
        return _call_norm_matmul("glu", x, g_mix[1], mod, (0, 1), [(w_pw1[0], 0), (w_pw1[0], 1)],
                                 [(b_pw1_row, 0), (b_pw1_row, 1)],
                                 tm=tm, tn=d, n_cols=d, out_dtype=F32, name=f"glu_{tag}")

    u_p = glu(xp, mod_p, 1024, "prompt")
    a_p = _conv_prompt(u_p, w_dw[0], b_dw[0], ln_g[0], ln_b[0], seq, 512)
    xp = _matmul_residual(a_p, w_pw2[0], b_pw2[0], xp, mod_p, 2, tm=1024, name="conv_out_prompt")
    y_p = _ffn(xp, mod_p, g_ffn[1], w_gate[1], w_up[1], w_down[1], tm_up=512, tm_down=1024, tag="prompt1",
               g_final=g_final)

    u_s = glu(xs, mod_s, dec_batch, "sample")
    a_s, conv_s = _conv_sample(u_s, state_conv, 0, w_dw[0], b_dw[0], ln_g[0], ln_b[0])
    xs = _matmul_residual(a_s, w_pw2[0], b_pw2[0], xs, mod_s, 2, tm=dec_batch, name="conv_out_sample")
    y_s = _ffn(xs, mod_s, g_ffn[1], w_gate[1], w_up[1], w_down[1], tm_up=dec_batch, tm_down=dec_batch,
               tag="sample1", g_final=g_final)

    conv_p = u_p.reshape(batch, seq, d)[:, seq - (CONV_WIDTH - 1):][None]
    kv_sample = [qkv_s4[:, 3 * g + 1:3 * g + 3].reshape(1, dec_batch, 2, 1, HEADS, HEAD_DIM)
                 for g in range(N_GROUPS)]
    return (y_p.reshape(batch, seq, d), y_s.reshape(dec_batch, 1, d),
            kv_prompt[0], kv_prompt[1], kv_prompt[2], conv_p,
            kv_sample[0], kv_sample[1], kv_sample[2], conv_s)
```

```python
import functools
import math

import numpy as np
import jax
import jax.numpy as jnp
from jax import lax
from jax.experimental import pallas as pl
from jax.experimental.pallas import tpu as pltpu

D_MODEL = 1024
DIL_GROUPS = ((128, 1), (512, 4), (2048, 16))
N_GROUPS = len(DIL_GROUPS)
HEADS = 8
HEAD_DIM = 128
SPAN = 128
N_BUCKETS = 32
MAX_DISTANCE = 2048
CONV_WIDTH = 31
FFN_HIDDEN = 2816
EPS = 1e-6
NEG_INF = -1e30
SCALE = HEAD_DIM ** -0.5
LOG2E = math.log2(math.e)
LN2 = math.log(2.0)

F32 = jnp.float32
BF16 = jnp.bfloat16

VMEM_LIMIT_BYTES = 56 * 1024 * 1024


def _params(n_axes, vmem=VMEM_LIMIT_BYTES):
    return pltpu.CompilerParams(dimension_semantics=("arbitrary",) * n_axes, vmem_limit_bytes=vmem)


def _t5_bucket_np(dist):
    max_exact = N_BUCKETS // 2
    n = np.maximum(dist, 1).astype(np.float32)
    large = max_exact + (np.log(n / np.float32(max_exact)) / np.float32(math.log(MAX_DISTANCE / max_exact))
                         * np.float32(N_BUCKETS - max_exact)).astype(np.int32)
    large = np.minimum(large, N_BUCKETS - 1)
    return np.where(dist < max_exact, dist, large).astype(np.int32)


def _bucket_tables():
    qi = np.arange(SPAN)[:, None]
    ki = np.arange(2 * SPAN)[None, :]
    delta = qi + SPAN - ki
    in_band = (delta >= 0) & (delta <= SPAN)
    prompt, sample = [], []
    for _, dilation in DIL_GROUPS:
        b = _t5_bucket_np(np.clip(delta, 0, SPAN) * dilation)
        prompt.append(np.where(in_band, b, -1))
        j = SPAN - np.arange(136)
        sample.append(_t5_bucket_np(np.maximum(j, 0) * dilation)[:, None])
    return np.stack(prompt).astype(np.int32), np.stack(sample).astype(np.int32)


def _bias_kernel(rb_smem, rb_ref, bp_ref, bs_ref, op_ref, os_ref):
    g = pl.program_id(0)
    bp = bp_ref[...]
    in_prev_block = lax.broadcasted_iota(jnp.int32, bp.shape, 1) < SPAN
    for h in range(HEADS):
        acc = jnp.full(bp.shape, NEG_INF, F32)
        for b in range(N_BUCKETS):
            acc = jnp.where(bp == b, rb_smem[b, g * HEADS + h] * LOG2E, acc)
        op_ref[0, h] = acc
        op_ref[1, h] = jnp.where(in_prev_block, NEG_INF, acc)
    bs = bs_ref[...]
    acc = jnp.zeros((bs.shape[0], rb_ref.shape[1]), F32)
    for b in range(N_BUCKETS):
        acc = jnp.where(bs == b, rb_ref[b:b + 1, :], acc)
    os_ref[...] = acc


def _bias_tables(rel_bias):
    bp, bs = _bucket_tables()
    n_cols = rel_bias.shape[1]
    return pl.pallas_call(
        _bias_kernel,
        grid=(N_GROUPS,),
        in_specs=[
            pl.BlockSpec(memory_space=pltpu.SMEM),
            pl.BlockSpec((N_BUCKETS, n_cols), lambda g: (0, 0)),
            pl.BlockSpec((None, SPAN, 2 * SPAN), lambda g: (g, 0, 0)),
            pl.BlockSpec((None, 136, 1), lambda g: (g, 0, 0)),
        ],
        out_specs=[
            pl.BlockSpec((None, 2, HEADS, SPAN, 2 * SPAN), lambda g: (g, 0, 0, 0, 0)),
            pl.BlockSpec((None, 136, n_cols), lambda g: (g, 0, 0)),
        ],
        out_shape=[
            jax.ShapeDtypeStruct((N_GROUPS, 2, HEADS, SPAN, 2 * SPAN), F32),
            jax.ShapeDtypeStruct((N_GROUPS, 136, n_cols), F32),
        ],
        compiler_params=_params(1),
        name="bias_tables",
    )(rel_bias, rel_bias, jnp.asarray(bp), jnp.asarray(bs))


def _mod_kernel(c_ref, w_ref, b_ref, o_ref):
    c = c_ref[...]
    a = (c * jax.nn.sigmoid(c)).astype(BF16)
    o_ref[...] = jnp.dot(a, w_ref[...].astype(BF16), preferred_element_type=F32) + b_ref[...]


def _modulation(c_all, w_mod, b_mod):
    depth, d, n = w_mod.shape
    rows = c_all.shape[0]
    tn = 1536
    return pl.pallas_call(
        _mod_kernel,
        grid=(depth, n // tn),
        in_specs=[
            pl.BlockSpec((rows, d), lambda l, j: (0, 0)),
            pl.BlockSpec((None, d, tn), lambda l, j: (l, 0, j)),
            pl.BlockSpec((None, 1, tn), lambda l, j: (l, 0, j)),
        ],
        out_specs=pl.BlockSpec((None, rows, tn), lambda l, j: (l, 0, j)),
        out_shape=jax.ShapeDtypeStruct((depth, rows, n), F32),
        compiler_params=_params(2),
        name="modulation",
    )(c_all, w_mod, b_mod.reshape(depth, 1, n))


MOD_ROW_BLOCK = 8


def _mod_rows(ref, tiles_per_seq, tile_axis=0):
    rows = ref[...]
    if tiles_per_seq is None:
        return rows
    seq_id = pl.program_id(tile_axis) // tiles_per_seq
    pick = lax.broadcasted_iota(jnp.int32, rows.shape, 0) == seq_id
    return jnp.sum(jnp.where(pick, rows, 0.0), axis=0, keepdims=True)


def _normed_f32(x_ref, g_ref, sc_ref, sh_ref, tiles_per_seq, tile_axis=0):
    x = x_ref[...]
    y = x * lax.rsqrt(jnp.mean(x * x, axis=-1, keepdims=True) + EPS)
    scale = _mod_rows(sc_ref, tiles_per_seq, tile_axis)
    return (y * g_ref[...]) * (1.0 + scale) + _mod_rows(sh_ref, tiles_per_seq, tile_axis)


def _normed(x_ref, g_ref, sc_ref, sh_ref, tiles_per_seq, tile_axis=0):
    return _normed_f32(x_ref, g_ref, sc_ref, sh_ref, tiles_per_seq, tile_axis).astype(BF16)


def _norm_mm_kernel(x_ref, g_ref, sc_ref, sh_ref, w_ref, o_ref, h_ref, *, tiles_per_seq):
    @pl.when(pl.program_id(1) == 0)
    def _():
        h_ref[...] = _normed(x_ref, g_ref, sc_ref, sh_ref, tiles_per_seq)
    o_ref[...] = jnp.dot(h_ref[...], w_ref[...], preferred_element_type=F32).astype(o_ref.dtype)


def _norm_glu_kernel(x_ref, g_ref, sc_ref, sh_ref, wa_ref, wg_ref, ba_ref, bg_ref, o_ref, h_ref, *, tiles_per_seq):
    @pl.when(pl.program_id(1) == 0)
    def _():
        h_ref[...] = _normed(x_ref, g_ref, sc_ref, sh_ref, tiles_per_seq)
    h = h_ref[...]
    a = jnp.dot(h, wa_ref[...], preferred_element_type=F32) + ba_ref[...]
    gate = jnp.dot(h, wg_ref[...], preferred_element_type=F32) + bg_ref[...]
    o_ref[...] = a * jax.nn.sigmoid(gate)


def _norm_swiglu_kernel(x_ref, g_ref, sc_ref, sh_ref, wg_ref, wu_ref, o_ref, h_ref, *, tiles_per_seq):
    @pl.when(pl.program_id(1) == 0)
    def _():
        h_ref[...] = _normed(x_ref, g_ref, sc_ref, sh_ref, tiles_per_seq)
    h = h_ref[...]
    gate = jnp.dot(h, wg_ref[...], preferred_element_type=F32)
    up = jnp.dot(h, wu_ref[...], preferred_element_type=F32)
    o_ref[...] = ((gate * jax.nn.sigmoid(gate)) * up).astype(o_ref.dtype)


class _LayerWeight:
    def __init__(self, stacked, layer):
        self.arr = stacked
        self.layer = layer

    def spec(self, block, index, **kwargs):
        layer = self.layer
        return pl.BlockSpec((None,) + tuple(block), lambda *i: (layer,) + tuple(index(*i)), **kwargs)


class _Mod:
    def __init__(self, mod_all, layer, per_token, rows_per_seq):
        self.layer = layer
        self.per_token = per_token
        self.rows_per_seq = rows_per_seq
        self.arr = mod_all
        self.first_seq_block = (mod_all.shape[1] - MOD_ROW_BLOCK) // MOD_ROW_BLOCK

    def spec(self, chunk, tm):
        layer = self.layer
        if self.per_token:
            return pl.BlockSpec((None, tm, D_MODEL), lambda *i: (layer, i[0], chunk))
        block = self.first_seq_block
        return pl.BlockSpec((None, MOD_ROW_BLOCK, D_MODEL), lambda *i: (layer, block, chunk))

    def tiles_per_seq(self, tm):
        return None if self.per_token else self.rows_per_seq // tm


def _call_norm_matmul(kind, x, gain, mod, chunks, weights, biases, *, tm, tn, n_cols, out_dtype, name):
    t = x.shape[0]
    kernel = {"mm": _norm_mm_kernel, "glu": _norm_glu_kernel, "swiglu": _norm_swiglu_kernel}[kind]
    kernel = functools.partial(kernel, tiles_per_seq=mod.tiles_per_seq(tm))
    chunk_shift, chunk_scale = chunks
    in_specs = [
        pl.BlockSpec((tm, D_MODEL), lambda i, j: (i, 0)),
        pl.BlockSpec((1, D_MODEL), lambda i, j: (0, 0)),
        mod.spec(chunk_scale, tm),
        mod.spec(chunk_shift, tm),
    ]
    args = [x, gain.reshape(1, D_MODEL), mod.arr, mod.arr]
    resident = dict(pipeline_mode=pl.Buffered(1)) if n_cols == tn else {}
    for w, col0 in weights:
        in_specs.append(w.spec((D_MODEL, tn), lambda i, j, col0=col0: (0, col0 + j), **resident))
        args.append(w.arr)
    for b, col0 in biases:
        in_specs.append(pl.BlockSpec((1, tn), lambda i, j, col0=col0: (0, col0 + j)))
        args.append(b)
    return pl.pallas_call(
        kernel,
        grid=(t // tm, n_cols // tn),
        in_specs=in_specs,
        out_specs=pl.BlockSpec((tm, tn), lambda i, j: (i, j)),
        out_shape=jax.ShapeDtypeStruct((t, n_cols), out_dtype),
        scratch_shapes=[pltpu.VMEM((tm, D_MODEL), BF16)],
        compiler_params=_params(2),
        name=name,
    )(*args)


def _mm_kernel(a_ref, w_ref, o_ref):
    o_ref[...] = jnp.dot(a_ref[...], w_ref[...], preferred_element_type=F32)


def _kv_tail(h, w_qkv, g, batch, seq, keep):
    tm = min(keep, 1024)
    per_seq = keep // tm
    first = (seq - keep) // tm
    return pl.pallas_call(
        _mm_kernel,
        grid=(2, batch * per_seq),
        in_specs=[
            pl.BlockSpec((tm, D_MODEL), lambda j, i: ((i // per_seq) * (seq // tm) + first + i % per_seq, 0)),
            w_qkv.spec((D_MODEL, D_MODEL), lambda j, i: (0, 3 * g + 1 + j)),
        ],
        out_specs=pl.BlockSpec((None, None, tm, D_MODEL), lambda j, i: (i // per_seq, j, i % per_seq, 0)),
        out_shape=jax.ShapeDtypeStruct((batch, 2, keep, D_MODEL), F32),
        compiler_params=_params(2),
        name=f"kv_tail_g{g}",
    )(h, w_qkv.arr)


LANES = 128
N_LANE_SLABS = D_MODEL // LANES


def _qkv_prompt_kernel(x_ref, g_ref, sc_ref, sh_ref, w_ref, o0_ref, o1_ref, o2_ref, hn_ref, h_ref, slab_ref, *,
                       tiles_per_seq):
    tm = x_ref.shape[0]
    group = pl.program_id(0)
    o_refs = (o0_ref, o1_ref, o2_ref)
    quarter, sixteenth = tm // 4, tm // 16

    def project(gi):
        acc = jnp.dot(h_ref[...], w_ref[...], preferred_element_type=F32).astype(BF16)
        dil = DIL_GROUPS[gi][1]
        rows = tm // dil
        for r in range(dil):
            o_refs[gi][r] = acc[r * rows:(r + 1) * rows, :]

    @pl.when(group == 0)
    def _():
        h = _normed(x_ref, g_ref, sc_ref, sh_ref, tiles_per_seq, tile_axis=1)
        hn_ref[...] = h
        h_ref[...] = h
        project(0)

    @pl.when(group == 1)
    def _():
        h = _normed_f32(x_ref, g_ref, sc_ref, sh_ref, tiles_per_seq, tile_axis=1)
        for c in range(N_LANE_SLABS):
            lanes = slice(c * LANES, (c + 1) * LANES)
            slab_ref[0, c] = h[:, lanes]
            for b in range(4):
                part = slab_ref[0, c, pl.ds(b, quarter, stride=4), :]
                h_ref[b * quarter:(b + 1) * quarter, lanes] = part.astype(BF16)
        project(1)

    @pl.when(group == 2)
    def _():
        h = _normed_f32(x_ref, g_ref, sc_ref, sh_ref, tiles_per_seq, tile_axis=1)
        for c in range(N_LANE_SLABS):
            lanes = slice(c * LANES, (c + 1) * LANES)
            slab_ref[0, c] = h[:, lanes]
            for b in range(4):
                slab_ref[1, c, b * quarter:(b + 1) * quarter, :] = slab_ref[0, c, pl.ds(b, quarter, stride=4), :]
            for b in range(4):
                for a in range(4):
                    r = 4 * a + b
                    part = slab_ref[1, c, pl.ds(b * quarter + a, sixteenth, stride=4), :]
                    h_ref[r * sixteenth:(r + 1) * sixteenth, lanes] = part.astype(BF16)
        project(2)


def _qkv_prompt(x, gain, mod, w, batch, seq, tm):
    assert [dil for _, dil in DIL_GROUPS] == [1, 4, 16]
    tiles_per_seq = seq // tm
    n_tiles = batch * tiles_per_seq

    def own_pass(g):
        return lambda n, i: jnp.where(n == g, i, jnp.where(n < g, 0, n_tiles - 1))

    out_specs, out_shapes = [], []
    for g, (_, dil) in enumerate(DIL_GROUPS):
        tile = own_pass(g)
        out_specs.append(pl.BlockSpec(
            (None, dil, tm // dil, 3 * D_MODEL),
            lambda n, i, tile=tile: (tile(n, i) // tiles_per_seq, 0, tile(n, i) % tiles_per_seq, 0)))
        out_shapes.append(jax.ShapeDtypeStruct((batch, dil, seq // dil, 3 * D_MODEL), BF16))
    out_specs.append(pl.BlockSpec((tm, D_MODEL), lambda n, i: (own_pass(0)(n, i), 0)))
    out_shapes.append(jax.ShapeDtypeStruct((batch * seq, D_MODEL), BF16))
    return pl.pallas_call(
        functools.partial(_qkv_prompt_kernel, tiles_per_seq=tiles_per_seq),
        grid=(N_GROUPS, n_tiles),
        in_specs=[
            pl.BlockSpec((tm, D_MODEL), lambda n, i: (i, 0)),
            pl.BlockSpec((1, D_MODEL), lambda n, i: (0, 0)),
            mod.spec(1, tm),
            mod.spec(0, tm),
            w.spec((D_MODEL, 3 * D_MODEL), lambda n, i: (0, n)),
        ],
        out_specs=out_specs,
        out_shape=out_shapes,
        scratch_shapes=[pltpu.VMEM((tm, D_MODEL), BF16),
                        pltpu.VMEM((2, N_LANE_SLABS, tm, LANES), F32)],
        compiler_params=_params(2),
        name="qkv_prompt",
    )(x, gain.reshape(1, D_MODEL), mod.arr, mod.arr, w.arr)


def _mm_res_kernel(*refs, has_bias, final_norm, tiles_per_seq):
    a_ref, w_ref = refs[0], refs[1]
    k = 2
    b_ref = None
    if has_bias:
        b_ref = refs[k]
        k += 1
    x_ref, gate_ref = refs[k], refs[k + 1]
    k += 2
    gf_ref = None
    if final_norm:
        gf_ref = refs[k]
        k += 1
    o_ref = refs[k]
    out = jnp.dot(a_ref[...], w_ref[...], preferred_element_type=F32)
    if has_bias:
        out = out + b_ref[...]
    x = x_ref[...] + _mod_rows(gate_ref, tiles_per_seq) * out
    if final_norm:
        x = (x * lax.rsqrt(jnp.mean(x * x, axis=-1, keepdims=True) + EPS)) * gf_ref[...]
    o_ref[...] = x


def _matmul_residual(a, w, bias, x, mod, chunk_gate, *, tm, name, g_final=None):
    t, k = a.shape
    in_specs = [
        pl.BlockSpec((tm, k), lambda i: (i, 0)),
        w.spec((k, D_MODEL), lambda i: (0, 0), pipeline_mode=pl.Buffered(1)),
    ]
    args = [a, w.arr]
    if bias is not None:
        in_specs.append(pl.BlockSpec((1, D_MODEL), lambda i: (0, 0)))
        args.append(bias.reshape(1, D_MODEL))
    in_specs += [pl.BlockSpec((tm, D_MODEL), lambda i: (i, 0)), mod.spec(chunk_gate, tm)]
    args += [x, mod.arr]
    if g_final is not None:
        in_specs.append(pl.BlockSpec((1, D_MODEL), lambda i: (0, 0)))
        args.append(g_final.reshape(1, D_MODEL))
    return pl.pallas_call(
        functools.partial(_mm_res_kernel, has_bias=bias is not None, final_norm=g_final is not None,
                          tiles_per_seq=mod.tiles_per_seq(tm)),
        grid=(t // tm,),
        in_specs=in_specs,
        out_specs=pl.BlockSpec((tm, D_MODEL), lambda i: (i, 0)),
        out_shape=jax.ShapeDtypeStruct((t, D_MODEL), F32),
        compiler_params=_params(1),
        name=name,
    )(*args)


ATTN_TILES = 2
ATTN_ROWS = 1024


def _attn_prompt_kernel(q_ref, k_ref, v_ref, bias_ref, o_ref, lse_ref, kcat_ref, vcat_ref):
    step = pl.program_id(2)
    n_res, nq, _ = q_ref.shape
    lane = lax.broadcasted_iota(jnp.int32, (SPAN, HEAD_DIM), 1)
    dims = (((1,), (1,)), ((), ()))
    work = [(j, h) for j in range(ATTN_TILES) for h in range(HEADS)]
    cols = lambda h: slice(h * HEAD_DIM, (h + 1) * HEAD_DIM)

    for res in range(n_res):
        @pl.when(step == 0)
        def _():
            kcat_ref[:SPAN] = jnp.zeros((SPAN, D_MODEL), BF16)
            vcat_ref[:SPAN] = jnp.zeros((SPAN, D_MODEL), BF16)

        @pl.when(step > 0)
        def _():
            kcat_ref[:SPAN] = kcat_ref[nq:]
            vcat_ref[:SPAN] = vcat_ref[nq:]

        kcat_ref[SPAN:] = k_ref[res]
        vcat_ref[SPAN:] = v_ref[res]

        def tiles(it, carry, res=res):
            rows = lambda j: pl.ds(pl.multiple_of((it * ATTN_TILES + j) * SPAN, SPAN), SPAN)
            keys = lambda j: pl.ds(pl.multiple_of((it * ATTN_TILES + j) * SPAN, SPAN), 2 * SPAN)
            starts_seq = jnp.where((step == 0) & (it == 0), 1, 0)
            s, m, p, l, o = {}, {}, {}, {}, {}
            for j, h in work:
                bias = bias_ref[starts_seq if j == 0 else 0, h]
                qk = lax.dot_general(q_ref[res, rows(j), cols(h)], kcat_ref[keys(j), cols(h)], dims,
                                     preferred_element_type=F32)
                s[j, h] = qk * (SCALE * LOG2E) + bias
            for j, h in work:
                m[j, h] = jnp.max(jnp.maximum(s[j, h][:, :SPAN], s[j, h][:, SPAN:]), axis=-1, keepdims=True)
            for j, h in work:
                e = jnp.exp2(s[j, h] - m[j, h])
                l[j, h] = jnp.sum(e[:, :SPAN] + e[:, SPAN:], axis=-1, keepdims=True)
                p[j, h] = e.astype(BF16)
            for j, h in work:
                o[j, h] = jnp.dot(p[j, h], vcat_ref[keys(j), cols(h)], preferred_element_type=F32)
            for j in range(ATTN_TILES):
                lse_all = jnp.zeros((SPAN, HEAD_DIM), F32)
                for h in range(HEADS):
                    o_ref[res, rows(j), cols(h)] = o[j, h] / l[j, h]
                    lse_all = jnp.where(lane == h, (m[j, h] + jnp.log2(l[j, h])) * LN2, lse_all)
                lse_ref[res, rows(j), :] = lse_all
            return carry

        lax.fori_loop(0, nq // (ATTN_TILES * SPAN), tiles, 0)


def _attn_prompt(qkv_g, bias_p, g):
    batch, dil, sub, _ = qkv_g.shape
    nq = min(sub, ATTN_ROWS)
    n_res = min(dil, ATTN_ROWS // nq)
    assert n_res == 1 or nq == sub
    blk = (None, n_res, nq, D_MODEL)
    return pl.pallas_call(
        _attn_prompt_kernel,
        grid=(batch, dil // n_res, sub // nq),
        in_specs=[
            pl.BlockSpec(blk, lambda b, r, t: (b, r, t, 0)),
            pl.BlockSpec(blk, lambda b, r, t: (b, r, t, 1)),
            pl.BlockSpec(blk, lambda b, r, t: (b, r, t, 2)),
            pl.BlockSpec((None, 2, HEADS, SPAN, 2 * SPAN), lambda b, r, t: (g, 0, 0, 0, 0)),
        ],
        out_specs=[
            pl.BlockSpec(blk, lambda b, r, t: (b, r, t, 0)),
            pl.BlockSpec((None, n_res, nq, HEAD_DIM), lambda b, r, t: (b, r, t, 0)),
        ],
        out_shape=[
            jax.ShapeDtypeStruct((batch, dil, sub, D_MODEL), F32),
            jax.ShapeDtypeStruct((batch, dil, sub, HEAD_DIM), F32),
        ],
        scratch_shapes=[pltpu.VMEM((SPAN + nq, D_MODEL), BF16), pltpu.VMEM((SPAN + nq, D_MODEL), BF16)],
        compiler_params=_params(3),
        name=f"attn_prompt_g{g}",
    )(qkv_g, qkv_g, qkv_g, bias_p)


def _attn_sample_kernel(qkv_ref, k0_ref, v0_ref, k1_ref, v1_ref, k2_ref, v2_ref, bias_ref, a_ref, *, bb):
    kv_refs = ((k0_ref, v0_ref), (k1_ref, v1_ref), (k2_ref, v2_ref))
    for bi in range(bb):
        outs, lses = [], []
        for g, (k_ref, v_ref) in enumerate(kv_refs):
            q, k_new, v_new = qkv_ref[bi, 3 * g], qkv_ref[bi, 3 * g + 1], qkv_ref[bi, 3 * g + 2]
            s = jnp.sum(k_ref[bi] * q[None], axis=-1, keepdims=True) * SCALE + bias_ref[g, :SPAN]
            s_new = jnp.sum(k_new * q, axis=-1, keepdims=True) * SCALE + bias_ref[g, SPAN]
            m = jnp.maximum(jnp.max(s, axis=0), s_new)
            e = jnp.exp(s - m[None])
            e_new = jnp.exp(s_new - m)
            l = jnp.sum(e, axis=0) + e_new
            outs.append((jnp.sum(e * v_ref[bi], axis=0) + e_new * v_new) / l)
            lses.append(m + jnp.log(l))
        top = jnp.maximum(jnp.maximum(lses[0], lses[1]), lses[2])
        ws = [jnp.exp(lse - top) for lse in lses]
        den = ws[0] + ws[1] + ws[2]
        a_ref[bi] = (outs[0] * (ws[0] / den) + outs[1] * (ws[1] / den)) + outs[2] * (ws[2] / den)


def _attn_sample(qkv, caches, layer, bias_s):
    b = qkv.shape[0]
    bb = 4
    in_specs = [pl.BlockSpec((bb,) + qkv.shape[1:], lambda i: (i, 0, 0, 0))]
    args = [qkv]
    for g, (_, dil) in enumerate(DIL_GROUPS):
        cache = caches[g]
        assert cache.shape[3] == SPAN * dil
        view = cache.reshape(cache.shape[:3] + (SPAN, dil, HEADS, HEAD_DIM))
        for kv in range(2):
            in_specs.append(pl.BlockSpec((None, bb, None, SPAN, None, HEADS, HEAD_DIM),
                                         lambda i, kv=kv: (layer, i, kv, 0, 0, 0, 0)))
            args.append(view)
    in_specs.append(pl.BlockSpec(bias_s.shape, lambda i: (0, 0, 0, 0)))
    args.append(bias_s)
    return pl.pallas_call(
        functools.partial(_attn_sample_kernel, bb=bb),
        grid=(b // bb,),
        in_specs=in_specs,
        out_specs=pl.BlockSpec((bb, HEADS, HEAD_DIM), lambda i: (i, 0, 0)),
        out_shape=jax.ShapeDtypeStruct((b, HEADS, HEAD_DIM), F32),
        compiler_params=_params(1),
        name="attn_sample",
    )(*args)


def _combine_kernel(o0_ref, o1_ref, o2_ref, l0_ref, l1_ref, l2_ref, a_ref, os_ref, ls_ref):
    tm = a_ref.shape[0]
    for gi, o_ref, l_ref in ((1, o1_ref, l1_ref), (2, o2_ref, l2_ref)):
        dil = DIL_GROUPS[gi][1]
        rows = tm // dil
        for r in range(dil):
            ls_ref[gi - 1, pl.ds(r, rows, stride=dil), :] = l_ref[r]
            for h in range(HEADS):
                os_ref[gi - 1, h, pl.ds(r, rows, stride=dil), :] = o_ref[r, :, h * HEAD_DIM:(h + 1) * HEAD_DIM]
    l0, l1, l2 = l0_ref[0], ls_ref[0], ls_ref[1]
    m = jnp.maximum(jnp.maximum(l0, l1), l2)
    e0, e1, e2 = jnp.exp(l0 - m), jnp.exp(l1 - m), jnp.exp(l2 - m)
    den = e0 + e1 + e2
    w1, w2 = e1 / den, e2 / den
    for h in range(HEADS):
        sl = slice(h * HEAD_DIM, (h + 1) * HEAD_DIM)
        o0 = o0_ref[0, :, sl]
        o = o0 + (os_ref[0, h] - o0) * w1[:, h:h + 1] + (os_ref[1, h] - o0) * w2[:, h:h + 1]
        a_ref[:, sl] = o.astype(a_ref.dtype)


def _combine(outs, lses, seq, tm):
    batch = outs[0].shape[0]
    tiles_per_seq = seq // tm
    in_specs = []
    for width in (D_MODEL, HEAD_DIM):
        for _, dil in DIL_GROUPS:
            in_specs.append(pl.BlockSpec((None, dil, tm // dil, width),
                                         lambda i: (i // tiles_per_seq, 0, i % tiles_per_seq, 0)))
    return pl.pallas_call(
        _combine_kernel,
        grid=(batch * tiles_per_seq,),
        in_specs=in_specs,
        out_specs=pl.BlockSpec((tm, D_MODEL), lambda i: (i, 0)),
        out_shape=jax.ShapeDtypeStruct((batch * seq, D_MODEL), BF16),
        scratch_shapes=[pltpu.VMEM((N_GROUPS - 1, HEADS, tm, HEAD_DIM), F32),
                        pltpu.VMEM((N_GROUPS - 1, tm, HEAD_DIM), F32)],
        compiler_params=_params(1),
        name="combine_groups",
    )(*outs, *lses)


def _ln_silu(z, g_ref, b_ref):
    mu = jnp.mean(z, axis=-1, keepdims=True)
    zc = z - mu
    var = jnp.mean(zc * zc, axis=-1, keepdims=True)
    y = (zc * lax.rsqrt(var + EPS)) * g_ref[...] + b_ref[...]
    return y * jax.nn.sigmoid(y)


SUBLANES = 8
HALO = 32
CONV_ROWS = 128
CONV_CHAINS = 2


def _conv_prompt_kernel(u_ref, prev_ref, w_ref, bdw_ref, g_ref, b_ref, a_ref, ext_ref, z_ref, *, tiles_per_seq):
    tm = u_ref.shape[0]
    starts_seq = pl.program_id(0) % tiles_per_seq == 0
    n_shifted = tm + HALO - SUBLANES
    for c in range(N_LANE_SLABS):
        lanes = slice(c * LANES, (c + 1) * LANES)
        ext_ref[0, c, :HALO, :] = jnp.where(starts_seq, 0.0, prev_ref[:, lanes])
        ext_ref[0, c, HALO:, :] = u_ref[:, lanes]
        for s in range(1, SUBLANES):
            ext_ref[s, c, :n_shifted, :] = ext_ref[0, c, s:s + n_shifted, :]
    off = HALO - (CONV_WIDTH - 1)

    def strip(idx, carry):
        c = idx % N_LANE_SLABS
        r0 = pl.multiple_of((idx // N_LANE_SLABS) * CONV_ROWS, CONV_ROWS)
        accs = [jnp.zeros((CONV_ROWS, LANES), F32) for _ in range(CONV_CHAINS)]
        for shift in range(SUBLANES):
            taps = [k for k in range(CONV_WIDTH) if (off + k) % SUBLANES == shift]
            steps = [(off + k - shift) // SUBLANES for k in taps]
            n_window = SUBLANES * (steps[-1] - steps[0]) + CONV_ROWS
            window = ext_ref[shift, c, pl.ds(r0 + SUBLANES * steps[0], n_window), :]
            for k, q in zip(taps, steps):
                first = SUBLANES * (q - steps[0])
                accs[k % CONV_CHAINS] = accs[k % CONV_CHAINS] + window[first:first + CONV_ROWS] * w_ref[c, k:k + 1, :]
        z_ref[c, pl.ds(r0, CONV_ROWS), :] = sum(accs[1:], accs[0]) + bdw_ref[c]
        return carry

    lax.fori_loop(0, (tm // CONV_ROWS) * N_LANE_SLABS, strip, 0)
    z = jnp.concatenate([z_ref[c] for c in range(N_LANE_SLABS)], axis=-1)
    a_ref[...] = _ln_silu(z, g_ref, b_ref).astype(a_ref.dtype)


def _conv_prompt(u, w_dw, b_dw, ln_g, ln_b, seq, tm):
    t = u.shape[0]
    row = lambda v: v.reshape(1, D_MODEL)
    vec = pl.BlockSpec((1, D_MODEL), lambda i: (0, 0))
    slabs = lambda v: jnp.transpose(v.reshape(-1, N_LANE_SLABS, LANES), (1, 0, 2))
    return pl.pallas_call(
        functools.partial(_conv_prompt_kernel, tiles_per_seq=seq // tm),
        grid=(t // tm,),
        in_specs=[
            pl.BlockSpec((tm, D_MODEL), lambda i: (i, 0)),
            pl.BlockSpec((HALO, D_MODEL), lambda i: (jnp.maximum(i * (tm // HALO) - 1, 0), 0)),
            pl.BlockSpec((N_LANE_SLABS, CONV_WIDTH, LANES), lambda i: (0, 0, 0)),
            pl.BlockSpec((N_LANE_SLABS, 1, LANES), lambda i: (0, 0, 0)),
            vec, vec,
        ],
        out_specs=pl.BlockSpec((tm, D_MODEL), lambda i: (i, 0)),
        out_shape=jax.ShapeDtypeStruct((t, D_MODEL), BF16),
        scratch_shapes=[pltpu.VMEM((SUBLANES, N_LANE_SLABS, tm + HALO, LANES), F32),
                        pltpu.VMEM((N_LANE_SLABS, tm, LANES), F32)],
        compiler_params=_params(1),
        name="conv_prompt",
    )(u, u, slabs(w_dw), slabs(b_dw), row(ln_g), row(ln_b))


def _conv_sample_kernel(u_ref, st_ref, w_ref, bdw_ref, g_ref, b_ref, a_ref, ns_ref):
    n_state = st_ref.shape[0]
    u = u_ref[...]
    z = u * w_ref[n_state:n_state + 1, :] + bdw_ref[...]
    for k in range(n_state):
        z = z + st_ref[k] * w_ref[k:k + 1, :]
    a_ref[...] = _ln_silu(z, g_ref, b_ref).astype(a_ref.dtype)
    for k in range(n_state - 1):
        ns_ref[k] = st_ref[k + 1]
    ns_ref[n_state - 1] = u


def _conv_sample(u, state, layer, w_dw, b_dw, ln_g, ln_b):
    _, b, n_state, _ = state.shape
    bb = 16
    row = lambda v: v.reshape(1, D_MODEL)
    vec = pl.BlockSpec((1, D_MODEL), lambda i: (0, 0))
    a, new_state = pl.pallas_call(
        _conv_sample_kernel,
        grid=(b // bb,),
        in_specs=[
            pl.BlockSpec((bb, D_MODEL), lambda i: (i, 0)),
            pl.BlockSpec((None, n_state, bb, D_MODEL), lambda i: (layer, 0, i, 0)),
            pl.BlockSpec((CONV_WIDTH, D_MODEL), lambda i: (0, 0)),
            vec, vec, vec,
        ],
        out_specs=[
            pl.BlockSpec((bb, D_MODEL), lambda i: (i, 0)),
            pl.BlockSpec((None, n_state, bb, D_MODEL), lambda i: (0, 0, i, 0)),
        ],
        out_shape=[
            jax.ShapeDtypeStruct((b, D_MODEL), BF16),
            jax.ShapeDtypeStruct((1, n_state, b, D_MODEL), F32),
        ],
        compiler_params=_params(1),
        name="conv_sample",
    )(u, jnp.transpose(state, (0, 2, 1, 3)), w_dw, row(b_dw), row(ln_g), row(ln_b))
    return a, jnp.transpose(new_state, (0, 2, 1, 3))


def _ffn(x, mod, g_ffn, w_gate, w_up, w_down, *, tm_up, tm_down, tag, g_final=None):
    u = _call_norm_matmul("swiglu", x, g_ffn, mod, (3, 4), [(w_gate, 0), (w_up, 0)], [],
                          tm=tm_up, tn=FFN_HIDDEN, n_cols=FFN_HIDDEN, out_dtype=BF16,
                          name=f"ffn_up_{tag}")
    return _matmul_residual(u, w_down, None, x, mod, 5, tm=tm_down, name=f"ffn_down_{tag}", g_final=g_final)


def kernel(x_prompt, x_sample, cache_kv_w128, cache_kv_w512, cache_kv_w2048, state_conv, c_prompt, c_sample,
           w_mod, b_mod, g_mix, g_ffn, g_final, w_qkv, w_o, rel_bias, w_pw1, b_pw1, w_dw, b_dw, ln_g, ln_b,
           w_pw2, b_pw2, w_gate, w_up, w_down):
    batch, seq, d = x_prompt.shape
    dec_batch = x_sample.shape[0]
    assert d == D_MODEL and x_sample.shape[1] == 1
    caches = (cache_kv_w128, cache_kv_w512, cache_kv_w2048)

    def layers(w):
        w = w.astype(BF16)
        return [_LayerWeight(w, layer) for layer in range(w.shape[0])]

    w_qkv, w_o, w_pw1, w_pw2, w_gate, w_up, w_down = map(layers, (w_qkv, w_o, w_pw1, w_pw2, w_gate, w_up, w_down))

    c_all = jnp.concatenate([c_sample, c_prompt, jnp.zeros((8 - batch, d), F32)], axis=0)
    mod_all = _modulation(c_all, w_mod, b_mod)
    bias_p, bias_s = _bias_tables(rel_bias)

    xp = x_prompt.reshape(batch * seq, d)
    xs = x_sample.reshape(dec_batch, d)
    n_qkv = 3 * N_GROUPS * d

    mod_p = _Mod(mod_all, 0, False, seq)
    mod_s = _Mod(mod_all, 0, True, 1)

    qkv_p = _qkv_prompt(xp, g_mix[0], mod_p, w_qkv[0], batch, seq, 512)
    qkv_s = _call_norm_matmul("mm", xs, g_mix[0], mod_s, (0, 1), [(w_qkv[0], 0)], [],
                              tm=dec_batch, tn=1536, n_cols=n_qkv, out_dtype=F32, name="qkv_sample")

    kv_prompt = [_kv_tail(qkv_p[N_GROUPS], w_qkv[0], g, batch, seq, min(window, seq))
                 .reshape(1, batch, 2, min(window, seq), HEADS, HEAD_DIM) for g, (window, _) in enumerate(DIL_GROUPS)]

    outs, lses = zip(*[_attn_prompt(qkv_p[g], bias_p, g) for g in range(N_GROUPS)])
    a_p = _combine(outs, lses, seq, 1024)
    xp = _matmul_residual(a_p, w_o[0], None, xp, mod_p, 2, tm=1024, name="attn_out_prompt")
    xp = _ffn(xp, mod_p, g_ffn[0], w_gate[0], w_up[0], w_down[0], tm_up=512, tm_down=1024, tag="prompt0")

    qkv_s4 = qkv_s.reshape(dec_batch, 3 * N_GROUPS, HEADS, HEAD_DIM)
    bias_s4 = jnp.stack([bias_s[g, :, g * HEADS:(g + 1) * HEADS] for g in range(N_GROUPS)])[..., None]
    a_s = _attn_sample(qkv_s4, caches, 0, bias_s4).reshape(dec_batch, d).astype(BF16)
    xs = _matmul_residual(a_s, w_o[0], None, xs, mod_s, 2, tm=dec_batch, name="attn_out_sample")
    xs = _ffn(xs, mod_s, g_ffn[0], w_gate[0], w_up[0], w_down[0], tm_up=dec_batch, tm_down=dec_batch,
              tag="sample0")

    mod_p = _Mod(mod_all, 1, False, seq)
    mod_s = _Mod(mod_all, 1, True, 1)
    b_pw1_row = b_pw1[0].reshape(1, 2 * d)

    def glu(x, mod, tm, tag):
        return _call_norm_matmul("glu", x, g_mix[1], mod, (0, 1), [(w_pw1[0], 0), (w_pw1[0], 1)],
                                 [(b_pw1_row, 0), (b_pw1_row, 1)],
                                 tm=tm, tn=d, n_cols=d, out_dtype=F32, name=f"glu_{tag}")

    u_p = glu(xp, mod_p, 1024, "prompt")
    a_p = _conv_prompt(u_p, w_dw[0], b_dw[0], ln_g[0], ln_b[0], seq, 512)
    xp = _matmul_residual(a_p, w_pw2[0], b_pw2[0], xp, mod_p, 2, tm=1024, name="conv_out_prompt")
    y_p = _ffn(xp, mod_p, g_ffn[1], w_gate[1], w_up[1], w_down[1], tm_up=512, tm_down=1024, tag="prompt1",
               g_final=g_final)

    u_s = glu(xs, mod_s, dec_batch, "sample")
    a_s, conv_s = _conv_sample(u_s, state_conv, 0, w_dw[0], b_dw[0], ln_g[0], ln_b[0])
    xs = _matmul_residual(a_s, w_pw2[0], b_pw2[0], xs, mod_s, 2, tm=dec_batch, name="conv_out_sample")
    y_s = _ffn(xs, mod_s, g_ffn[1], w_gate[1], w_up[1], w_down[1], tm_up=dec_batch, tm_down=dec_batch,
               tag="sample1", g_final=g_final)

    conv_p = u_p.reshape(batch, seq, d)[:, seq - (CONV_WIDTH - 1):][None]
    kv_sample = [qkv_s4[:, 3 * g + 1:3 * g + 3].reshape(1, dec_batch, 2, 1, HEADS, HEAD_DIM)
                 for g in range(N_GROUPS)]
    return (y_p.reshape(batch, seq, d), y_s.reshape(dec_batch, 1, d),
            kv_prompt[0], kv_prompt[1], kv_prompt[2], conv_p,
            kv_sample[0], kv_sample[1], kv_sample[2], conv_s)
```

```python
import functools
import math

import numpy as np
import jax
import jax.numpy as jnp
from jax import lax
from jax.experimental import pallas as pl
from jax.experimental.pallas import tpu as pltpu

D_MODEL = 1024
DIL_GROUPS = ((128, 1), (512, 4), (2048, 16))
N_GROUPS = len(DIL_GROUPS)
HEADS = 8
HEAD_DIM = 128
SPAN = 128
N_BUCKETS = 32
MAX_DISTANCE = 2048
CONV_WIDTH = 31
FFN_HIDDEN = 2816
EPS = 1e-6
NEG_INF = -1e30
SCALE = HEAD_DIM ** -0.5
LOG2E = math.log2(math.e)
LN2 = math.log(2.0)

F32 = jnp.float32
BF16 = jnp.bfloat16

VMEM_LIMIT_BYTES = 56 * 1024 * 1024


def _params(n_axes, vmem=VMEM_LIMIT_BYTES):
    return pltpu.CompilerParams(dimension_semantics=("arbitrary",) * n_axes, vmem_limit_bytes=vmem)


def _t5_bucket_np(dist):
    max_exact = N_BUCKETS // 2
    n = np.maximum(dist, 1).astype(np.float32)
    large = max_exact + (np.log(n / np.float32(max_exact)) / np.float32(math.log(MAX_DISTANCE / max_exact))
                         * np.float32(N_BUCKETS - max_exact)).astype(np.int32)
    large = np.minimum(large, N_BUCKETS - 1)
    return np.where(dist < max_exact, dist, large).astype(np.int32)


def _bucket_tables():
    qi = np.arange(SPAN)[:, None]
    ki = np.arange(2 * SPAN)[None, :]
    delta = qi + SPAN - ki
    in_band = (delta >= 0) & (delta <= SPAN)
    prompt, sample = [], []
    for _, dilation in DIL_GROUPS:
        b = _t5_bucket_np(np.clip(delta, 0, SPAN) * dilation)
        prompt.append(np.where(in_band, b, -1))
        j = SPAN - np.arange(136)
        sample.append(_t5_bucket_np(np.maximum(j, 0) * dilation)[:, None])
    return np.stack(prompt).astype(np.int32), np.stack(sample).astype(np.int32)


def _bias_kernel(rb_smem, rb_ref, bp_ref, bs_ref, op_ref, os_ref):
    g = pl.program_id(0)
    bp = bp_ref[...]
    in_prev_block = lax.broadcasted_iota(jnp.int32, bp.shape, 1) < SPAN
    for h in range(HEADS):
        acc = jnp.full(bp.shape, NEG_INF, F32)
        for b in range(N_BUCKETS):
            acc = jnp.where(bp == b, rb_smem[b, g * HEADS + h] * LOG2E, acc)
        op_ref[0, h] = acc
        op_ref[1, h] = jnp.where(in_prev_block, NEG_INF, acc)
    bs = bs_ref[...]
    acc = jnp.zeros((bs.shape[0], rb_ref.shape[1]), F32)
    for b in range(N_BUCKETS):
        acc = jnp.where(bs == b, rb_ref[b:b + 1, :], acc)
    os_ref[...] = acc


def _bias_tables(rel_bias):
    bp, bs = _bucket_tables()
    n_cols = rel_bias.shape[1]
    return pl.pallas_call(
        _bias_kernel,
        grid=(N_GROUPS,),
        in_specs=[
            pl.BlockSpec(memory_space=pltpu.SMEM),
            pl.BlockSpec((N_BUCKETS, n_cols), lambda g: (0, 0)),
            pl.BlockSpec((None, SPAN, 2 * SPAN), lambda g: (g, 0, 0)),
            pl.BlockSpec((None, 136, 1), lambda g: (g, 0, 0)),
        ],
        out_specs=[
            pl.BlockSpec((None, 2, HEADS, SPAN, 2 * SPAN), lambda g: (g, 0, 0, 0, 0)),
            pl.BlockSpec((None, 136, n_cols), lambda g: (g, 0, 0)),
        ],
        out_shape=[
            jax.ShapeDtypeStruct((N_GROUPS, 2, HEADS, SPAN, 2 * SPAN), F32),
            jax.ShapeDtypeStruct((N_GROUPS, 136, n_cols), F32),
        ],
        compiler_params=_params(1),
        name="bias_tables",
    )(rel_bias, rel_bias, jnp.asarray(bp), jnp.asarray(bs))


def _mod_kernel(c_ref, w_ref, b_ref, o_ref):
    c = c_ref[...]
    a = (c * jax.nn.sigmoid(c)).astype(BF16)
    o_ref[...] = jnp.dot(a, w_ref[...].astype(BF16), preferred_element_type=F32) + b_ref[...]


def _modulation(c_all, w_mod, b_mod):
    depth, d, n = w_mod.shape
    rows = c_all.shape[0]
    tn = 1536
    return pl.pallas_call(
        _mod_kernel,
        grid=(depth, n // tn),
        in_specs=[
            pl.BlockSpec((rows, d), lambda l, j: (0, 0)),
            pl.BlockSpec((None, d, tn), lambda l, j: (l, 0, j)),
            pl.BlockSpec((None, 1, tn), lambda l, j: (l, 0, j)),
        ],
        out_specs=pl.BlockSpec((None, rows, tn), lambda l, j: (l, 0, j)),
        out_shape=jax.ShapeDtypeStruct((depth, rows, n), F32),
        compiler_params=_params(2),
        name="modulation",
    )(c_all, w_mod, b_mod.reshape(depth, 1, n))


MOD_ROW_BLOCK = 8


def _mod_rows(ref, tiles_per_seq, tile_axis=0):
    rows = ref[...]
    if tiles_per_seq is None:
        return rows
    seq_id = pl.program_id(tile_axis) // tiles_per_seq
    pick = lax.broadcasted_iota(jnp.int32, rows.shape, 0) == seq_id
    return jnp.sum(jnp.where(pick, rows, 0.0), axis=0, keepdims=True)


def _normed_f32(x_ref, g_ref, sc_ref, sh_ref, tiles_per_seq, tile_axis=0):
    x = x_ref[...]
    y = x * lax.rsqrt(jnp.mean(x * x, axis=-1, keepdims=True) + EPS)
    scale = _mod_rows(sc_ref, tiles_per_seq, tile_axis)
    return (y * g_ref[...]) * (1.0 + scale) + _mod_rows(sh_ref, tiles_per_seq, tile_axis)


def _normed(x_ref, g_ref, sc_ref, sh_ref, tiles_per_seq, tile_axis=0):
    return _normed_f32(x_ref, g_ref, sc_ref, sh_ref, tiles_per_seq, tile_axis).astype(BF16)


def _norm_mm_kernel(x_ref, g_ref, sc_ref, sh_ref, w_ref, o_ref, h_ref, *, tiles_per_seq):
    @pl.when(pl.program_id(1) == 0)
    def _():
        h_ref[...] = _normed(x_ref, g_ref, sc_ref, sh_ref, tiles_per_seq)
    o_ref[...] = jnp.dot(h_ref[...], w_ref[...], preferred_element_type=F32).astype(o_ref.dtype)


def _norm_glu_kernel(x_ref, g_ref, sc_ref, sh_ref, wa_ref, wg_ref, ba_ref, bg_ref, o_ref, h_ref, *, tiles_per_seq):
    @pl.when(pl.program_id(1) == 0)
    def _():
        h_ref[...] = _normed(x_ref, g_ref, sc_ref, sh_ref, tiles_per_seq)
    h = h_ref[...]
    a = jnp.dot(h, wa_ref[...], preferred_element_type=F32) + ba_ref[...]
    gate = jnp.dot(h, wg_ref[...], preferred_element_type=F32) + bg_ref[...]
    o_ref[...] = a * jax.nn.sigmoid(gate)


class _LayerWeight:
    def __init__(self, stacked, layer):
        self.arr = stacked
        self.layer = layer

    def spec(self, block, index, **kwargs):
        layer = self.layer
        return pl.BlockSpec((None,) + tuple(block), lambda *i: (layer,) + tuple(index(*i)), **kwargs)


class _Mod:
    def __init__(self, mod_all, layer, per_token, rows_per_seq):
        self.layer = layer
        self.per_token = per_token
        self.rows_per_seq = rows_per_seq
        self.arr = mod_all
        self.first_seq_block = (mod_all.shape[1] - MOD_ROW_BLOCK) // MOD_ROW_BLOCK

    def spec(self, chunk, tm):
        layer = self.layer
        if self.per_token:
            return pl.BlockSpec((None, tm, D_MODEL), lambda *i: (layer, i[0], chunk))
        block = self.first_seq_block
        return pl.BlockSpec((None, MOD_ROW_BLOCK, D_MODEL), lambda *i: (layer, block, chunk))

    def tiles_per_seq(self, tm):
        return None if self.per_token else self.rows_per_seq // tm


def _call_norm_matmul(kind, x, gain, mod, chunks, weights, biases, *, tm, tn, n_cols, out_dtype, name):
    t = x.shape[0]
    kernel = {"mm": _norm_mm_kernel, "glu": _norm_glu_kernel}[kind]
    kernel = functools.partial(kernel, tiles_per_seq=mod.tiles_per_seq(tm))
    chunk_shift, chunk_scale = chunks
    in_specs = [
        pl.BlockSpec((tm, D_MODEL), lambda i, j: (i, 0)),
        pl.BlockSpec((1, D_MODEL), lambda i, j: (0, 0)),
        mod.spec(chunk_scale, tm),
        mod.spec(chunk_shift, tm),
    ]
    args = [x, gain.reshape(1, D_MODEL), mod.arr, mod.arr]
    resident = dict(pipeline_mode=pl.Buffered(1)) if n_cols == tn else {}
    for w, col0 in weights:
        in_specs.append(w.spec((D_MODEL, tn), lambda i, j, col0=col0: (0, col0 + j), **resident))
        args.append(w.arr)
    for b, col0 in biases:
        in_specs.append(pl.BlockSpec((1, tn), lambda i, j, col0=col0: (0, col0 + j)))
        args.append(b)
    return pl.pallas_call(
        kernel,
        grid=(t // tm, n_cols // tn),
        in_specs=in_specs,
        out_specs=pl.BlockSpec((tm, tn), lambda i, j: (i, j)),
        out_shape=jax.ShapeDtypeStruct((t, n_cols), out_dtype),
        scratch_shapes=[pltpu.VMEM((tm, D_MODEL), BF16)],
        compiler_params=_params(2),
        name=name,
    )(*args)


def _mm_kernel(a_ref, w_ref, o_ref):
    o_ref[...] = jnp.dot(a_ref[...], w_ref[...], preferred_element_type=F32)


def _kv_tail(h, w_qkv, g, batch, seq, keep):
    tm = min(keep, 1024)
    per_seq = keep // tm
    first = (seq - keep) // tm
    return pl.pallas_call(
        _mm_kernel,
        grid=(2, batch * per_seq),
        in_specs=[
            pl.BlockSpec((tm, D_MODEL), lambda j, i: ((i // per_seq) * (seq // tm) + first + i % per_seq, 0)),
            w_qkv.spec((D_MODEL, D_MODEL), lambda j, i: (0, 3 * g + 1 + j)),
        ],
        out_specs=pl.BlockSpec((None, None, tm, D_MODEL), lambda j, i: (i // per_seq, j, i % per_seq, 0)),
        out_shape=jax.ShapeDtypeStruct((batch, 2, keep, D_MODEL), F32),
        compiler_params=_params(2),
        name=f"kv_tail_g{g}",
    )(h, w_qkv.arr)


LANES = 128
N_LANE_SLABS = D_MODEL // LANES


def _qkv_prompt_kernel(x_ref, g_ref, sc_ref, sh_ref, w_ref, o0_ref, o1_ref, o2_ref, hn_ref, h_ref, slab_ref, *,
                       tiles_per_seq):
    tm = x_ref.shape[0]
    group = pl.program_id(0)
    o_refs = (o0_ref, o1_ref, o2_ref)
    quarter, sixteenth = tm // 4, tm // 16

    def project(gi):
        acc = jnp.dot(h_ref[...], w_ref[...], preferred_element_type=F32).astype(BF16)
        dil = DIL_GROUPS[gi][1]
        rows = tm // dil
        for r in range(dil):
            o_refs[gi][r] = acc[r * rows:(r + 1) * rows, :]

    @pl.when(group == 0)
    def _():
        h = _normed(x_ref, g_ref, sc_ref, sh_ref, tiles_per_seq, tile_axis=1)
        hn_ref[...] = h
        h_ref[...] = h
        project(0)

    @pl.when(group == 1)
    def _():
        h = _normed_f32(x_ref, g_ref, sc_ref, sh_ref, tiles_per_seq, tile_axis=1)
        for c in range(N_LANE_SLABS):
            lanes = slice(c * LANES, (c + 1) * LANES)
            slab_ref[0, c] = h[:, lanes]
            for b in range(4):
                part = slab_ref[0, c, pl.ds(b, quarter, stride=4), :]
                h_ref[b * quarter:(b + 1) * quarter, lanes] = part.astype(BF16)
        project(1)

    @pl.when(group == 2)
    def _():
        h = _normed_f32(x_ref, g_ref, sc_ref, sh_ref, tiles_per_seq, tile_axis=1)
        for c in range(N_LANE_SLABS):
            lanes = slice(c * LANES, (c + 1) * LANES)
            slab_ref[0, c] = h[:, lanes]
            for b in range(4):
                slab_ref[1, c, b * quarter:(b + 1) * quarter, :] = slab_ref[0, c, pl.ds(b, quarter, stride=4), :]
            for b in range(4):
                for a in range(4):
                    r = 4 * a + b
                    part = slab_ref[1, c, pl.ds(b * quarter + a, sixteenth, stride=4), :]
                    h_ref[r * sixteenth:(r + 1) * sixteenth, lanes] = part.astype(BF16)
        project(2)


def _qkv_prompt(x, gain, mod, w, batch, seq, tm):
    assert [dil for _, dil in DIL_GROUPS] == [1, 4, 16]
    tiles_per_seq = seq // tm
    n_tiles = batch * tiles_per_seq

    def own_pass(g):
        return lambda n, i: jnp.where(n == g, i, jnp.where(n < g, 0, n_tiles - 1))

    out_specs, out_shapes = [], []
    for g, (_, dil) in enumerate(DIL_GROUPS):
        tile = own_pass(g)
        out_specs.append(pl.BlockSpec(
            (None, dil, tm // dil, 3 * D_MODEL),
            lambda n, i, tile=tile: (tile(n, i) // tiles_per_seq, 0, tile(n, i) % tiles_per_seq, 0)))
        out_shapes.append(jax.ShapeDtypeStruct((batch, dil, seq // dil, 3 * D_MODEL), BF16))
    out_specs.append(pl.BlockSpec((tm, D_MODEL), lambda n, i: (own_pass(0)(n, i), 0)))
    out_shapes.append(jax.ShapeDtypeStruct((batch * seq, D_MODEL), BF16))
    return pl.pallas_call(
        functools.partial(_qkv_prompt_kernel, tiles_per_seq=tiles_per_seq),
        grid=(N_GROUPS, n_tiles),
        in_specs=[
            pl.BlockSpec((tm, D_MODEL), lambda n, i: (i, 0)),
            pl.BlockSpec((1, D_MODEL), lambda n, i: (0, 0)),
            mod.spec(1, tm),
            mod.spec(0, tm),
            w.spec((D_MODEL, 3 * D_MODEL), lambda n, i: (0, n)),
        ],
        out_specs=out_specs,
        out_shape=out_shapes,
        scratch_shapes=[pltpu.VMEM((tm, D_MODEL), BF16),
                        pltpu.VMEM((2, N_LANE_SLABS, tm, LANES), F32)],
        compiler_params=_params(2),
        name="qkv_prompt",
    )(x, gain.reshape(1, D_MODEL), mod.arr, mod.arr, w.arr)


def _mixer_ffn_kernel(*refs, has_bias, final_norm, tiles_per_seq):
    it = iter(refs)
    a_ref, wo_ref = next(it), next(it)
    bo_ref = next(it) if has_bias else None
    x_ref, g1_ref, gffn_ref, sc2_ref, sh2_ref, g2_ref = (next(it) for _ in range(6))
    wg_ref, wu_ref, wd_ref = next(it), next(it), next(it)
    gf_ref = next(it) if final_norm else None
    o_ref = next(it)
    rows = lambda ref: _mod_rows(ref, tiles_per_seq)

    out = jnp.dot(a_ref[...], wo_ref[...], preferred_element_type=F32)
    if has_bias:
        out = out + bo_ref[...]
    x = x_ref[...] + rows(g1_ref) * out
    y = x * lax.rsqrt(jnp.mean(x * x, axis=-1, keepdims=True) + EPS)
    h = ((y * gffn_ref[...]) * (1.0 + rows(sc2_ref)) + rows(sh2_ref)).astype(BF16)
    gate = jnp.dot(h, wg_ref[...], preferred_element_type=F32)
    up = jnp.dot(h, wu_ref[...], preferred_element_type=F32)
    u = ((gate * jax.nn.sigmoid(gate)) * up).astype(BF16)
    x = x + rows(g2_ref) * jnp.dot(u, wd_ref[...], preferred_element_type=F32)
    if final_norm:
        x = (x * lax.rsqrt(jnp.mean(x * x, axis=-1, keepdims=True) + EPS)) * gf_ref[...]
    o_ref[...] = x


def _mixer_ffn(a, w_out, b_out, x, mod, g_ffn, w_gate, w_up, w_down, *, tm, name, g_final=None):
    t, k = a.shape
    tile = pl.BlockSpec((tm, D_MODEL), lambda i: (i, 0))
    vec = pl.BlockSpec((1, D_MODEL), lambda i: (0, 0))
    once = dict(pipeline_mode=pl.Buffered(1))
    in_specs = [pl.BlockSpec((tm, k), lambda i: (i, 0)), w_out.spec((k, D_MODEL), lambda i: (0, 0), **once)]
    args = [a, w_out.arr]
    if b_out is not None:
        in_specs.append(vec)
        args.append(b_out.reshape(1, D_MODEL))
    in_specs += [tile, mod.spec(2, tm), vec, mod.spec(4, tm), mod.spec(3, tm), mod.spec(5, tm),
                 w_gate.spec((D_MODEL, FFN_HIDDEN), lambda i: (0, 0), **once),
                 w_up.spec((D_MODEL, FFN_HIDDEN), lambda i: (0, 0), **once),
                 w_down.spec((FFN_HIDDEN, D_MODEL), lambda i: (0, 0), **once)]
    args += [x, mod.arr, g_ffn.reshape(1, D_MODEL), mod.arr, mod.arr, mod.arr, w_gate.arr, w_up.arr, w_down.arr]
    if g_final is not None:
        in_specs.append(vec)
        args.append(g_final.reshape(1, D_MODEL))
    return pl.pallas_call(
        functools.partial(_mixer_ffn_kernel, has_bias=b_out is not None, final_norm=g_final is not None,
                          tiles_per_seq=mod.tiles_per_seq(tm)),
        grid=(t // tm,),
        in_specs=in_specs,
        out_specs=tile,
        out_shape=jax.ShapeDtypeStruct((t, D_MODEL), F32),
        compiler_params=_params(1),
        name=name,
    )(*args)


ATTN_TILES = 2
ATTN_ROWS = 1024


def _attn_prompt_kernel(q_ref, k_ref, v_ref, bias_ref, o_ref, lse_ref, kcat_ref, vcat_ref):
    step = pl.program_id(2)
    n_res, nq, _ = q_ref.shape
    lane = lax.broadcasted_iota(jnp.int32, (SPAN, HEAD_DIM), 1)
    dims = (((1,), (1,)), ((), ()))
    work = [(j, h) for j in range(ATTN_TILES) for h in range(HEADS)]
    cols = lambda h: slice(h * HEAD_DIM, (h + 1) * HEAD_DIM)

    for res in range(n_res):
        @pl.when(step == 0)
        def _():
            kcat_ref[:SPAN] = jnp.zeros((SPAN, D_MODEL), BF16)
            vcat_ref[:SPAN] = jnp.zeros((SPAN, D_MODEL), BF16)

        @pl.when(step > 0)
        def _():
            kcat_ref[:SPAN] = kcat_ref[nq:]
            vcat_ref[:SPAN] = vcat_ref[nq:]

        kcat_ref[SPAN:] = k_ref[res]
        vcat_ref[SPAN:] = v_ref[res]

        def tiles(it, carry, res=res):
            rows = lambda j: pl.ds(pl.multiple_of((it * ATTN_TILES + j) * SPAN, SPAN), SPAN)
            keys = lambda j: pl.ds(pl.multiple_of((it * ATTN_TILES + j) * SPAN, SPAN), 2 * SPAN)
            starts_seq = jnp.where((step == 0) & (it == 0), 1, 0)
            s, m, p, l, o = {}, {}, {}, {}, {}
            for j, h in work:
                bias = bias_ref[starts_seq if j == 0 else 0, h]
                qk = lax.dot_general(q_ref[res, rows(j), cols(h)], kcat_ref[keys(j), cols(h)], dims,
                                     preferred_element_type=F32)
                s[j, h] = qk * (SCALE * LOG2E) + bias
            for j, h in work:
                m[j, h] = jnp.max(jnp.maximum(s[j, h][:, :SPAN], s[j, h][:, SPAN:]), axis=-1, keepdims=True)
            for j, h in work:
                e = jnp.exp2(s[j, h] - m[j, h])
                l[j, h] = jnp.sum(e[:, :SPAN] + e[:, SPAN:], axis=-1, keepdims=True)
                p[j, h] = e.astype(BF16)
            for j, h in work:
                o[j, h] = jnp.dot(p[j, h], vcat_ref[keys(j), cols(h)], preferred_element_type=F32)
            for j in range(ATTN_TILES):
                lse_all = jnp.zeros((SPAN, HEAD_DIM), F32)
                for h in range(HEADS):
                    o_ref[res, rows(j), cols(h)] = o[j, h] / l[j, h]
                    lse_all = jnp.where(lane == h, (m[j, h] + jnp.log2(l[j, h])) * LN2, lse_all)
                lse_ref[res, rows(j), :] = lse_all
            return carry

        lax.fori_loop(0, nq // (ATTN_TILES * SPAN), tiles, 0)


def _attn_prompt(qkv_g, bias_p, g):
    batch, dil, sub, _ = qkv_g.shape
    nq = min(sub, ATTN_ROWS)
    n_res = min(dil, ATTN_ROWS // nq)
    assert n_res == 1 or nq == sub
    blk = (None, n_res, nq, D_MODEL)
    return pl.pallas_call(
        _attn_prompt_kernel,
        grid=(batch, dil // n_res, sub // nq),
        in_specs=[
            pl.BlockSpec(blk, lambda b, r, t: (b, r, t, 0)),
            pl.BlockSpec(blk, lambda b, r, t: (b, r, t, 1)),
            pl.BlockSpec(blk, lambda b, r, t: (b, r, t, 2)),
            pl.BlockSpec((None, 2, HEADS, SPAN, 2 * SPAN), lambda b, r, t: (g, 0, 0, 0, 0)),
        ],
        out_specs=[
            pl.BlockSpec(blk, lambda b, r, t: (b, r, t, 0)),
            pl.BlockSpec((None, n_res, nq, HEAD_DIM), lambda b, r, t: (b, r, t, 0)),
        ],
        out_shape=[
            jax.ShapeDtypeStruct((batch, dil, sub, D_MODEL), F32),
            jax.ShapeDtypeStruct((batch, dil, sub, HEAD_DIM), F32),
        ],
        scratch_shapes=[pltpu.VMEM((SPAN + nq, D_MODEL), BF16), pltpu.VMEM((SPAN + nq, D_MODEL), BF16)],
        compiler_params=_params(3),
        name=f"attn_prompt_g{g}",
    )(qkv_g, qkv_g, qkv_g, bias_p)


def _attn_sample_kernel(qkv_ref, k0_ref, v0_ref, k1_ref, v1_ref, k2_ref, v2_ref, bias_ref, a_ref, *, bb):
    kv_refs = ((k0_ref, v0_ref), (k1_ref, v1_ref), (k2_ref, v2_ref))
    for bi in range(bb):
        outs, lses = [], []
        for g, (k_ref, v_ref) in enumerate(kv_refs):
            q, k_new, v_new = qkv_ref[bi, 3 * g], qkv_ref[bi, 3 * g + 1], qkv_ref[bi, 3 * g + 2]
            s = jnp.sum(k_ref[bi] * q[None], axis=-1, keepdims=True) * SCALE + bias_ref[g, :SPAN]
            s_new = jnp.sum(k_new * q, axis=-1, keepdims=True) * SCALE + bias_ref[g, SPAN]
            m = jnp.maximum(jnp.max(s, axis=0), s_new)
            e = jnp.exp(s - m[None])
            e_new = jnp.exp(s_new - m)
            l = jnp.sum(e, axis=0) + e_new
            outs.append((jnp.sum(e * v_ref[bi], axis=0) + e_new * v_new) / l)
            lses.append(m + jnp.log(l))
        top = jnp.maximum(jnp.maximum(lses[0], lses[1]), lses[2])
        ws = [jnp.exp(lse - top) for lse in lses]
        den = ws[0] + ws[1] + ws[2]
        a_ref[bi] = (outs[0] * (ws[0] / den) + outs[1] * (ws[1] / den)) + outs[2] * (ws[2] / den)


def _attn_sample(qkv, caches, layer, bias_s):
    b = qkv.shape[0]
    bb = 4
    in_specs = [pl.BlockSpec((bb,) + qkv.shape[1:], lambda i: (i, 0, 0, 0))]
    args = [qkv]
    for g, (_, dil) in enumerate(DIL_GROUPS):
        cache = caches[g]
        assert cache.shape[3] == SPAN * dil
        view = cache.reshape(cache.shape[:3] + (SPAN, dil, HEADS, HEAD_DIM))
        for kv in range(2):
            in_specs.append(pl.BlockSpec((None, bb, None, SPAN, None, HEADS, HEAD_DIM),
                                         lambda i, kv=kv: (layer, i, kv, 0, 0, 0, 0)))
            args.append(view)
    in_specs.append(pl.BlockSpec(bias_s.shape, lambda i: (0, 0, 0, 0)))
    args.append(bias_s)
    return pl.pallas_call(
        functools.partial(_attn_sample_kernel, bb=bb),
        grid=(b // bb,),
        in_specs=in_specs,
        out_specs=pl.BlockSpec((bb, HEADS, HEAD_DIM), lambda i: (i, 0, 0)),
        out_shape=jax.ShapeDtypeStruct((b, HEADS, HEAD_DIM), F32),
        compiler_params=_params(1),
        name="attn_sample",
    )(*args)


def _combine_kernel(o0_ref, o1_ref, o2_ref, l0_ref, l1_ref, l2_ref, a_ref, os_ref, ls_ref):
    tm = a_ref.shape[0]
    for gi, o_ref, l_ref in ((1, o1_ref, l1_ref), (2, o2_ref, l2_ref)):
        dil = DIL_GROUPS[gi][1]
        rows = tm // dil
        for r in range(dil):
            ls_ref[gi - 1, pl.ds(r, rows, stride=dil), :] = l_ref[r]
            for h in range(HEADS):
                os_ref[gi - 1, h, pl.ds(r, rows, stride=dil), :] = o_ref[r, :, h * HEAD_DIM:(h + 1) * HEAD_DIM]
    l0, l1, l2 = l0_ref[0], ls_ref[0], ls_ref[1]
    m = jnp.maximum(jnp.maximum(l0, l1), l2)
    e0, e1, e2 = jnp.exp(l0 - m), jnp.exp(l1 - m), jnp.exp(l2 - m)
    den = e0 + e1 + e2
    w1, w2 = e1 / den, e2 / den
    for h in range(HEADS):
        sl = slice(h * HEAD_DIM, (h + 1) * HEAD_DIM)
        o0 = o0_ref[0, :, sl]
        o = o0 + (os_ref[0, h] - o0) * w1[:, h:h + 1] + (os_ref[1, h] - o0) * w2[:, h:h + 1]
        a_ref[:, sl] = o.astype(a_ref.dtype)


def _combine(outs, lses, seq, tm):
    batch = outs[0].shape[0]
    tiles_per_seq = seq // tm
    in_specs = []
    for width in (D_MODEL, HEAD_DIM):
        for _, dil in DIL_GROUPS:
            in_specs.append(pl.BlockSpec((None, dil, tm // dil, width),
                                         lambda i: (i // tiles_per_seq, 0, i % tiles_per_seq, 0)))
    return pl.pallas_call(
        _combine_kernel,
        grid=(batch * tiles_per_seq,),
        in_specs=in_specs,
        out_specs=pl.BlockSpec((tm, D_MODEL), lambda i: (i, 0)),
        out_shape=jax.ShapeDtypeStruct((batch * seq, D_MODEL), BF16),
        scratch_shapes=[pltpu.VMEM((N_GROUPS - 1, HEADS, tm, HEAD_DIM), F32),
                        pltpu.VMEM((N_GROUPS - 1, tm, HEAD_DIM), F32)],
        compiler_params=_params(1),
        name="combine_groups",
    )(*outs, *lses)


def _ln_silu(z, g_ref, b_ref):
    mu = jnp.mean(z, axis=-1, keepdims=True)
    zc = z - mu
    var = jnp.mean(zc * zc, axis=-1, keepdims=True)
    y = (zc * lax.rsqrt(var + EPS)) * g_ref[...] + b_ref[...]
    return y * jax.nn.sigmoid(y)


SUBLANES = 8
HALO = 32
CONV_ROWS = 128
CONV_CHAINS = 2


def _conv_prompt_kernel(u_ref, prev_ref, w_ref, bdw_ref, g_ref, b_ref, a_ref, ext_ref, z_ref, *, tiles_per_seq):
    tm = u_ref.shape[0]
    starts_seq = pl.program_id(0) % tiles_per_seq == 0
    n_shifted = tm + HALO - SUBLANES
    for c in range(N_LANE_SLABS):
        lanes = slice(c * LANES, (c + 1) * LANES)
        ext_ref[0, c, :HALO, :] = jnp.where(starts_seq, 0.0, prev_ref[:, lanes])
        ext_ref[0, c, HALO:, :] = u_ref[:, lanes]
        for s in range(1, SUBLANES):
            ext_ref[s, c, :n_shifted, :] = ext_ref[0, c, s:s + n_shifted, :]
    off = HALO - (CONV_WIDTH - 1)

    def strip(idx, carry):
        c = idx % N_LANE_SLABS
        r0 = pl.multiple_of((idx // N_LANE_SLABS) * CONV_ROWS, CONV_ROWS)
        accs = [jnp.zeros((CONV_ROWS, LANES), F32) for _ in range(CONV_CHAINS)]
        for shift in range(SUBLANES):
            taps = [k for k in range(CONV_WIDTH) if (off + k) % SUBLANES == shift]
            steps = [(off + k - shift) // SUBLANES for k in taps]
            n_window = SUBLANES * (steps[-1] - steps[0]) + CONV_ROWS
            window = ext_ref[shift, c, pl.ds(r0 + SUBLANES * steps[0], n_window), :]
            for k, q in zip(taps, steps):
                first = SUBLANES * (q - steps[0])
                accs[k % CONV_CHAINS] = accs[k % CONV_CHAINS] + window[first:first + CONV_ROWS] * w_ref[c, k:k + 1, :]
        z_ref[c, pl.ds(r0, CONV_ROWS), :] = sum(accs[1:], accs[0]) + bdw_ref[c]
        return carry

    lax.fori_loop(0, (tm // CONV_ROWS) * N_LANE_SLABS, strip, 0)
    z = jnp.concatenate([z_ref[c] for c in range(N_LANE_SLABS)], axis=-1)
    a_ref[...] = _ln_silu(z, g_ref, b_ref).astype(a_ref.dtype)


def _conv_prompt(u, w_dw, b_dw, ln_g, ln_b, seq, tm):
    t = u.shape[0]
    row = lambda v: v.reshape(1, D_MODEL)
    vec = pl.BlockSpec((1, D_MODEL), lambda i: (0, 0))
    slabs = lambda v: jnp.transpose(v.reshape(-1, N_LANE_SLABS, LANES), (1, 0, 2))
    return pl.pallas_call(
        functools.partial(_conv_prompt_kernel, tiles_per_seq=seq // tm),
        grid=(t // tm,),
        in_specs=[
            pl.BlockSpec((tm, D_MODEL), lambda i: (i, 0)),
            pl.BlockSpec((HALO, D_MODEL), lambda i: (jnp.maximum(i * (tm // HALO) - 1, 0), 0)),
            pl.BlockSpec((N_LANE_SLABS, CONV_WIDTH, LANES), lambda i: (0, 0, 0)),
            pl.BlockSpec((N_LANE_SLABS, 1, LANES), lambda i: (0, 0, 0)),
            vec, vec,
        ],
        out_specs=pl.BlockSpec((tm, D_MODEL), lambda i: (i, 0)),
        out_shape=jax.ShapeDtypeStruct((t, D_MODEL), BF16),
        scratch_shapes=[pltpu.VMEM((SUBLANES, N_LANE_SLABS, tm + HALO, LANES), F32),
                        pltpu.VMEM((N_LANE_SLABS, tm, LANES), F32)],
        compiler_params=_params(1),
        name="conv_prompt",
    )(u, u, slabs(w_dw), slabs(b_dw), row(ln_g), row(ln_b))


def _conv_sample_kernel(u_ref, st_ref, w_ref, bdw_ref, g_ref, b_ref, a_ref, ns_ref):
    n_state = st_ref.shape[0]
    u = u_ref[...]
    z = u * w_ref[n_state:n_state + 1, :] + bdw_ref[...]
    for k in range(n_state):
        z = z + st_ref[k] * w_ref[k:k + 1, :]
    a_ref[...] = _ln_silu(z, g_ref, b_ref).astype(a_ref.dtype)
    for k in range(n_state - 1):
        ns_ref[k] = st_ref[k + 1]
    ns_ref[n_state - 1] = u


def _conv_sample(u, state, layer, w_dw, b_dw, ln_g, ln_b):
    _, b, n_state, _ = state.shape
    bb = 16
    row = lambda v: v.reshape(1, D_MODEL)
    vec = pl.BlockSpec((1, D_MODEL), lambda i: (0, 0))
    a, new_state = pl.pallas_call(
        _conv_sample_kernel,
        grid=(b // bb,),
        in_specs=[
            pl.BlockSpec((bb, D_MODEL), lambda i: (i, 0)),
            pl.BlockSpec((None, n_state, bb, D_MODEL), lambda i: (layer, 0, i, 0)),
            pl.BlockSpec((CONV_WIDTH, D_MODEL), lambda i: (0, 0)),
            vec, vec, vec,
        ],
        out_specs=[
            pl.BlockSpec((bb, D_MODEL), lambda i: (i, 0)),
            pl.BlockSpec((None, n_state, bb, D_MODEL), lambda i: (0, 0, i, 0)),
        ],
        out_shape=[
            jax.ShapeDtypeStruct((b, D_MODEL), BF16),
            jax.ShapeDtypeStruct((1, n_state, b, D_MODEL), F32),
        ],
        compiler_params=_params(1),
        name="conv_sample",
    )(u, jnp.transpose(state, (0, 2, 1, 3)), w_dw, row(b_dw), row(ln_g), row(ln_b))
    return a, jnp.transpose(new_state, (0, 2, 1, 3))


PROMPT_FFN_ROWS = 512


def kernel(x_prompt, x_sample, cache_kv_w128, cache_kv_w512, cache_kv_w2048, state_conv, c_prompt, c_sample,
           w_mod, b_mod, g_mix, g_ffn, g_final, w_qkv, w_o, rel_bias, w_pw1, b_pw1, w_dw, b_dw, ln_g, ln_b,
           w_pw2, b_pw2, w_gate, w_up, w_down):
    batch, seq, d = x_prompt.shape
    dec_batch = x_sample.shape[0]
    assert d == D_MODEL and x_sample.shape[1] == 1
    caches = (cache_kv_w128, cache_kv_w512, cache_kv_w2048)

    def layers(w):
        w = w.astype(BF16)
        return [_LayerWeight(w, layer) for layer in range(w.shape[0])]

    w_qkv, w_o, w_pw1, w_pw2, w_gate, w_up, w_down = map(layers, (w_qkv, w_o, w_pw1, w_pw2, w_gate, w_up, w_down))

    c_all = jnp.concatenate([c_sample, c_prompt, jnp.zeros((8 - batch, d), F32)], axis=0)
    mod_all = _modulation(c_all, w_mod, b_mod)
    bias_p, bias_s = _bias_tables(rel_bias)

    xp = x_prompt.reshape(batch * seq, d)
    xs = x_sample.reshape(dec_batch, d)
    n_qkv = 3 * N_GROUPS * d

    mod_p = _Mod(mod_all, 0, False, seq)
    mod_s = _Mod(mod_all, 0, True, 1)

    qkv_p = _qkv_prompt(xp, g_mix[0], mod_p, w_qkv[0], batch, seq, 512)
    qkv_s = _call_norm_matmul("mm", xs, g_mix[0], mod_s, (0, 1), [(w_qkv[0], 0)], [],
                              tm=dec_batch, tn=1536, n_cols=n_qkv, out_dtype=F32, name="qkv_sample")

    kv_prompt = [_kv_tail(qkv_p[N_GROUPS], w_qkv[0], g, batch, seq, min(window, seq))
                 .reshape(1, batch, 2, min(window, seq), HEADS, HEAD_DIM) for g, (window, _) in enumerate(DIL_GROUPS)]

    outs, lses = zip(*[_attn_prompt(qkv_p[g], bias_p, g) for g in range(N_GROUPS)])
    a_p = _combine(outs, lses, seq, 1024)
    xp = _mixer_ffn(a_p, w_o[0], None, xp, mod_p, g_ffn[0], w_gate[0], w_up[0], w_down[0],
                    tm=PROMPT_FFN_ROWS, name="attn_out_ffn_prompt")

    qkv_s4 = qkv_s.reshape(dec_batch, 3 * N_GROUPS, HEADS, HEAD_DIM)
    bias_s4 = jnp.stack([bias_s[g, :, g * HEADS:(g + 1) * HEADS] for g in range(N_GROUPS)])[..., None]
    a_s = _attn_sample(qkv_s4, caches, 0, bias_s4).reshape(dec_batch, d).astype(BF16)
    xs = _mixer_ffn(a_s, w_o[0], None, xs, mod_s, g_ffn[0], w_gate[0], w_up[0], w_down[0],
                    tm=dec_batch, name="attn_out_ffn_sample")

    mod_p = _Mod(mod_all, 1, False, seq)
    mod_s = _Mod(mod_all, 1, True, 1)
    b_pw1_row = b_pw1[0].reshape(1, 2 * d)

    def glu(x, mod, tm, tag):
        return _call_norm_matmul("glu", x, g_mix[1], mod, (0, 1), [(w_pw1[0], 0), (w_pw1[0], 1)],
                                 [(b_pw1_row, 0), (b_pw1_row, 1)],
                                 tm=tm, tn=d, n_cols=d, out_dtype=F32, name=f"glu_{tag}")

    u_p = glu(xp, mod_p, 1024, "prompt")
    a_p = _conv_prompt(u_p, w_dw[0], b_dw[0], ln_g[0], ln_b[0], seq, 512)
    y_p = _mixer_ffn(a_p, w_pw2[0], b_pw2[0], xp, mod_p, g_ffn[1], w_gate[1], w_up[1], w_down[1],
                     tm=PROMPT_FFN_ROWS, name="conv_out_ffn_prompt", g_final=g_final)

    u_s = glu(xs, mod_s, dec_batch, "sample")
    a_s, conv_s = _conv_sample(u_s, state_conv, 0, w_dw[0], b_dw[0], ln_g[0], ln_b[0])
    y_s = _mixer_ffn(a_s, w_pw2[0], b_pw2[0], xs, mod_s, g_ffn[1], w_gate[1], w_up[1], w_down[1],
                     tm=dec_batch, name="conv_out_ffn_sample", g_final=g_final)

    conv_p = u_p.reshape(batch, seq, d)[:, seq - (CONV_WIDTH - 1):][None]
    kv_sample = [qkv_s4[:, 3 * g + 1:3 * g + 3].reshape(1, dec_batch, 2, 1, HEADS, HEAD_DIM)
                 for g in range(N_GROUPS)]
    return (y_p.reshape(batch, seq, d), y_s.reshape(dec_batch, 1, d),
            kv_prompt[0], kv_prompt[1], kv_prompt[2], conv_p,
            kv_sample[0], kv_sample[1], kv_sample[2], conv_s)
```

```python
import functools
import math

import numpy as np
import jax
import jax.numpy as jnp
from jax import lax
from jax.experimental import pallas as pl
from jax.experimental.pallas import tpu as pltpu

D_MODEL = 1024
DIL_GROUPS = ((128, 1), (512, 4), (2048, 16))
N_GROUPS = len(DIL_GROUPS)
HEADS = 8
HEAD_DIM = 128
SPAN = 128
N_BUCKETS = 32
MAX_DISTANCE = 2048
CONV_WIDTH = 31
FFN_HIDDEN = 2816
EPS = 1e-6
NEG_INF = -1e30
SCALE = HEAD_DIM ** -0.5
LOG2E = math.log2(math.e)
LN2 = math.log(2.0)

F32 = jnp.float32
BF16 = jnp.bfloat16

VMEM_LIMIT_BYTES = 56 * 1024 * 1024


def _params(n_axes, vmem=VMEM_LIMIT_BYTES):
    return pltpu.CompilerParams(dimension_semantics=("arbitrary",) * n_axes, vmem_limit_bytes=vmem)


def _t5_bucket_np(dist):
    max_exact = N_BUCKETS // 2
    n = np.maximum(dist, 1).astype(np.float32)
    large = max_exact + (np.log(n / np.float32(max_exact)) / np.float32(math.log(MAX_DISTANCE / max_exact))
                         * np.float32(N_BUCKETS - max_exact)).astype(np.int32)
    large = np.minimum(large, N_BUCKETS - 1)
    return np.where(dist < max_exact, dist, large).astype(np.int32)


def _bucket_tables():
    qi = np.arange(SPAN)[:, None]
    ki = np.arange(2 * SPAN)[None, :]
    delta = qi + SPAN - ki
    in_band = (delta >= 0) & (delta <= SPAN)
    prompt, sample = [], []
    for _, dilation in DIL_GROUPS:
        b = _t5_bucket_np(np.clip(delta, 0, SPAN) * dilation)
        prompt.append(np.where(in_band, b, -1))
        j = SPAN - np.arange(136)
        sample.append(_t5_bucket_np(np.maximum(j, 0) * dilation)[:, None])
    return np.stack(prompt).astype(np.int32), np.stack(sample).astype(np.int32)


def _bias_kernel(rb_smem, rb_ref, bp_ref, bs_ref, op_ref, os_ref):
    g = pl.program_id(0)
    bp = bp_ref[...]
    in_prev_block = lax.broadcasted_iota(jnp.int32, bp.shape, 1) < SPAN
    for h in range(HEADS):
        acc = jnp.full(bp.shape, NEG_INF, F32)
        for b in range(N_BUCKETS):
            acc = jnp.where(bp == b, rb_smem[b, g * HEADS + h] * LOG2E, acc)
        op_ref[0, h] = acc
        op_ref[1, h] = jnp.where(in_prev_block, NEG_INF, acc)
    bs = bs_ref[...]
    acc = jnp.zeros((bs.shape[0], rb_ref.shape[1]), F32)
    for b in range(N_BUCKETS):
        acc = jnp.where(bs == b, rb_ref[b:b + 1, :], acc)
    os_ref[...] = acc


def _bias_tables(rel_bias):
    bp, bs = _bucket_tables()
    n_cols = rel_bias.shape[1]
    return pl.pallas_call(
        _bias_kernel,
        grid=(N_GROUPS,),
        in_specs=[
            pl.BlockSpec(memory_space=pltpu.SMEM),
            pl.BlockSpec((N_BUCKETS, n_cols), lambda g: (0, 0)),
            pl.BlockSpec((None, SPAN, 2 * SPAN), lambda g: (g, 0, 0)),
            pl.BlockSpec((None, 136, 1), lambda g: (g, 0, 0)),
        ],
        out_specs=[
            pl.BlockSpec((None, 2, HEADS, SPAN, 2 * SPAN), lambda g: (g, 0, 0, 0, 0)),
            pl.BlockSpec((None, 136, n_cols), lambda g: (g, 0, 0)),
        ],
        out_shape=[
            jax.ShapeDtypeStruct((N_GROUPS, 2, HEADS, SPAN, 2 * SPAN), F32),
            jax.ShapeDtypeStruct((N_GROUPS, 136, n_cols), F32),
        ],
        compiler_params=_params(1),
        name="bias_tables",
    )(rel_bias, rel_bias, jnp.asarray(bp), jnp.asarray(bs))


def _mod_kernel(c_ref, w_ref, b_ref, o_ref):
    c = c_ref[...]
    a = (c * jax.nn.sigmoid(c)).astype(BF16)
    o_ref[...] = jnp.dot(a, w_ref[...].astype(BF16), preferred_element_type=F32) + b_ref[...]


def _modulation(c_all, w_mod, b_mod):
    depth, d, n = w_mod.shape
    rows = c_all.shape[0]
    tn = 1536
    return pl.pallas_call(
        _mod_kernel,
        grid=(depth, n // tn),
        in_specs=[
            pl.BlockSpec((rows, d), lambda l, j: (0, 0)),
            pl.BlockSpec((None, d, tn), lambda l, j: (l, 0, j)),
            pl.BlockSpec((None, 1, tn), lambda l, j: (l, 0, j)),
        ],
        out_specs=pl.BlockSpec((None, rows, tn), lambda l, j: (l, 0, j)),
        out_shape=jax.ShapeDtypeStruct((depth, rows, n), F32),
        compiler_params=_params(2),
        name="modulation",
    )(c_all, w_mod, b_mod.reshape(depth, 1, n))


MOD_ROW_BLOCK = 8


def _mod_rows(ref, tiles_per_seq, tile_axis=0):
    rows = ref[...]
    if tiles_per_seq is None:
        return rows
    seq_id = pl.program_id(tile_axis) // tiles_per_seq
    pick = lax.broadcasted_iota(jnp.int32, rows.shape, 0) == seq_id
    return jnp.sum(jnp.where(pick, rows, 0.0), axis=0, keepdims=True)


def _normed_f32(x_ref, g_ref, sc_ref, sh_ref, tiles_per_seq, tile_axis=0):
    x = x_ref[...]
    y = x * lax.rsqrt(jnp.mean(x * x, axis=-1, keepdims=True) + EPS)
    scale = _mod_rows(sc_ref, tiles_per_seq, tile_axis)
    return (y * g_ref[...]) * (1.0 + scale) + _mod_rows(sh_ref, tiles_per_seq, tile_axis)


def _normed(x_ref, g_ref, sc_ref, sh_ref, tiles_per_seq, tile_axis=0):
    return _normed_f32(x_ref, g_ref, sc_ref, sh_ref, tiles_per_seq, tile_axis).astype(BF16)


def _norm_mm_kernel(x_ref, g_ref, sc_ref, sh_ref, w_ref, o_ref, h_ref, *, tiles_per_seq):
    @pl.when(pl.program_id(1) == 0)
    def _():
        h_ref[...] = _normed(x_ref, g_ref, sc_ref, sh_ref, tiles_per_seq)
    o_ref[...] = jnp.dot(h_ref[...], w_ref[...], preferred_element_type=F32).astype(o_ref.dtype)


def _norm_glu_kernel(x_ref, g_ref, sc_ref, sh_ref, wa_ref, wg_ref, ba_ref, bg_ref, o_ref, h_ref, *, tiles_per_seq):
    @pl.when(pl.program_id(1) == 0)
    def _():
        h_ref[...] = _normed(x_ref, g_ref, sc_ref, sh_ref, tiles_per_seq)
    h = h_ref[...]
    a = jnp.dot(h, wa_ref[...], preferred_element_type=F32) + ba_ref[...]
    gate = jnp.dot(h, wg_ref[...], preferred_element_type=F32) + bg_ref[...]
    o_ref[...] = a * jax.nn.sigmoid(gate)


class _LayerWeight:
    def __init__(self, stacked, layer):
        self.arr = stacked
        self.layer = layer

    def spec(self, block, index, **kwargs):
        layer = self.layer
        return pl.BlockSpec((None,) + tuple(block), lambda *i: (layer,) + tuple(index(*i)), **kwargs)


class _Mod:
    def __init__(self, mod_all, layer, per_token, rows_per_seq):
        self.layer = layer
        self.per_token = per_token
        self.rows_per_seq = rows_per_seq
        self.arr = mod_all
        self.first_seq_block = (mod_all.shape[1] - MOD_ROW_BLOCK) // MOD_ROW_BLOCK

    def spec(self, chunk, tm):
        layer = self.layer
        if self.per_token:
            return pl.BlockSpec((None, tm, D_MODEL), lambda *i: (layer, i[0], chunk))
        block = self.first_seq_block
        return pl.BlockSpec((None, MOD_ROW_BLOCK, D_MODEL), lambda *i: (layer, block, chunk))

    def tiles_per_seq(self, tm):
        return None if self.per_token else self.rows_per_seq // tm


def _call_norm_matmul(kind, x, gain, mod, chunks, weights, biases, *, tm, tn, n_cols, out_dtype, name):
    t = x.shape[0]
    kernel = {"mm": _norm_mm_kernel, "glu": _norm_glu_kernel}[kind]
    kernel = functools.partial(kernel, tiles_per_seq=mod.tiles_per_seq(tm))
    chunk_shift, chunk_scale = chunks
    in_specs = [
        pl.BlockSpec((tm, D_MODEL), lambda i, j: (i, 0)),
        pl.BlockSpec((1, D_MODEL), lambda i, j: (0, 0)),
        mod.spec(chunk_scale, tm),
        mod.spec(chunk_shift, tm),
    ]
    args = [x, gain.reshape(1, D_MODEL), mod.arr, mod.arr]
    resident = dict(pipeline_mode=pl.Buffered(1)) if n_cols == tn else {}
    for w, col0 in weights:
        in_specs.append(w.spec((D_MODEL, tn), lambda i, j, col0=col0: (0, col0 + j), **resident))
        args.append(w.arr)
    for b, col0 in biases:
        in_specs.append(pl.BlockSpec((1, tn), lambda i, j, col0=col0: (0, col0 + j)))
        args.append(b)
    return pl.pallas_call(
        kernel,
        grid=(t // tm, n_cols // tn),
        in_specs=in_specs,
        out_specs=pl.BlockSpec((tm, tn), lambda i, j: (i, j)),
        out_shape=jax.ShapeDtypeStruct((t, n_cols), out_dtype),
        scratch_shapes=[pltpu.VMEM((tm, D_MODEL), BF16)],
        compiler_params=_params(2),
        name=name,
    )(*args)


def _mm_kernel(a_ref, w_ref, o_ref):
    o_ref[...] = jnp.dot(a_ref[...], w_ref[...], preferred_element_type=F32)


def _mm_per_seq_kernel(*refs):
    *a_refs, w_ref, o_ref = refs
    for b, a_ref in enumerate(a_refs):
        o_ref[b] = jnp.dot(a_ref[...], w_ref[...], preferred_element_type=F32)


KV_TAIL_ONE_STEP_ROWS = 512


def _kv_tail(h, w_qkv, g, batch, seq, keep):
    if keep <= KV_TAIL_ONE_STEP_ROWS:
        return pl.pallas_call(
            _mm_per_seq_kernel,
            grid=(2,),
            in_specs=[pl.BlockSpec((keep, D_MODEL), lambda j, b=b: ((b + 1) * (seq // keep) - 1, 0))
                      for b in range(batch)]
            + [w_qkv.spec((D_MODEL, D_MODEL), lambda j: (0, 3 * g + 1 + j))],
            out_specs=pl.BlockSpec((batch, None, keep, D_MODEL), lambda j: (0, j, 0, 0)),
            out_shape=jax.ShapeDtypeStruct((batch, 2, keep, D_MODEL), F32),
            compiler_params=_params(1),
            name=f"kv_tail_g{g}",
        )(*([h] * batch), w_qkv.arr)
    tm = min(keep, 1024)
    per_seq = keep // tm
    first = (seq - keep) // tm
    return pl.pallas_call(
        _mm_kernel,
        grid=(2, batch * per_seq),
        in_specs=[
            pl.BlockSpec((tm, D_MODEL), lambda j, i: ((i // per_seq) * (seq // tm) + first + i % per_seq, 0)),
            w_qkv.spec((D_MODEL, D_MODEL), lambda j, i: (0, 3 * g + 1 + j)),
        ],
        out_specs=pl.BlockSpec((None, None, tm, D_MODEL), lambda j, i: (i // per_seq, j, i % per_seq, 0)),
        out_shape=jax.ShapeDtypeStruct((batch, 2, keep, D_MODEL), F32),
        compiler_params=_params(2),
        name=f"kv_tail_g{g}",
    )(h, w_qkv.arr)


LANES = 128
N_LANE_SLABS = D_MODEL // LANES


def _qkv_prompt_kernel(x_ref, g_ref, sc_ref, sh_ref, w_ref, o0_ref, o1_ref, o2_ref, hn_ref, h_ref, slab_ref, *,
                       tiles_per_seq):
    tm = x_ref.shape[0]
    group = pl.program_id(0)
    o_refs = (o0_ref, o1_ref, o2_ref)
    quarter, sixteenth = tm // 4, tm // 16

    def project(gi):
        acc = jnp.dot(h_ref[...], w_ref[...], preferred_element_type=F32).astype(BF16)
        dil = DIL_GROUPS[gi][1]
        rows = tm // dil
        for r in range(dil):
            o_refs[gi][r] = acc[r * rows:(r + 1) * rows, :]

    @pl.when(group == 0)
    def _():
        h = _normed(x_ref, g_ref, sc_ref, sh_ref, tiles_per_seq, tile_axis=1)
        hn_ref[...] = h
        h_ref[...] = h
        project(0)

    @pl.when(group == 1)
    def _():
        h = _normed_f32(x_ref, g_ref, sc_ref, sh_ref, tiles_per_seq, tile_axis=1)
        for c in range(N_LANE_SLABS):
            lanes = slice(c * LANES, (c + 1) * LANES)
            slab_ref[0, c] = h[:, lanes]
            for b in range(4):
                part = slab_ref[0, c, pl.ds(b, quarter, stride=4), :]
                h_ref[b * quarter:(b + 1) * quarter, lanes] = part.astype(BF16)
        project(1)

    @pl.when(group == 2)
    def _():
        h = _normed_f32(x_ref, g_ref, sc_ref, sh_ref, tiles_per_seq, tile_axis=1)
        for c in range(N_LANE_SLABS):
            lanes = slice(c * LANES, (c + 1) * LANES)
            slab_ref[0, c] = h[:, lanes]
            for b in range(4):
                slab_ref[1, c, b * quarter:(b + 1) * quarter, :] = slab_ref[0, c, pl.ds(b, quarter, stride=4), :]
            for b in range(4):
                for a in range(4):
                    r = 4 * a + b
                    part = slab_ref[1, c, pl.ds(b * quarter + a, sixteenth, stride=4), :]
                    h_ref[r * sixteenth:(r + 1) * sixteenth, lanes] = part.astype(BF16)
        project(2)


def _qkv_prompt(x, gain, mod, w, batch, seq, tm):
    assert [dil for _, dil in DIL_GROUPS] == [1, 4, 16]
    tiles_per_seq = seq // tm
    n_tiles = batch * tiles_per_seq

    def own_pass(g):
        return lambda n, i: jnp.where(n == g, i, jnp.where(n < g, 0, n_tiles - 1))

    out_specs, out_shapes = [], []
    for g, (_, dil) in enumerate(DIL_GROUPS):
        tile = own_pass(g)
        out_specs.append(pl.BlockSpec(
            (None, dil, tm // dil, 3 * D_MODEL),
            lambda n, i, tile=tile: (tile(n, i) // tiles_per_seq, 0, tile(n, i) % tiles_per_seq, 0)))
        out_shapes.append(jax.ShapeDtypeStruct((batch, dil, seq // dil, 3 * D_MODEL), BF16))
    out_specs.append(pl.BlockSpec((tm, D_MODEL), lambda n, i: (own_pass(0)(n, i), 0)))
    out_shapes.append(jax.ShapeDtypeStruct((batch * seq, D_MODEL), BF16))
    return pl.pallas_call(
        functools.partial(_qkv_prompt_kernel, tiles_per_seq=tiles_per_seq),
        grid=(N_GROUPS, n_tiles),
        in_specs=[
            pl.BlockSpec((tm, D_MODEL), lambda n, i: (i, 0)),
            pl.BlockSpec((1, D_MODEL), lambda n, i: (0, 0)),
            mod.spec(1, tm),
            mod.spec(0, tm),
            w.spec((D_MODEL, 3 * D_MODEL), lambda n, i: (0, n)),
        ],
        out_specs=out_specs,
        out_shape=out_shapes,
        scratch_shapes=[pltpu.VMEM((tm, D_MODEL), BF16),
                        pltpu.VMEM((2, N_LANE_SLABS, tm, LANES), F32)],
        compiler_params=_params(2),
        name="qkv_prompt",
    )(x, gain.reshape(1, D_MODEL), mod.arr, mod.arr, w.arr)


def _mixer_ffn_kernel(*refs, has_bias, final_norm, tiles_per_seq):
    it = iter(refs)
    a_ref, wo_ref = next(it), next(it)
    bo_ref = next(it) if has_bias else None
    x_ref, g1_ref, gffn_ref, sc2_ref, sh2_ref, g2_ref = (next(it) for _ in range(6))
    wg_ref, wu_ref, wd_ref = next(it), next(it), next(it)
    gf_ref = next(it) if final_norm else None
    o_ref = next(it)
    rows = lambda ref: _mod_rows(ref, tiles_per_seq)

    out = jnp.dot(a_ref[...], wo_ref[...], preferred_element_type=F32)
    if has_bias:
        out = out + bo_ref[...]
    x = x_ref[...] + rows(g1_ref) * out
    y = x * lax.rsqrt(jnp.mean(x * x, axis=-1, keepdims=True) + EPS)
    h = ((y * gffn_ref[...]) * (1.0 + rows(sc2_ref)) + rows(sh2_ref)).astype(BF16)
    gate = jnp.dot(h, wg_ref[...], preferred_element_type=F32)
    up = jnp.dot(h, wu_ref[...], preferred_element_type=F32)
    u = ((gate * jax.nn.sigmoid(gate)) * up).astype(BF16)
    x = x + rows(g2_ref) * jnp.dot(u, wd_ref[...], preferred_element_type=F32)
    if final_norm:
        x = (x * lax.rsqrt(jnp.mean(x * x, axis=-1, keepdims=True) + EPS)) * gf_ref[...]
    o_ref[...] = x


def _mixer_ffn(a, w_out, b_out, x, mod, g_ffn, w_gate, w_up, w_down, *, tm, name, g_final=None):
    t, k = a.shape
    tile = pl.BlockSpec((tm, D_MODEL), lambda i: (i, 0))
    vec = pl.BlockSpec((1, D_MODEL), lambda i: (0, 0))
    once = dict(pipeline_mode=pl.Buffered(1))
    in_specs = [pl.BlockSpec((tm, k), lambda i: (i, 0)), w_out.spec((k, D_MODEL), lambda i: (0, 0), **once)]
    args = [a, w_out.arr]
    if b_out is not None:
        in_specs.append(vec)
        args.append(b_out.reshape(1, D_MODEL))
    in_specs += [tile, mod.spec(2, tm), vec, mod.spec(4, tm), mod.spec(3, tm), mod.spec(5, tm),
                 w_gate.spec((D_MODEL, FFN_HIDDEN), lambda i: (0, 0), **once),
                 w_up.spec((D_MODEL, FFN_HIDDEN), lambda i: (0, 0), **once),
                 w_down.spec((FFN_HIDDEN, D_MODEL), lambda i: (0, 0), **once)]
    args += [x, mod.arr, g_ffn.reshape(1, D_MODEL), mod.arr, mod.arr, mod.arr, w_gate.arr, w_up.arr, w_down.arr]
    if g_final is not None:
        in_specs.append(vec)
        args.append(g_final.reshape(1, D_MODEL))
    return pl.pallas_call(
        functools.partial(_mixer_ffn_kernel, has_bias=b_out is not None, final_norm=g_final is not None,
                          tiles_per_seq=mod.tiles_per_seq(tm)),
        grid=(t // tm,),
        in_specs=in_specs,
        out_specs=tile,
        out_shape=jax.ShapeDtypeStruct((t, D_MODEL), F32),
        compiler_params=_params(1),
        name=name,
    )(*args)


ATTN_TILES = 1
ATTN_ROWS = 1024


def _attn_prompt_kernel(q_ref, k_ref, v_ref, bias_ref, o_ref, lse_ref, kcat_ref, vcat_ref):
    step = pl.program_id(2)
    n_res, nq, _ = q_ref.shape
    lane = lax.broadcasted_iota(jnp.int32, (SPAN, HEAD_DIM), 1)
    dims = (((1,), (1,)), ((), ()))
    work = [(j, h) for j in range(ATTN_TILES) for h in range(HEADS)]
    cols = lambda h: slice(h * HEAD_DIM, (h + 1) * HEAD_DIM)

    for res in range(n_res):
        @pl.when(step == 0)
        def _():
            kcat_ref[:SPAN] = jnp.zeros((SPAN, D_MODEL), BF16)
            vcat_ref[:SPAN] = jnp.zeros((SPAN, D_MODEL), BF16)

        @pl.when(step > 0)
        def _():
            kcat_ref[:SPAN] = kcat_ref[nq:]
            vcat_ref[:SPAN] = vcat_ref[nq:]

        kcat_ref[SPAN:] = k_ref[res]
        vcat_ref[SPAN:] = v_ref[res]

        def tiles(it, carry, res=res):
            rows = lambda j: pl.ds(pl.multiple_of((it * ATTN_TILES + j) * SPAN, SPAN), SPAN)
            keys = lambda j: pl.ds(pl.multiple_of((it * ATTN_TILES + j) * SPAN, SPAN), 2 * SPAN)
            starts_seq = jnp.where((step == 0) & (it == 0), 1, 0)
            s, m, p, l, o = {}, {}, {}, {}, {}
            for j, h in work:
                bias = bias_ref[starts_seq if j == 0 else 0, h]
                qk = lax.dot_general(q_ref[res, rows(j), cols(h)], kcat_ref[keys(j), cols(h)], dims,
                                     preferred_element_type=F32)
                s[j, h] = qk * (SCALE * LOG2E) + bias
            for j, h in work:
                m[j, h] = jnp.max(jnp.maximum(s[j, h][:, :SPAN], s[j, h][:, SPAN:]), axis=-1, keepdims=True)
            for j, h in work:
                e = jnp.exp2(s[j, h] - m[j, h])
                l[j, h] = jnp.sum(e[:, :SPAN] + e[:, SPAN:], axis=-1, keepdims=True)
                p[j, h] = e.astype(BF16)
            for j, h in work:
                o[j, h] = jnp.dot(p[j, h], vcat_ref[keys(j), cols(h)], preferred_element_type=F32)
            for j in range(ATTN_TILES):
                lse_all = jnp.zeros((SPAN, HEAD_DIM), F32)
                for h in range(HEADS):
                    o_ref[res, rows(j), cols(h)] = o[j, h] / l[j, h]
                    lse_all = jnp.where(lane == h, (m[j, h] + jnp.log2(l[j, h])) * LN2, lse_all)
                lse_ref[res, rows(j), :] = lse_all
            return carry

        lax.fori_loop(0, nq // (ATTN_TILES * SPAN), tiles, 0)


def _attn_prompt(qkv_g, bias_p, g):
    batch, dil, sub, _ = qkv_g.shape
    nq = min(sub, ATTN_ROWS)
    n_res = min(dil, ATTN_ROWS // nq)
    assert n_res == 1 or nq == sub
    blk = (None, n_res, nq, D_MODEL)
    return pl.pallas_call(
        _attn_prompt_kernel,
        grid=(batch, dil // n_res, sub // nq),
        in_specs=[
            pl.BlockSpec(blk, lambda b, r, t: (b, r, t, 0)),
            pl.BlockSpec(blk, lambda b, r, t: (b, r, t, 1)),
            pl.BlockSpec(blk, lambda b, r, t: (b, r, t, 2)),
            pl.BlockSpec((None, 2, HEADS, SPAN, 2 * SPAN), lambda b, r, t: (g, 0, 0, 0, 0)),
        ],
        out_specs=[
            pl.BlockSpec(blk, lambda b, r, t: (b, r, t, 0)),
            pl.BlockSpec((None, n_res, nq, HEAD_DIM), lambda b, r, t: (b, r, t, 0)),
        ],
        out_shape=[
            jax.ShapeDtypeStruct((batch, dil, sub, D_MODEL), F32),
            jax.ShapeDtypeStruct((batch, dil, sub, HEAD_DIM), F32),
        ],
        scratch_shapes=[pltpu.VMEM((SPAN + nq, D_MODEL), BF16), pltpu.VMEM((SPAN + nq, D_MODEL), BF16)],
        compiler_params=_params(3),
        name=f"attn_prompt_g{g}",
    )(qkv_g, qkv_g, qkv_g, bias_p)


def _attn_sample_kernel(qkv_ref, k0_ref, v0_ref, k1_ref, v1_ref, k2_ref, v2_ref, bias_ref, a_ref, *, bb):
    kv_refs = ((k0_ref, v0_ref), (k1_ref, v1_ref), (k2_ref, v2_ref))
    for bi in range(bb):
        outs, lses = [], []
        for g, (k_ref, v_ref) in enumerate(kv_refs):
            q, k_new, v_new = qkv_ref[bi, 3 * g], qkv_ref[bi, 3 * g + 1], qkv_ref[bi, 3 * g + 2]
            s = jnp.sum(k_ref[bi] * q[None], axis=-1, keepdims=True) * SCALE + bias_ref[g, :SPAN]
            s_new = jnp.sum(k_new * q, axis=-1, keepdims=True) * SCALE + bias_ref[g, SPAN]
            m = jnp.maximum(jnp.max(s, axis=0), s_new)
            e = jnp.exp(s - m[None])
            e_new = jnp.exp(s_new - m)
            l = jnp.sum(e, axis=0) + e_new
            outs.append((jnp.sum(e * v_ref[bi], axis=0) + e_new * v_new) / l)
            lses.append(m + jnp.log(l))
        top = jnp.maximum(jnp.maximum(lses[0], lses[1]), lses[2])
        ws = [jnp.exp(lse - top) for lse in lses]
        den = ws[0] + ws[1] + ws[2]
        a_ref[bi] = (outs[0] * (ws[0] / den) + outs[1] * (ws[1] / den)) + outs[2] * (ws[2] / den)


def _attn_sample(qkv, caches, layer, bias_s):
    b = qkv.shape[0]
    bb = 4
    in_specs = [pl.BlockSpec((bb,) + qkv.shape[1:], lambda i: (i, 0, 0, 0))]
    args = [qkv]
    for g, (_, dil) in enumerate(DIL_GROUPS):
        cache = caches[g]
        assert cache.shape[3] == SPAN * dil
        view = cache.reshape(cache.shape[:3] + (SPAN, dil, HEADS, HEAD_DIM))
        for kv in range(2):
            in_specs.append(pl.BlockSpec((None, bb, None, SPAN, None, HEADS, HEAD_DIM),
                                         lambda i, kv=kv: (layer, i, kv, 0, 0, 0, 0)))
            args.append(view)
    in_specs.append(pl.BlockSpec(bias_s.shape, lambda i: (0, 0, 0, 0)))
    args.append(bias_s)
    return pl.pallas_call(
        functools.partial(_attn_sample_kernel, bb=bb),
        grid=(b // bb,),
        in_specs=in_specs,
        out_specs=pl.BlockSpec((bb, HEADS, HEAD_DIM), lambda i: (i, 0, 0)),
        out_shape=jax.ShapeDtypeStruct((b, HEADS, HEAD_DIM), F32),
        compiler_params=_params(1),
        name="attn_sample",
    )(*args)


def _combine_kernel(o0_ref, o1_ref, o2_ref, l0_ref, l1_ref, l2_ref, a_ref, os_ref, ls_ref):
    tm = a_ref.shape[0]
    for gi, o_ref, l_ref in ((1, o1_ref, l1_ref), (2, o2_ref, l2_ref)):
        dil = DIL_GROUPS[gi][1]
        rows = tm // dil
        for r in range(dil):
            ls_ref[gi - 1, pl.ds(r, rows, stride=dil), :] = l_ref[r]
            for h in range(HEADS):
                os_ref[gi - 1, h, pl.ds(r, rows, stride=dil), :] = o_ref[r, :, h * HEAD_DIM:(h + 1) * HEAD_DIM]
    l0, l1, l2 = l0_ref[0], ls_ref[0], ls_ref[1]
    m = jnp.maximum(jnp.maximum(l0, l1), l2)
    e0, e1, e2 = jnp.exp(l0 - m), jnp.exp(l1 - m), jnp.exp(l2 - m)
    den = e0 + e1 + e2
    w1, w2 = e1 / den, e2 / den
    for h in range(HEADS):
        sl = slice(h * HEAD_DIM, (h + 1) * HEAD_DIM)
        o0 = o0_ref[0, :, sl]
        o = o0 + (os_ref[0, h] - o0) * w1[:, h:h + 1] + (os_ref[1, h] - o0) * w2[:, h:h + 1]
        a_ref[:, sl] = o.astype(a_ref.dtype)


def _combine(outs, lses, seq, tm):
    batch = outs[0].shape[0]
    tiles_per_seq = seq // tm
    in_specs = []
    for width in (D_MODEL, HEAD_DIM):
        for _, dil in DIL_GROUPS:
            in_specs.append(pl.BlockSpec((None, dil, tm // dil, width),
                                         lambda i: (i // tiles_per_seq, 0, i % tiles_per_seq, 0)))
    return pl.pallas_call(
        _combine_kernel,
        grid=(batch * tiles_per_seq,),
        in_specs=in_specs,
        out_specs=pl.BlockSpec((tm, D_MODEL), lambda i: (i, 0)),
        out_shape=jax.ShapeDtypeStruct((batch * seq, D_MODEL), BF16),
        scratch_shapes=[pltpu.VMEM((N_GROUPS - 1, HEADS, tm, HEAD_DIM), F32),
                        pltpu.VMEM((N_GROUPS - 1, tm, HEAD_DIM), F32)],
        compiler_params=_params(1),
        name="combine_groups",
    )(*outs, *lses)


def _ln_silu(z, g_ref, b_ref):
    mu = jnp.mean(z, axis=-1, keepdims=True)
    zc = z - mu
    var = jnp.mean(zc * zc, axis=-1, keepdims=True)
    y = (zc * lax.rsqrt(var + EPS)) * g_ref[...] + b_ref[...]
    return y * jax.nn.sigmoid(y)


SUBLANES = 8
HALO = 32
CONV_ROWS = 128
CONV_CHAINS = 2


def _conv_prompt_kernel(u_ref, prev_ref, w_ref, bdw_ref, g_ref, b_ref, a_ref, ext_ref, z_ref, *, tiles_per_seq):
    tm = u_ref.shape[0]
    starts_seq = pl.program_id(0) % tiles_per_seq == 0
    n_shifted = tm + HALO - SUBLANES
    for c in range(N_LANE_SLABS):
        lanes = slice(c * LANES, (c + 1) * LANES)
        ext_ref[0, c, :HALO, :] = jnp.where(starts_seq, 0.0, prev_ref[:, lanes])
        ext_ref[0, c, HALO:, :] = u_ref[:, lanes]
        for s in range(1, SUBLANES):
            ext_ref[s, c, :n_shifted, :] = ext_ref[0, c, s:s + n_shifted, :]
    off = HALO - (CONV_WIDTH - 1)

    def strip(idx, carry):
        c = idx % N_LANE_SLABS
        r0 = pl.multiple_of((idx // N_LANE_SLABS) * CONV_ROWS, CONV_ROWS)
        accs = [jnp.zeros((CONV_ROWS, LANES), F32) for _ in range(CONV_CHAINS)]
        for shift in range(SUBLANES):
            taps = [k for k in range(CONV_WIDTH) if (off + k) % SUBLANES == shift]
            steps = [(off + k - shift) // SUBLANES for k in taps]
            n_window = SUBLANES * (steps[-1] - steps[0]) + CONV_ROWS
            window = ext_ref[shift, c, pl.ds(r0 + SUBLANES * steps[0], n_window), :]
            for k, q in zip(taps, steps):
                first = SUBLANES * (q - steps[0])
                accs[k % CONV_CHAINS] = accs[k % CONV_CHAINS] + window[first:first + CONV_ROWS] * w_ref[c, k:k + 1, :]
        z_ref[c, pl.ds(r0, CONV_ROWS), :] = sum(accs[1:], accs[0]) + bdw_ref[c]
        return carry

    lax.fori_loop(0, (tm // CONV_ROWS) * N_LANE_SLABS, strip, 0)
    z = jnp.concatenate([z_ref[c] for c in range(N_LANE_SLABS)], axis=-1)
    a_ref[...] = _ln_silu(z, g_ref, b_ref).astype(a_ref.dtype)


def _conv_prompt(u, w_dw, b_dw, ln_g, ln_b, seq, tm):
    t = u.shape[0]
    row = lambda v: v.reshape(1, D_MODEL)
    vec = pl.BlockSpec((1, D_MODEL), lambda i: (0, 0))
    slabs = lambda v: jnp.transpose(v.reshape(-1, N_LANE_SLABS, LANES), (1, 0, 2))
    return pl.pallas_call(
        functools.partial(_conv_prompt_kernel, tiles_per_seq=seq // tm),
        grid=(t // tm,),
        in_specs=[
            pl.BlockSpec((tm, D_MODEL), lambda i: (i, 0)),
            pl.BlockSpec((HALO, D_MODEL), lambda i: (jnp.maximum(i * (tm // HALO) - 1, 0), 0)),
            pl.BlockSpec((N_LANE_SLABS, CONV_WIDTH, LANES), lambda i: (0, 0, 0)),
            pl.BlockSpec((N_LANE_SLABS, 1, LANES), lambda i: (0, 0, 0)),
            vec, vec,
        ],
        out_specs=pl.BlockSpec((tm, D_MODEL), lambda i: (i, 0)),
        out_shape=jax.ShapeDtypeStruct((t, D_MODEL), BF16),
        scratch_shapes=[pltpu.VMEM((SUBLANES, N_LANE_SLABS, tm + HALO, LANES), F32),
                        pltpu.VMEM((N_LANE_SLABS, tm, LANES), F32)],
        compiler_params=_params(1),
        name="conv_prompt",
    )(u, u, slabs(w_dw), slabs(b_dw), row(ln_g), row(ln_b))


def _conv_sample_kernel(u_ref, st_ref, w_ref, bdw_ref, g_ref, b_ref, a_ref, ns_ref):
    n_state = st_ref.shape[0]
    u = u_ref[...]
    z = u * w_ref[n_state:n_state + 1, :] + bdw_ref[...]
    for k in range(n_state):
        z = z + st_ref[k] * w_ref[k:k + 1, :]
    a_ref[...] = _ln_silu(z, g_ref, b_ref).astype(a_ref.dtype)
    for k in range(n_state - 1):
        ns_ref[k] = st_ref[k + 1]
    ns_ref[n_state - 1] = u


def _conv_sample(u, state, layer, w_dw, b_dw, ln_g, ln_b):
    _, b, n_state, _ = state.shape
    bb = 16
    row = lambda v: v.reshape(1, D_MODEL)
    vec = pl.BlockSpec((1, D_MODEL), lambda i: (0, 0))
    a, new_state = pl.pallas_call(
        _conv_sample_kernel,
        grid=(b // bb,),
        in_specs=[
            pl.BlockSpec((bb, D_MODEL), lambda i: (i, 0)),
            pl.BlockSpec((None, n_state, bb, D_MODEL), lambda i: (layer, 0, i, 0)),
            pl.BlockSpec((CONV_WIDTH, D_MODEL), lambda i: (0, 0)),
            vec, vec, vec,
        ],
        out_specs=[
            pl.BlockSpec((bb, D_MODEL), lambda i: (i, 0)),
            pl.BlockSpec((None, n_state, bb, D_MODEL), lambda i: (0, 0, i, 0)),
        ],
        out_shape=[
            jax.ShapeDtypeStruct((b, D_MODEL), BF16),
            jax.ShapeDtypeStruct((1, n_state, b, D_MODEL), F32),
        ],
        compiler_params=_params(1),
        name="conv_sample",
    )(u, jnp.transpose(state, (0, 2, 1, 3)), w_dw, row(b_dw), row(ln_g), row(ln_b))
    return a, jnp.transpose(new_state, (0, 2, 1, 3))


PROMPT_FFN_ROWS = 512


def kernel(x_prompt, x_sample, cache_kv_w128, cache_kv_w512, cache_kv_w2048, state_conv, c_prompt, c_sample,
           w_mod, b_mod, g_mix, g_ffn, g_final, w_qkv, w_o, rel_bias, w_pw1, b_pw1, w_dw, b_dw, ln_g, ln_b,
           w_pw2, b_pw2, w_gate, w_up, w_down):
    batch, seq, d = x_prompt.shape
    dec_batch = x_sample.shape[0]
    assert d == D_MODEL and x_sample.shape[1] == 1
    caches = (cache_kv_w128, cache_kv_w512, cache_kv_w2048)

    def layers(w):
        w = w.astype(BF16)
        return [_LayerWeight(w, layer) for layer in range(w.shape[0])]

    w_qkv, w_o, w_pw1, w_pw2, w_gate, w_up, w_down = map(layers, (w_qkv, w_o, w_pw1, w_pw2, w_gate, w_up, w_down))

    c_all = jnp.concatenate([c_sample, c_prompt, jnp.zeros((8 - batch, d), F32)], axis=0)
    mod_all = _modulation(c_all, w_mod, b_mod)
    bias_p, bias_s = _bias_tables(rel_bias)

    xp = x_prompt.reshape(batch * seq, d)
    xs = x_sample.reshape(dec_batch, d)
    n_qkv = 3 * N_GROUPS * d

    mod_p = _Mod(mod_all, 0, False, seq)
    mod_s = _Mod(mod_all, 0, True, 1)

    qkv_p = _qkv_prompt(xp, g_mix[0], mod_p, w_qkv[0], batch, seq, 512)
    qkv_s = _call_norm_matmul("mm", xs, g_mix[0], mod_s, (0, 1), [(w_qkv[0], 0)], [],
                              tm=dec_batch, tn=1536, n_cols=n_qkv, out_dtype=F32, name="qkv_sample")

    kv_prompt = [_kv_tail(qkv_p[N_GROUPS], w_qkv[0], g, batch, seq, min(window, seq))
                 .reshape(1, batch, 2, min(window, seq), HEADS, HEAD_DIM) for g, (window, _) in enumerate(DIL_GROUPS)]

    outs, lses = zip(*[_attn_prompt(qkv_p[g], bias_p, g) for g in range(N_GROUPS)])
    a_p = _combine(outs, lses, seq, 1024)
    xp = _mixer_ffn(a_p, w_o[0], None, xp, mod_p, g_ffn[0], w_gate[0], w_up[0], w_down[0],
                    tm=PROMPT_FFN_ROWS, name="attn_out_ffn_prompt")

    qkv_s4 = qkv_s.reshape(dec_batch, 3 * N_GROUPS, HEADS, HEAD_DIM)
    bias_s4 = jnp.stack([bias_s[g, :, g * HEADS:(g + 1) * HEADS] for g in range(N_GROUPS)])[..., None]
    a_s = _attn_sample(qkv_s4, caches, 0, bias_s4).reshape(dec_batch, d).astype(BF16)
    xs = _mixer_ffn(a_s, w_o[0], None, xs, mod_s, g_ffn[0], w_gate[0], w_up[0], w_down[0],
                    tm=dec_batch, name="attn_out_ffn_sample")

    mod_p = _Mod(mod_all, 1, False, seq)
    mod_s = _Mod(mod_all, 1, True, 1)
    b_pw1_row = b_pw1[0].reshape(1, 2 * d)

    def glu(x, mod, tm, tag):
        return _call_norm_matmul("glu", x, g_mix[1], mod, (0, 1), [(w_pw1[0], 0), (w_pw1[0], 1)],
                                 [(b_pw1_row, 0), (b_pw1_row, 1)],
                                 tm=tm, tn=d, n_cols=d, out_dtype=F32, name=f"glu_{tag}")

    u_p = glu(xp, mod_p, 1024, "prompt")
    a_p = _conv_prompt(u_p, w_dw[0], b_dw[0], ln_g[0], ln_b[0], seq, 512)
    y_p = _mixer_ffn(a_p, w_pw2[0], b_pw2[0], xp, mod_p, g_ffn[1], w_gate[1], w_up[1], w_down[1],
                     tm=PROMPT_FFN_ROWS, name="conv_out_ffn_prompt", g_final=g_final)

    u_s = glu(xs, mod_s, dec_batch, "sample")
    a_s, conv_s = _conv_sample(u_s, state_conv, 0, w_dw[0], b_dw[0], ln_g[0], ln_b[0])
    y_s = _mixer_ffn(a_s, w_pw2[0], b_pw2[0], xs, mod_s, g_ffn[1], w_gate[1], w_up[1], w_down[1],
                     tm=dec_batch, name="conv_out_ffn_sample", g_final=g_final)

    conv_p = u_p.reshape(batch, seq, d)[:, seq - (CONV_WIDTH - 1):][None]
    kv_sample = [qkv_s4[:, 3 * g + 1:3 * g + 3].reshape(1, dec_batch, 2, 1, HEADS, HEAD_DIM)
                 for g in range(N_GROUPS)]
    return (y_p.reshape(batch, seq, d), y_s.reshape(dec_batch, 1, d),
            kv_prompt[0], kv_prompt[1], kv_prompt[2], conv_p,
            kv_sample[0], kv_sample[1], kv_sample[2], conv_s)
```

```python
import functools
import math

import numpy as np
import jax
import jax.numpy as jnp
from jax import lax
from jax.experimental import pallas as pl
from jax.experimental.pallas import tpu as pltpu

D_MODEL = 1024
DIL_GROUPS = ((128, 1), (512, 4), (2048, 16))
N_GROUPS = len(DIL_GROUPS)
HEADS = 8
HEAD_DIM = 128
SPAN = 128
N_BUCKETS = 32
MAX_DISTANCE = 2048
CONV_WIDTH = 31
FFN_HIDDEN = 2816
EPS = 1e-6
NEG_INF = -1e30
SCALE = HEAD_DIM ** -0.5
LOG2E = math.log2(math.e)
LN2 = math.log(2.0)

F32 = jnp.float32
BF16 = jnp.bfloat16

VMEM_LIMIT_BYTES = 56 * 1024 * 1024


def _params(n_axes, vmem=VMEM_LIMIT_BYTES):
    return pltpu.CompilerParams(dimension_semantics=("arbitrary",) * n_axes, vmem_limit_bytes=vmem)


def _t5_bucket_np(dist):
    max_exact = N_BUCKETS // 2
    n = np.maximum(dist, 1).astype(np.float32)
    large = max_exact + (np.log(n / np.float32(max_exact)) / np.float32(math.log(MAX_DISTANCE / max_exact))
                         * np.float32(N_BUCKETS - max_exact)).astype(np.int32)
    large = np.minimum(large, N_BUCKETS - 1)
    return np.where(dist < max_exact, dist, large).astype(np.int32)


def _bucket_tables():
    qi = np.arange(SPAN)[:, None]
    ki = np.arange(2 * SPAN)[None, :]
    delta = qi + SPAN - ki
    in_band = (delta >= 0) & (delta <= SPAN)
    prompt, sample = [], []
    for _, dilation in DIL_GROUPS:
        b = _t5_bucket_np(np.clip(delta, 0, SPAN) * dilation)
        prompt.append(np.where(in_band, b, -1))
        j = SPAN - np.arange(136)
        sample.append(_t5_bucket_np(np.maximum(j, 0) * dilation)[:, None])
    return np.stack(prompt).astype(np.int32), np.stack(sample).astype(np.int32)


def _bias_kernel(rb_smem, rb_ref, bp_ref, bs_ref, op_ref, os_ref):
    g = pl.program_id(0)
    bp = bp_ref[...]
    in_prev_block = lax.broadcasted_iota(jnp.int32, bp.shape, 1) < SPAN
    for h in range(HEADS):
        acc = jnp.full(bp.shape, NEG_INF, F32)
        for b in range(N_BUCKETS):
            acc = jnp.where(bp == b, rb_smem[b, g * HEADS + h] * LOG2E, acc)
        op_ref[0, h] = acc
        op_ref[1, h] = jnp.where(in_prev_block, NEG_INF, acc)
    bs = bs_ref[...]
    acc = jnp.zeros((bs.shape[0], rb_ref.shape[1]), F32)
    for b in range(N_BUCKETS):
        acc = jnp.where(bs == b, rb_ref[b:b + 1, :], acc)
    os_ref[...] = acc


def _bias_tables(rel_bias):
    bp, bs = _bucket_tables()
    n_cols = rel_bias.shape[1]
    return pl.pallas_call(
        _bias_kernel,
        grid=(N_GROUPS,),
        in_specs=[
            pl.BlockSpec(memory_space=pltpu.SMEM),
            pl.BlockSpec((N_BUCKETS, n_cols), lambda g: (0, 0)),
            pl.BlockSpec((None, SPAN, 2 * SPAN), lambda g: (g, 0, 0)),
            pl.BlockSpec((None, 136, 1), lambda g: (g, 0, 0)),
        ],
        out_specs=[
            pl.BlockSpec((None, 2, HEADS, SPAN, 2 * SPAN), lambda g: (g, 0, 0, 0, 0)),
            pl.BlockSpec((None, 136, n_cols), lambda g: (g, 0, 0)),
        ],
        out_shape=[
            jax.ShapeDtypeStruct((N_GROUPS, 2, HEADS, SPAN, 2 * SPAN), F32),
            jax.ShapeDtypeStruct((N_GROUPS, 136, n_cols), F32),
        ],
        compiler_params=_params(1),
        name="bias_tables",
    )(rel_bias, rel_bias, jnp.asarray(bp), jnp.asarray(bs))


def _mod_kernel(c_ref, w_ref, b_ref, o_ref):
    c = c_ref[...]
    a = (c * jax.nn.sigmoid(c)).astype(BF16)
    o_ref[...] = jnp.dot(a, w_ref[...].astype(BF16), preferred_element_type=F32) + b_ref[...]


def _modulation(c_all, w_mod, b_mod):
    depth, d, n = w_mod.shape
    rows = c_all.shape[0]
    tn = 1536
    return pl.pallas_call(
        _mod_kernel,
        grid=(depth, n // tn),
        in_specs=[
            pl.BlockSpec((rows, d), lambda l, j: (0, 0)),
            pl.BlockSpec((None, d, tn), lambda l, j: (l, 0, j)),
            pl.BlockSpec((None, 1, tn), lambda l, j: (l, 0, j)),
        ],
        out_specs=pl.BlockSpec((None, rows, tn), lambda l, j: (l, 0, j)),
        out_shape=jax.ShapeDtypeStruct((depth, rows, n), F32),
        compiler_params=_params(2),
        name="modulation",
    )(c_all, w_mod, b_mod.reshape(depth, 1, n))


MOD_ROW_BLOCK = 8


def _mod_rows(ref, tiles_per_seq, tile_axis=0):
    rows = ref[...]
    if tiles_per_seq is None:
        return rows
    seq_id = pl.program_id(tile_axis) // tiles_per_seq
    pick = lax.broadcasted_iota(jnp.int32, rows.shape, 0) == seq_id
    return jnp.sum(jnp.where(pick, rows, 0.0), axis=0, keepdims=True)


def _normed_f32(x_ref, g_ref, sc_ref, sh_ref, tiles_per_seq, tile_axis=0):
    x = x_ref[...]
    y = x * lax.rsqrt(jnp.mean(x * x, axis=-1, keepdims=True) + EPS)
    scale = _mod_rows(sc_ref, tiles_per_seq, tile_axis)
    return (y * g_ref[...]) * (1.0 + scale) + _mod_rows(sh_ref, tiles_per_seq, tile_axis)


def _normed(x_ref, g_ref, sc_ref, sh_ref, tiles_per_seq, tile_axis=0):
    return _normed_f32(x_ref, g_ref, sc_ref, sh_ref, tiles_per_seq, tile_axis).astype(BF16)


def _norm_glu_kernel(x_ref, g_ref, sc_ref, sh_ref, wa_ref, wg_ref, ba_ref, bg_ref, o_ref, h_ref, *, tiles_per_seq):
    @pl.when(pl.program_id(1) == 0)
    def _():
        h_ref[...] = _normed(x_ref, g_ref, sc_ref, sh_ref, tiles_per_seq)
    h = h_ref[...]
    a = jnp.dot(h, wa_ref[...], preferred_element_type=F32) + ba_ref[...]
    gate = jnp.dot(h, wg_ref[...], preferred_element_type=F32) + bg_ref[...]
    o_ref[...] = a * jax.nn.sigmoid(gate)


class _LayerWeight:
    def __init__(self, stacked, layer):
        self.arr = stacked
        self.layer = layer

    def spec(self, block, index, **kwargs):
        layer = self.layer
        return pl.BlockSpec((None,) + tuple(block), lambda *i: (layer,) + tuple(index(*i)), **kwargs)


class _Mod:
    def __init__(self, mod_all, layer, per_token, rows_per_seq):
        self.layer = layer
        self.per_token = per_token
        self.rows_per_seq = rows_per_seq
        self.arr = mod_all
        self.first_seq_block = (mod_all.shape[1] - MOD_ROW_BLOCK) // MOD_ROW_BLOCK

    def spec(self, chunk, tm, single_tile=False):
        layer = self.layer
        if self.per_token:
            return pl.BlockSpec((None, tm, D_MODEL), lambda *i: (layer, 0 if single_tile else i[0], chunk))
        block = self.first_seq_block
        return pl.BlockSpec((None, MOD_ROW_BLOCK, D_MODEL), lambda *i: (layer, block, chunk))

    def tiles_per_seq(self, tm):
        return None if self.per_token else self.rows_per_seq // tm


def _norm_glu(x, gain, mod, chunks, weights, biases, *, tm, tn, n_cols, out_dtype, name):
    t = x.shape[0]
    kernel = functools.partial(_norm_glu_kernel, tiles_per_seq=mod.tiles_per_seq(tm))
    chunk_shift, chunk_scale = chunks
    in_specs = [
        pl.BlockSpec((tm, D_MODEL), lambda i, j: (i, 0)),
        pl.BlockSpec((1, D_MODEL), lambda i, j: (0, 0)),
        mod.spec(chunk_scale, tm),
        mod.spec(chunk_shift, tm),
    ]
    args = [x, gain.reshape(1, D_MODEL), mod.arr, mod.arr]
    resident = dict(pipeline_mode=pl.Buffered(1)) if n_cols == tn else {}
    for w, col0 in weights:
        in_specs.append(w.spec((D_MODEL, tn), lambda i, j, col0=col0: (0, col0 + j), **resident))
        args.append(w.arr)
    for b, col0 in biases:
        in_specs.append(pl.BlockSpec((1, tn), lambda i, j, col0=col0: (0, col0 + j)))
        args.append(b)
    return pl.pallas_call(
        kernel,
        grid=(t // tm, n_cols // tn),
        in_specs=in_specs,
        out_specs=pl.BlockSpec((tm, tn), lambda i, j: (i, j)),
        out_shape=jax.ShapeDtypeStruct((t, n_cols), out_dtype),
        scratch_shapes=[pltpu.VMEM((tm, D_MODEL), BF16)],
        compiler_params=_params(2),
        name=name,
    )(*args)


def _norm_mm_cast_kernel(x_ref, g_ref, sc_ref, sh_ref, w_ref, o_ref, w16_ref, h_ref):
    @pl.when(pl.program_id(0) == 0)
    def _():
        h_ref[...] = _normed(x_ref, g_ref, sc_ref, sh_ref, None)
    w = w_ref[...].astype(BF16)
    w16_ref[...] = w
    o_ref[...] = jnp.dot(h_ref[...], w, preferred_element_type=F32)


def _qkv_sample(x, gain, mod, w32, layer, tn):
    rows = x.shape[0]
    n = w32.shape[-1]
    return pl.pallas_call(
        _norm_mm_cast_kernel,
        grid=(n // tn,),
        in_specs=[
            pl.BlockSpec((rows, D_MODEL), lambda j: (0, 0)),
            pl.BlockSpec((1, D_MODEL), lambda j: (0, 0)),
            mod.spec(1, rows, single_tile=True),
            mod.spec(0, rows, single_tile=True),
            pl.BlockSpec((None, D_MODEL, tn), lambda j: (layer, 0, j)),
        ],
        out_specs=[
            pl.BlockSpec((rows, tn), lambda j: (0, j)),
            pl.BlockSpec((None, D_MODEL, tn), lambda j: (0, 0, j)),
        ],
        out_shape=[
            jax.ShapeDtypeStruct((rows, n), F32),
            jax.ShapeDtypeStruct((1, D_MODEL, n), BF16),
        ],
        scratch_shapes=[pltpu.VMEM((rows, D_MODEL), BF16)],
        compiler_params=_params(1),
        name="qkv_sample",
    )(x, gain.reshape(1, D_MODEL), mod.arr, mod.arr, w32)


def _mm_kernel(a_ref, w_ref, o_ref):
    o_ref[...] = jnp.dot(a_ref[...], w_ref[...], preferred_element_type=F32)


def _mm_per_seq_kernel(*refs):
    *a_refs, w_ref, o_ref = refs
    for b, a_ref in enumerate(a_refs):
        o_ref[b] = jnp.dot(a_ref[...], w_ref[...], preferred_element_type=F32)


KV_TAIL_ONE_STEP_ROWS = 512


def _kv_tail(h, w_qkv, g, batch, seq, keep):
    if keep <= KV_TAIL_ONE_STEP_ROWS:
        return pl.pallas_call(
            _mm_per_seq_kernel,
            grid=(2,),
            in_specs=[pl.BlockSpec((keep, D_MODEL), lambda j, b=b: ((b + 1) * (seq // keep) - 1, 0))
                      for b in range(batch)]
            + [w_qkv.spec((D_MODEL, D_MODEL), lambda j: (0, 3 * g + 1 + j))],
            out_specs=pl.BlockSpec((batch, None, keep, D_MODEL), lambda j: (0, j, 0, 0)),
            out_shape=jax.ShapeDtypeStruct((batch, 2, keep, D_MODEL), F32),
            compiler_params=_params(1),
            name=f"kv_tail_g{g}",
        )(*([h] * batch), w_qkv.arr)
    tm = min(keep, 1024)
    per_seq = keep // tm
    first = (seq - keep) // tm
    return pl.pallas_call(
        _mm_kernel,
        grid=(2, batch * per_seq),
        in_specs=[
            pl.BlockSpec((tm, D_MODEL), lambda j, i: ((i // per_seq) * (seq // tm) + first + i % per_seq, 0)),
            w_qkv.spec((D_MODEL, D_MODEL), lambda j, i: (0, 3 * g + 1 + j)),
        ],
        out_specs=pl.BlockSpec((None, None, tm, D_MODEL), lambda j, i: (i // per_seq, j, i % per_seq, 0)),
        out_shape=jax.ShapeDtypeStruct((batch, 2, keep, D_MODEL), F32),
        compiler_params=_params(2),
        name=f"kv_tail_g{g}",
    )(h, w_qkv.arr)


LANES = 128
N_LANE_SLABS = D_MODEL // LANES


def _qkv_prompt_kernel(x_ref, g_ref, sc_ref, sh_ref, w_ref, o0_ref, o1_ref, o2_ref, hn_ref, h_ref, slab_ref, *,
                       tiles_per_seq):
    tm = x_ref.shape[0]
    group = pl.program_id(0)
    o_refs = (o0_ref, o1_ref, o2_ref)
    quarter, sixteenth = tm // 4, tm // 16

    def project(gi):
        acc = jnp.dot(h_ref[...], w_ref[...], preferred_element_type=F32).astype(BF16)
        dil = DIL_GROUPS[gi][1]
        rows = tm // dil
        for r in range(dil):
            o_refs[gi][r] = acc[r * rows:(r + 1) * rows, :]

    @pl.when(group == 0)
    def _():
        h = _normed(x_ref, g_ref, sc_ref, sh_ref, tiles_per_seq, tile_axis=1)
        hn_ref[...] = h
        h_ref[...] = h
        project(0)

    @pl.when(group == 1)
    def _():
        h = _normed_f32(x_ref, g_ref, sc_ref, sh_ref, tiles_per_seq, tile_axis=1)
        for c in range(N_LANE_SLABS):
            lanes = slice(c * LANES, (c + 1) * LANES)
            slab_ref[0, c] = h[:, lanes]
            for b in range(4):
                part = slab_ref[0, c, pl.ds(b, quarter, stride=4), :]
                h_ref[b * quarter:(b + 1) * quarter, lanes] = part.astype(BF16)
        project(1)

    @pl.when(group == 2)
    def _():
        h = _normed_f32(x_ref, g_ref, sc_ref, sh_ref, tiles_per_seq, tile_axis=1)
        for c in range(N_LANE_SLABS):
            lanes = slice(c * LANES, (c + 1) * LANES)
            slab_ref[0, c] = h[:, lanes]
            for b in range(4):
                slab_ref[1, c, b * quarter:(b + 1) * quarter, :] = slab_ref[0, c, pl.ds(b, quarter, stride=4), :]
            for b in range(4):
                for a in range(4):
                    r = 4 * a + b
                    part = slab_ref[1, c, pl.ds(b * quarter + a, sixteenth, stride=4), :]
                    h_ref[r * sixteenth:(r + 1) * sixteenth, lanes] = part.astype(BF16)
        project(2)


def _qkv_prompt(x, gain, mod, w, batch, seq, tm):
    assert [dil for _, dil in DIL_GROUPS] == [1, 4, 16]
    tiles_per_seq = seq // tm
    n_tiles = batch * tiles_per_seq

    def own_pass(g):
        return lambda n, i: jnp.where(n == g, i, jnp.where(n < g, 0, n_tiles - 1))

    out_specs, out_shapes = [], []
    for g, (_, dil) in enumerate(DIL_GROUPS):
        tile = own_pass(g)
        out_specs.append(pl.BlockSpec(
            (None, dil, tm // dil, 3 * D_MODEL),
            lambda n, i, tile=tile: (tile(n, i) // tiles_per_seq, 0, tile(n, i) % tiles_per_seq, 0)))
        out_shapes.append(jax.ShapeDtypeStruct((batch, dil, seq // dil, 3 * D_MODEL), BF16))
    out_specs.append(pl.BlockSpec((tm, D_MODEL), lambda n, i: (own_pass(0)(n, i), 0)))
    out_shapes.append(jax.ShapeDtypeStruct((batch * seq, D_MODEL), BF16))
    return pl.pallas_call(
        functools.partial(_qkv_prompt_kernel, tiles_per_seq=tiles_per_seq),
        grid=(N_GROUPS, n_tiles),
        in_specs=[
            pl.BlockSpec((tm, D_MODEL), lambda n, i: (i, 0)),
            pl.BlockSpec((1, D_MODEL), lambda n, i: (0, 0)),
            mod.spec(1, tm),
            mod.spec(0, tm),
            w.spec((D_MODEL, 3 * D_MODEL), lambda n, i: (0, n)),
        ],
        out_specs=out_specs,
        out_shape=out_shapes,
        scratch_shapes=[pltpu.VMEM((tm, D_MODEL), BF16),
                        pltpu.VMEM((2, N_LANE_SLABS, tm, LANES), F32)],
        compiler_params=_params(2),
        name="qkv_prompt",
    )(x, gain.reshape(1, D_MODEL), mod.arr, mod.arr, w.arr)


def _mixer_ffn_kernel(*refs, has_bias, final_norm, tiles_per_seq):
    it = iter(refs)
    a_ref, wo_ref = next(it), next(it)
    bo_ref = next(it) if has_bias else None
    x_ref, g1_ref, gffn_ref, sc2_ref, sh2_ref, g2_ref = (next(it) for _ in range(6))
    wg_ref, wu_ref, wd_ref = next(it), next(it), next(it)
    gf_ref = next(it) if final_norm else None
    o_ref = next(it)
    rows = lambda ref: _mod_rows(ref, tiles_per_seq)

    out = jnp.dot(a_ref[...], wo_ref[...], preferred_element_type=F32)
    if has_bias:
        out = out + bo_ref[...]
    x = x_ref[...] + rows(g1_ref) * out
    y = x * lax.rsqrt(jnp.mean(x * x, axis=-1, keepdims=True) + EPS)
    h = ((y * gffn_ref[...]) * (1.0 + rows(sc2_ref)) + rows(sh2_ref)).astype(BF16)
    gate = jnp.dot(h, wg_ref[...], preferred_element_type=F32)
    up = jnp.dot(h, wu_ref[...], preferred_element_type=F32)
    u = ((gate * jax.nn.sigmoid(gate)) * up).astype(BF16)
    x = x + rows(g2_ref) * jnp.dot(u, wd_ref[...], preferred_element_type=F32)
    if final_norm:
        x = (x * lax.rsqrt(jnp.mean(x * x, axis=-1, keepdims=True) + EPS)) * gf_ref[...]
    o_ref[...] = x


def _mixer_ffn(a, w_out, b_out, x, mod, g_ffn, w_gate, w_up, w_down, *, tm, name, g_final=None):
    t, k = a.shape
    tile = pl.BlockSpec((tm, D_MODEL), lambda i: (i, 0))
    vec = pl.BlockSpec((1, D_MODEL), lambda i: (0, 0))
    once = dict(pipeline_mode=pl.Buffered(1))
    in_specs = [pl.BlockSpec((tm, k), lambda i: (i, 0)), w_out.spec((k, D_MODEL), lambda i: (0, 0), **once)]
    args = [a, w_out.arr]
    if b_out is not None:
        in_specs.append(vec)
        args.append(b_out.reshape(1, D_MODEL))
    in_specs += [tile, mod.spec(2, tm), vec, mod.spec(4, tm), mod.spec(3, tm), mod.spec(5, tm),
                 w_gate.spec((D_MODEL, FFN_HIDDEN), lambda i: (0, 0), **once),
                 w_up.spec((D_MODEL, FFN_HIDDEN), lambda i: (0, 0), **once),
                 w_down.spec((FFN_HIDDEN, D_MODEL), lambda i: (0, 0), **once)]
    args += [x, mod.arr, g_ffn.reshape(1, D_MODEL), mod.arr, mod.arr, mod.arr, w_gate.arr, w_up.arr, w_down.arr]
    if g_final is not None:
        in_specs.append(vec)
        args.append(g_final.reshape(1, D_MODEL))
    return pl.pallas_call(
        functools.partial(_mixer_ffn_kernel, has_bias=b_out is not None, final_norm=g_final is not None,
                          tiles_per_seq=mod.tiles_per_seq(tm)),
        grid=(t // tm,),
        in_specs=in_specs,
        out_specs=tile,
        out_shape=jax.ShapeDtypeStruct((t, D_MODEL), F32),
        compiler_params=_params(1),
        name=name,
    )(*args)


FFN_CAST_TILE = 256


def _mixer_ffn_cast_kernel(*refs, has_bias, final_norm):
    it = iter(refs)
    a_ref, wo_ref = next(it), next(it)
    bo_ref = next(it) if has_bias else None
    x_ref, g1_ref, gffn_ref, sc2_ref, sh2_ref, g2_ref = (next(it) for _ in range(6))
    wg_ref, wu_ref, wd_ref = next(it), next(it), next(it)
    gf_ref = next(it) if final_norm else None
    o_ref, wg16_ref, wu16_ref, wd16_ref, x1_ref, h_ref, acc_ref = (next(it) for _ in range(7))
    step = pl.program_id(0)

    @pl.when(step == 0)
    def _():
        out = jnp.dot(a_ref[...], wo_ref[...], preferred_element_type=F32)
        if has_bias:
            out = out + bo_ref[...]
        x = x_ref[...] + g1_ref[...] * out
        y = x * lax.rsqrt(jnp.mean(x * x, axis=-1, keepdims=True) + EPS)
        x1_ref[...] = x
        h_ref[...] = ((y * gffn_ref[...]) * (1.0 + sc2_ref[...]) + sh2_ref[...]).astype(BF16)
        acc_ref[...] = jnp.zeros(acc_ref.shape, F32)

    wg, wu, wd = wg_ref[...].astype(BF16), wu_ref[...].astype(BF16), wd_ref[...].astype(BF16)
    wg16_ref[...] = wg
    wu16_ref[...] = wu
    wd16_ref[...] = wd
    h = h_ref[...]
    gate = jnp.dot(h, wg, preferred_element_type=F32)
    up = jnp.dot(h, wu, preferred_element_type=F32)
    u = ((gate * jax.nn.sigmoid(gate)) * up).astype(BF16)
    acc_ref[...] += jnp.dot(u, wd, preferred_element_type=F32)

    @pl.when(step == pl.num_programs(0) - 1)
    def _():
        x = x1_ref[...] + g2_ref[...] * acc_ref[...]
        if final_norm:
            x = (x * lax.rsqrt(jnp.mean(x * x, axis=-1, keepdims=True) + EPS)) * gf_ref[...]
        o_ref[...] = x


def _mixer_ffn_cast(a, w_out, b_out, x, mod, g_ffn, w_gate32, w_up32, w_down32, layer, *, name, g_final=None):
    rows, k = a.shape
    tf = FFN_CAST_TILE
    tile = pl.BlockSpec((rows, D_MODEL), lambda f: (0, 0))
    vec = pl.BlockSpec((1, D_MODEL), lambda f: (0, 0))
    mod_spec = lambda chunk: mod.spec(chunk, rows, single_tile=True)
    in_specs = [pl.BlockSpec((rows, k), lambda f: (0, 0)),
                w_out.spec((k, D_MODEL), lambda f: (0, 0), pipeline_mode=pl.Buffered(1))]
    args = [a, w_out.arr]
    if b_out is not None:
        in_specs.append(vec)
        args.append(b_out.reshape(1, D_MODEL))
    in_specs += [tile, mod_spec(2), vec, mod_spec(4), mod_spec(3), mod_spec(5),
                 pl.BlockSpec((None, D_MODEL, tf), lambda f: (layer, 0, f)),
                 pl.BlockSpec((None, D_MODEL, tf), lambda f: (layer, 0, f)),
                 pl.BlockSpec((None, tf, D_MODEL), lambda f: (layer, f, 0))]
    args += [x, mod.arr, g_ffn.reshape(1, D_MODEL), mod.arr, mod.arr, mod.arr, w_gate32, w_up32, w_down32]
    if g_final is not None:
        in_specs.append(vec)
        args.append(g_final.reshape(1, D_MODEL))
    y, wg16, wu16, wd16 = pl.pallas_call(
        functools.partial(_mixer_ffn_cast_kernel, has_bias=b_out is not None, final_norm=g_final is not None),
        grid=(FFN_HIDDEN // tf,),
        in_specs=in_specs,
        out_specs=[tile,
                   pl.BlockSpec((None, D_MODEL, tf), lambda f: (0, 0, f)),
                   pl.BlockSpec((None, D_MODEL, tf), lambda f: (0, 0, f)),
                   pl.BlockSpec((None, tf, D_MODEL), lambda f: (0, f, 0))],
        out_shape=[jax.ShapeDtypeStruct((rows, D_MODEL), F32),
                   jax.ShapeDtypeStruct((1, D_MODEL, FFN_HIDDEN), BF16),
                   jax.ShapeDtypeStruct((1, D_MODEL, FFN_HIDDEN), BF16),
                   jax.ShapeDtypeStruct((1, FFN_HIDDEN, D_MODEL), BF16)],
        scratch_shapes=[pltpu.VMEM((rows, D_MODEL), F32), pltpu.VMEM((rows, D_MODEL), BF16),
                        pltpu.VMEM((rows, D_MODEL), F32)],
        compiler_params=_params(1),
        name=name,
    )(*args)
    return y, _LayerWeight(wg16, 0), _LayerWeight(wu16, 0), _LayerWeight(wd16, 0)


ATTN_TILES = 2
ATTN_ROWS = 1024


def _attn_prompt_kernel(q_ref, k_ref, v_ref, bias_ref, o_ref, lse_ref, kcat_ref, vcat_ref):
    step = pl.program_id(2)
    n_res, nq, _ = q_ref.shape
    lane = lax.broadcasted_iota(jnp.int32, (SPAN, HEAD_DIM), 1)
    dims = (((1,), (1,)), ((), ()))
    work = [(j, h) for j in range(ATTN_TILES) for h in range(HEADS)]
    cols = lambda h: slice(h * HEAD_DIM, (h + 1) * HEAD_DIM)

    for res in range(n_res):
        @pl.when(step == 0)
        def _():
            kcat_ref[:SPAN] = jnp.zeros((SPAN, D_MODEL), BF16)
            vcat_ref[:SPAN] = jnp.zeros((SPAN, D_MODEL), BF16)

        @pl.when(step > 0)
        def _():
            kcat_ref[:SPAN] = kcat_ref[nq:]
            vcat_ref[:SPAN] = vcat_ref[nq:]

        kcat_ref[SPAN:] = k_ref[res]
        vcat_ref[SPAN:] = v_ref[res]

        def tiles(it, carry, res=res):
            rows = lambda j: pl.ds(pl.multiple_of((it * ATTN_TILES + j) * SPAN, SPAN), SPAN)
            keys = lambda j: pl.ds(pl.multiple_of((it * ATTN_TILES + j) * SPAN, SPAN), 2 * SPAN)
            starts_seq = jnp.where((step == 0) & (it == 0), 1, 0)
            s, m, p, l, o = {}, {}, {}, {}, {}
            for j, h in work:
                bias = bias_ref[starts_seq if j == 0 else 0, h]
                qk = lax.dot_general(q_ref[res, rows(j), cols(h)], kcat_ref[keys(j), cols(h)], dims,
                                     preferred_element_type=F32)
                s[j, h] = qk * (SCALE * LOG2E) + bias
            for j, h in work:
                m[j, h] = jnp.max(jnp.maximum(s[j, h][:, :SPAN], s[j, h][:, SPAN:]), axis=-1, keepdims=True)
            for j, h in work:
                e = jnp.exp2(s[j, h] - m[j, h])
                l[j, h] = jnp.sum(e[:, :SPAN] + e[:, SPAN:], axis=-1, keepdims=True)
                p[j, h] = e.astype(BF16)
            for j, h in work:
                o[j, h] = jnp.dot(p[j, h], vcat_ref[keys(j), cols(h)], preferred_element_type=F32)
            for j in range(ATTN_TILES):
                lse_all = jnp.zeros((SPAN, HEAD_DIM), F32)
                for h in range(HEADS):
                    o_ref[res, rows(j), cols(h)] = o[j, h] / l[j, h]
                    lse_all = jnp.where(lane == h, (m[j, h] + jnp.log2(l[j, h])) * LN2, lse_all)
                lse_ref[res, rows(j), :] = lse_all
            return carry

        lax.fori_loop(0, nq // (ATTN_TILES * SPAN), tiles, 0)


def _attn_prompt(qkv_g, bias_p, g):
    batch, dil, sub, _ = qkv_g.shape
    nq = min(sub, ATTN_ROWS)
    n_res = min(dil, ATTN_ROWS // nq)
    assert n_res == 1 or nq == sub
    blk = (None, n_res, nq, D_MODEL)
    return pl.pallas_call(
        _attn_prompt_kernel,
        grid=(batch, dil // n_res, sub // nq),
        in_specs=[
            pl.BlockSpec(blk, lambda b, r, t: (b, r, t, 0)),
            pl.BlockSpec(blk, lambda b, r, t: (b, r, t, 1)),
            pl.BlockSpec(blk, lambda b, r, t: (b, r, t, 2)),
            pl.BlockSpec((None, 2, HEADS, SPAN, 2 * SPAN), lambda b, r, t: (g, 0, 0, 0, 0)),
        ],
        out_specs=[
            pl.BlockSpec(blk, lambda b, r, t: (b, r, t, 0)),
            pl.BlockSpec((None, n_res, nq, HEAD_DIM), lambda b, r, t: (b, r, t, 0)),
        ],
        out_shape=[
            jax.ShapeDtypeStruct((batch, dil, sub, D_MODEL), F32),
            jax.ShapeDtypeStruct((batch, dil, sub, HEAD_DIM), F32),
        ],
        scratch_shapes=[pltpu.VMEM((SPAN + nq, D_MODEL), BF16), pltpu.VMEM((SPAN + nq, D_MODEL), BF16)],
        compiler_params=_params(3),
        name=f"attn_prompt_g{g}",
    )(qkv_g, qkv_g, qkv_g, bias_p)


def _attn_sample_kernel(qkv_ref, k0_ref, v0_ref, k1_ref, v1_ref, k2_ref, v2_ref, bias_ref, a_ref, *, bb):
    kv_refs = ((k0_ref, v0_ref), (k1_ref, v1_ref), (k2_ref, v2_ref))
    for bi in range(bb):
        outs, lses = [], []
        for g, (k_ref, v_ref) in enumerate(kv_refs):
            q, k_new, v_new = qkv_ref[bi, 3 * g], qkv_ref[bi, 3 * g + 1], qkv_ref[bi, 3 * g + 2]
            s = jnp.sum(k_ref[bi] * q[None], axis=-1, keepdims=True) * SCALE + bias_ref[g, :SPAN]
            s_new = jnp.sum(k_new * q, axis=-1, keepdims=True) * SCALE + bias_ref[g, SPAN]
            m = jnp.maximum(jnp.max(s, axis=0), s_new)
            e = jnp.exp(s - m[None])
            e_new = jnp.exp(s_new - m)
            l = jnp.sum(e, axis=0) + e_new
            outs.append((jnp.sum(e * v_ref[bi], axis=0) + e_new * v_new) / l)
            lses.append(m + jnp.log(l))
        top = jnp.maximum(jnp.maximum(lses[0], lses[1]), lses[2])
        ws = [jnp.exp(lse - top) for lse in lses]
        den = ws[0] + ws[1] + ws[2]
        a_ref[bi] = (outs[0] * (ws[0] / den) + outs[1] * (ws[1] / den)) + outs[2] * (ws[2] / den)


def _attn_sample(qkv, caches, layer, bias_s):
    b = qkv.shape[0]
    bb = 4
    in_specs = [pl.BlockSpec((bb,) + qkv.shape[1:], lambda i: (i, 0, 0, 0))]
    args = [qkv]
    for g, (_, dil) in enumerate(DIL_GROUPS):
        cache = caches[g]
        assert cache.shape[3] == SPAN * dil
        view = cache.reshape(cache.shape[:3] + (SPAN, dil, HEADS, HEAD_DIM))
        for kv in range(2):
            in_specs.append(pl.BlockSpec((None, bb, None, SPAN, None, HEADS, HEAD_DIM),
                                         lambda i, kv=kv: (layer, i, kv, 0, 0, 0, 0)))
            args.append(view)
    in_specs.append(pl.BlockSpec(bias_s.shape, lambda i: (0, 0, 0, 0)))
    args.append(bias_s)
    return pl.pallas_call(
        functools.partial(_attn_sample_kernel, bb=bb),
        grid=(b // bb,),
        in_specs=in_specs,
        out_specs=pl.BlockSpec((bb, HEADS, HEAD_DIM), lambda i: (i, 0, 0)),
        out_shape=jax.ShapeDtypeStruct((b, HEADS, HEAD_DIM), F32),
        compiler_params=_params(1),
        name="attn_sample",
    )(*args)


def _combine_kernel(o0_ref, o1_ref, o2_ref, l0_ref, l1_ref, l2_ref, a_ref, os_ref, ls_ref):
    tm = a_ref.shape[0]
    for gi, o_ref, l_ref in ((1, o1_ref, l1_ref), (2, o2_ref, l2_ref)):
        dil = DIL_GROUPS[gi][1]
        rows = tm // dil
        for r in range(dil):
            ls_ref[gi - 1, pl.ds(r, rows, stride=dil), :] = l_ref[r]
            for h in range(HEADS):
                os_ref[gi - 1, h, pl.ds(r, rows, stride=dil), :] = o_ref[r, :, h * HEAD_DIM:(h + 1) * HEAD_DIM]
    l0, l1, l2 = l0_ref[0], ls_ref[0], ls_ref[1]
    m = jnp.maximum(jnp.maximum(l0, l1), l2)
    e0, e1, e2 = jnp.exp(l0 - m), jnp.exp(l1 - m), jnp.exp(l2 - m)
    den = e0 + e1 + e2
    w1, w2 = e1 / den, e2 / den
    for h in range(HEADS):
        sl = slice(h * HEAD_DIM, (h + 1) * HEAD_DIM)
        o0 = o0_ref[0, :, sl]
        o = o0 + (os_ref[0, h] - o0) * w1[:, h:h + 1] + (os_ref[1, h] - o0) * w2[:, h:h + 1]
        a_ref[:, sl] = o.astype(a_ref.dtype)


def _combine(outs, lses, seq, tm):
    batch = outs[0].shape[0]
    tiles_per_seq = seq // tm
    in_specs = []
    for width in (D_MODEL, HEAD_DIM):
        for _, dil in DIL_GROUPS:
            in_specs.append(pl.BlockSpec((None, dil, tm // dil, width),
                                         lambda i: (i // tiles_per_seq, 0, i % tiles_per_seq, 0)))
    return pl.pallas_call(
        _combine_kernel,
        grid=(batch * tiles_per_seq,),
        in_specs=in_specs,
        out_specs=pl.BlockSpec((tm, D_MODEL), lambda i: (i, 0)),
        out_shape=jax.ShapeDtypeStruct((batch * seq, D_MODEL), BF16),
        scratch_shapes=[pltpu.VMEM((N_GROUPS - 1, HEADS, tm, HEAD_DIM), F32),
                        pltpu.VMEM((N_GROUPS - 1, tm, HEAD_DIM), F32)],
        compiler_params=_params(1),
        name="combine_groups",
    )(*outs, *lses)


def _ln_silu(z, g_ref, b_ref):
    mu = jnp.mean(z, axis=-1, keepdims=True)
    zc = z - mu
    var = jnp.mean(zc * zc, axis=-1, keepdims=True)
    y = (zc * lax.rsqrt(var + EPS)) * g_ref[...] + b_ref[...]
    return y * jax.nn.sigmoid(y)


SUBLANES = 8
HALO = 32
CONV_ROWS = 128
CONV_CHAINS = 2


def _conv_prompt_kernel(u_ref, prev_ref, w_ref, bdw_ref, g_ref, b_ref, a_ref, ext_ref, z_ref, *, tiles_per_seq):
    tm = u_ref.shape[0]
    starts_seq = pl.program_id(0) % tiles_per_seq == 0
    n_shifted = tm + HALO - SUBLANES
    for c in range(N_LANE_SLABS):
        lanes = slice(c * LANES, (c + 1) * LANES)
        ext_ref[0, c, :HALO, :] = jnp.where(starts_seq, 0.0, prev_ref[:, lanes])
        ext_ref[0, c, HALO:, :] = u_ref[:, lanes]
        for s in range(1, SUBLANES):
            ext_ref[s, c, :n_shifted, :] = ext_ref[0, c, s:s + n_shifted, :]
    off = HALO - (CONV_WIDTH - 1)

    def strip(idx, carry):
        c = idx % N_LANE_SLABS
        r0 = pl.multiple_of((idx // N_LANE_SLABS) * CONV_ROWS, CONV_ROWS)
        accs = [jnp.zeros((CONV_ROWS, LANES), F32) for _ in range(CONV_CHAINS)]
        for shift in range(SUBLANES):
            taps = [k for k in range(CONV_WIDTH) if (off + k) % SUBLANES == shift]
            steps = [(off + k - shift) // SUBLANES for k in taps]
            n_window = SUBLANES * (steps[-1] - steps[0]) + CONV_ROWS
            window = ext_ref[shift, c, pl.ds(r0 + SUBLANES * steps[0], n_window), :]
            for k, q in zip(taps, steps):
                first = SUBLANES * (q - steps[0])
                accs[k % CONV_CHAINS] = accs[k % CONV_CHAINS] + window[first:first + CONV_ROWS] * w_ref[c, k:k + 1, :]
        z_ref[c, pl.ds(r0, CONV_ROWS), :] = sum(accs[1:], accs[0]) + bdw_ref[c]
        return carry

    lax.fori_loop(0, (tm // CONV_ROWS) * N_LANE_SLABS, strip, 0)
    z = jnp.concatenate([z_ref[c] for c in range(N_LANE_SLABS)], axis=-1)
    a_ref[...] = _ln_silu(z, g_ref, b_ref).astype(a_ref.dtype)


def _conv_prompt(u, w_dw, b_dw, ln_g, ln_b, seq, tm):
    t = u.shape[0]
    row = lambda v: v.reshape(1, D_MODEL)
    vec = pl.BlockSpec((1, D_MODEL), lambda i: (0, 0))
    slabs = lambda v: jnp.transpose(v.reshape(-1, N_LANE_SLABS, LANES), (1, 0, 2))
    return pl.pallas_call(
        functools.partial(_conv_prompt_kernel, tiles_per_seq=seq // tm),
        grid=(t // tm,),
        in_specs=[
            pl.BlockSpec((tm, D_MODEL), lambda i: (i, 0)),
            pl.BlockSpec((HALO, D_MODEL), lambda i: (jnp.maximum(i * (tm // HALO) - 1, 0), 0)),
            pl.BlockSpec((N_LANE_SLABS, CONV_WIDTH, LANES), lambda i: (0, 0, 0)),
            pl.BlockSpec((N_LANE_SLABS, 1, LANES), lambda i: (0, 0, 0)),
            vec, vec,
        ],
        out_specs=pl.BlockSpec((tm, D_MODEL), lambda i: (i, 0)),
        out_shape=jax.ShapeDtypeStruct((t, D_MODEL), BF16),
        scratch_shapes=[pltpu.VMEM((SUBLANES, N_LANE_SLABS, tm + HALO, LANES), F32),
                        pltpu.VMEM((N_LANE_SLABS, tm, LANES), F32)],
        compiler_params=_params(1),
        name="conv_prompt",
    )(u, u, slabs(w_dw), slabs(b_dw), row(ln_g), row(ln_b))


def _conv_sample_kernel(u_ref, st_ref, w_ref, bdw_ref, g_ref, b_ref, a_ref, ns_ref):
    n_state = st_ref.shape[0]
    u = u_ref[...]
    z = u * w_ref[n_state:n_state + 1, :] + bdw_ref[...]
    for k in range(n_state):
        z = z + st_ref[k] * w_ref[k:k + 1, :]
    a_ref[...] = _ln_silu(z, g_ref, b_ref).astype(a_ref.dtype)
    for k in range(n_state - 1):
        ns_ref[k] = st_ref[k + 1]
    ns_ref[n_state - 1] = u


def _conv_sample(u, state, layer, w_dw, b_dw, ln_g, ln_b):
    _, b, n_state, _ = state.shape
    bb = 16
    row = lambda v: v.reshape(1, D_MODEL)
    vec = pl.BlockSpec((1, D_MODEL), lambda i: (0, 0))
    a, new_state = pl.pallas_call(
        _conv_sample_kernel,
        grid=(b // bb,),
        in_specs=[
            pl.BlockSpec((bb, D_MODEL), lambda i: (i, 0)),
            pl.BlockSpec((None, n_state, bb, D_MODEL), lambda i: (layer, 0, i, 0)),
            pl.BlockSpec((CONV_WIDTH, D_MODEL), lambda i: (0, 0)),
            vec, vec, vec,
        ],
        out_specs=[
            pl.BlockSpec((bb, D_MODEL), lambda i: (i, 0)),
            pl.BlockSpec((None, n_state, bb, D_MODEL), lambda i: (0, 0, i, 0)),
        ],
        out_shape=[
            jax.ShapeDtypeStruct((b, D_MODEL), BF16),
            jax.ShapeDtypeStruct((1, n_state, b, D_MODEL), F32),
        ],
        compiler_params=_params(1),
        name="conv_sample",
    )(u, jnp.transpose(state, (0, 2, 1, 3)), w_dw, row(b_dw), row(ln_g), row(ln_b))
    return a, jnp.transpose(new_state, (0, 2, 1, 3))


PROMPT_FFN_ROWS = 512


def kernel(x_prompt, x_sample, cache_kv_w128, cache_kv_w512, cache_kv_w2048, state_conv, c_prompt, c_sample,
           w_mod, b_mod, g_mix, g_ffn, g_final, w_qkv, w_o, rel_bias, w_pw1, b_pw1, w_dw, b_dw, ln_g, ln_b,
           w_pw2, b_pw2, w_gate, w_up, w_down):
    batch, seq, d = x_prompt.shape
    dec_batch = x_sample.shape[0]
    assert d == D_MODEL and x_sample.shape[1] == 1
    caches = (cache_kv_w128, cache_kv_w512, cache_kv_w2048)

    def layers(w):
        w = w.astype(BF16)
        return [_LayerWeight(w, layer) for layer in range(w.shape[0])]

    w_o, w_pw1, w_pw2 = map(layers, (w_o, w_pw1, w_pw2))

    c_all = jnp.concatenate([c_sample, c_prompt, jnp.zeros((8 - batch, d), F32)], axis=0)
    mod_all = _modulation(c_all, w_mod, b_mod)
    bias_p, bias_s = _bias_tables(rel_bias)

    xp = x_prompt.reshape(batch * seq, d)
    xs = x_sample.reshape(dec_batch, d)

    mod_p = _Mod(mod_all, 0, False, seq)
    mod_s = _Mod(mod_all, 0, True, 1)

    qkv_s, w_qkv16 = _qkv_sample(xs, g_mix[0], mod_s, w_qkv, 0, 1536)
    w_qkv0 = _LayerWeight(w_qkv16, 0)
    qkv_s4 = qkv_s.reshape(dec_batch, 3 * N_GROUPS, HEADS, HEAD_DIM)
    bias_s4 = jnp.stack([bias_s[g, :, g * HEADS:(g + 1) * HEADS] for g in range(N_GROUPS)])[..., None]
    a_s = _attn_sample(qkv_s4, caches, 0, bias_s4).reshape(dec_batch, d).astype(BF16)
    xs, w_gate0, w_up0, w_down0 = _mixer_ffn_cast(a_s, w_o[0], None, xs, mod_s, g_ffn[0], w_gate, w_up, w_down, 0,
                                                  name="attn_out_ffn_sample")

    qkv_p = _qkv_prompt(xp, g_mix[0], mod_p, w_qkv0, batch, seq, 512)

    kv_prompt = [_kv_tail(qkv_p[N_GROUPS], w_qkv0, g, batch, seq, min(window, seq))
                 .reshape(1, batch, 2, min(window, seq), HEADS, HEAD_DIM) for g, (window, _) in enumerate(DIL_GROUPS)]

    outs, lses = zip(*[_attn_prompt(qkv_p[g], bias_p, g) for g in range(N_GROUPS)])
    a_p = _combine(outs, lses, seq, 1024)
    xp = _mixer_ffn(a_p, w_o[0], None, xp, mod_p, g_ffn[0], w_gate0, w_up0, w_down0,
                    tm=PROMPT_FFN_ROWS, name="attn_out_ffn_prompt")

    mod_p = _Mod(mod_all, 1, False, seq)
    mod_s = _Mod(mod_all, 1, True, 1)
    b_pw1_row = b_pw1[0].reshape(1, 2 * d)

    def glu(x, mod, tm, tag):
        return _norm_glu(x, g_mix[1], mod, (0, 1), [(w_pw1[0], 0), (w_pw1[0], 1)],
                                 [(b_pw1_row, 0), (b_pw1_row, 1)],
                                 tm=tm, tn=d, n_cols=d, out_dtype=F32, name=f"glu_{tag}")

    u_s = glu(xs, mod_s, dec_batch, "sample")
    a_s, conv_s = _conv_sample(u_s, state_conv, 0, w_dw[0], b_dw[0], ln_g[0], ln_b[0])
    y_s, w_gate1, w_up1, w_down1 = _mixer_ffn_cast(a_s, w_pw2[0], b_pw2[0], xs, mod_s, g_ffn[1], w_gate, w_up,
                                                   w_down, 1, name="conv_out_ffn_sample", g_final=g_final)

    u_p = glu(xp, mod_p, 1024, "prompt")
    a_p = _conv_prompt(u_p, w_dw[0], b_dw[0], ln_g[0], ln_b[0], seq, 512)
    y_p = _mixer_ffn(a_p, w_pw2[0], b_pw2[0], xp, mod_p, g_ffn[1], w_gate1, w_up1, w_down1,
                     tm=PROMPT_FFN_ROWS, name="conv_out_ffn_prompt", g_final=g_final)

    conv_p = u_p.reshape(batch, seq, d)[:, seq - (CONV_WIDTH - 1):][None]
    kv_sample = [qkv_s4[:, 3 * g + 1:3 * g + 3].reshape(1, dec_batch, 2, 1, HEADS, HEAD_DIM)
                 for g in range(N_GROUPS)]
    return (y_p.reshape(batch, seq, d), y_s.reshape(dec_batch, 1, d),
            kv_prompt[0], kv_prompt[1], kv_prompt[2], conv_p,
            kv_sample[0], kv_sample[1], kv_sample[2], conv_s)
```

```python
import functools
import math

import numpy as np
import jax
import jax.numpy as jnp
from jax import lax
from jax.experimental import pallas as pl
from jax.experimental.pallas import tpu as pltpu

D_MODEL = 1024
DIL_GROUPS = ((128, 1), (512, 4), (2048, 16))
N_GROUPS = len(DIL_GROUPS)
HEADS = 8
HEAD_DIM = 128
SPAN = 128
N_BUCKETS = 32
MAX_DISTANCE = 2048
CONV_WIDTH = 31
FFN_HIDDEN = 2816
EPS = 1e-6
NEG_INF = -1e30
SCALE = HEAD_DIM ** -0.5
LOG2E = math.log2(math.e)
LN2 = math.log(2.0)

F32 = jnp.float32
BF16 = jnp.bfloat16

VMEM_LIMIT_BYTES = 56 * 1024 * 1024


def _params(n_axes, vmem=VMEM_LIMIT_BYTES):
    return pltpu.CompilerParams(dimension_semantics=("arbitrary",) * n_axes, vmem_limit_bytes=vmem)


def _t5_bucket_np(dist):
    max_exact = N_BUCKETS // 2
    n = np.maximum(dist, 1).astype(np.float32)
    large = max_exact + (np.log(n / np.float32(max_exact)) / np.float32(math.log(MAX_DISTANCE / max_exact))
                         * np.float32(N_BUCKETS - max_exact)).astype(np.int32)
    large = np.minimum(large, N_BUCKETS - 1)
    return np.where(dist < max_exact, dist, large).astype(np.int32)


def _bucket_tables():
    qi = np.arange(SPAN)[:, None]
    ki = np.arange(2 * SPAN)[None, :]
    delta = qi + SPAN - ki
    in_band = (delta >= 0) & (delta <= SPAN)
    prompt, sample = [], []
    for _, dilation in DIL_GROUPS:
        b = _t5_bucket_np(np.clip(delta, 0, SPAN) * dilation)
        prompt.append(np.where(in_band, b, -1))
        j = SPAN - np.arange(136)
        sample.append(_t5_bucket_np(np.maximum(j, 0) * dilation)[:, None])
    return np.stack(prompt).astype(np.int32), np.stack(sample).astype(np.int32)


def _bias_kernel(rb_smem, rb_ref, bp_ref, bs_ref, op_ref, os_ref):
    g = pl.program_id(0)
    bp = bp_ref[...]
    in_prev_block = lax.broadcasted_iota(jnp.int32, bp.shape, 1) < SPAN
    for h in range(HEADS):
        acc = jnp.full(bp.shape, NEG_INF, F32)
        for b in range(N_BUCKETS):
            acc = jnp.where(bp == b, rb_smem[b, g * HEADS + h] * LOG2E, acc)
        op_ref[0, h] = acc
        op_ref[1, h] = jnp.where(in_prev_block, NEG_INF, acc)
    bs = bs_ref[...]
    acc = jnp.zeros((bs.shape[0], rb_ref.shape[1]), F32)
    for b in range(N_BUCKETS):
        acc = jnp.where(bs == b, rb_ref[b:b + 1, :], acc)
    os_ref[...] = acc


def _bias_tables(rel_bias):
    bp, bs = _bucket_tables()
    n_cols = rel_bias.shape[1]
    return pl.pallas_call(
        _bias_kernel,
        grid=(N_GROUPS,),
        in_specs=[
            pl.BlockSpec(memory_space=pltpu.SMEM),
            pl.BlockSpec((N_BUCKETS, n_cols), lambda g: (0, 0)),
            pl.BlockSpec((None, SPAN, 2 * SPAN), lambda g: (g, 0, 0)),
            pl.BlockSpec((None, 136, 1), lambda g: (g, 0, 0)),
        ],
        out_specs=[
            pl.BlockSpec((None, 2, HEADS, SPAN, 2 * SPAN), lambda g: (g, 0, 0, 0, 0)),
            pl.BlockSpec((None, 136, n_cols), lambda g: (g, 0, 0)),
        ],
        out_shape=[
            jax.ShapeDtypeStruct((N_GROUPS, 2, HEADS, SPAN, 2 * SPAN), F32),
            jax.ShapeDtypeStruct((N_GROUPS, 136, n_cols), F32),
        ],
        compiler_params=_params(1),
        name="bias_tables",
    )(rel_bias, rel_bias, jnp.asarray(bp), jnp.asarray(bs))


def _mod_kernel(c_ref, w_ref, b_ref, o_ref):
    c = c_ref[...]
    a = (c * jax.nn.sigmoid(c)).astype(BF16)
    o_ref[...] = jnp.dot(a, w_ref[...].astype(BF16), preferred_element_type=F32) + b_ref[...]


def _modulation(c_all, w_mod, b_mod):
    depth, d, n = w_mod.shape
    rows = c_all.shape[0]
    tn = 1536
    return pl.pallas_call(
        _mod_kernel,
        grid=(depth, n // tn),
        in_specs=[
            pl.BlockSpec((rows, d), lambda l, j: (0, 0)),
            pl.BlockSpec((None, d, tn), lambda l, j: (l, 0, j)),
            pl.BlockSpec((None, 1, tn), lambda l, j: (l, 0, j)),
        ],
        out_specs=pl.BlockSpec((None, rows, tn), lambda l, j: (l, 0, j)),
        out_shape=jax.ShapeDtypeStruct((depth, rows, n), F32),
        compiler_params=_params(2),
        name="modulation",
    )(c_all, w_mod, b_mod.reshape(depth, 1, n))


MOD_ROW_BLOCK = 8


def _mod_rows(ref, tiles_per_seq, tile_axis=0):
    rows = ref[...]
    if tiles_per_seq is None:
        return rows
    seq_id = pl.program_id(tile_axis) // tiles_per_seq
    pick = lax.broadcasted_iota(jnp.int32, rows.shape, 0) == seq_id
    return jnp.sum(jnp.where(pick, rows, 0.0), axis=0, keepdims=True)


def _normed_f32(x_ref, g_ref, sc_ref, sh_ref, tiles_per_seq, tile_axis=0):
    x = x_ref[...]
    y = x * lax.rsqrt(jnp.mean(x * x, axis=-1, keepdims=True) + EPS)
    scale = _mod_rows(sc_ref, tiles_per_seq, tile_axis)
    return (y * g_ref[...]) * (1.0 + scale) + _mod_rows(sh_ref, tiles_per_seq, tile_axis)


def _normed(x_ref, g_ref, sc_ref, sh_ref, tiles_per_seq, tile_axis=0):
    return _normed_f32(x_ref, g_ref, sc_ref, sh_ref, tiles_per_seq, tile_axis).astype(BF16)


def _norm_glu_kernel(x_ref, g_ref, sc_ref, sh_ref, wa_ref, wg_ref, ba_ref, bg_ref, o_ref, h_ref, *, tiles_per_seq):
    @pl.when(pl.program_id(1) == 0)
    def _():
        h_ref[...] = _normed(x_ref, g_ref, sc_ref, sh_ref, tiles_per_seq)
    h = h_ref[...]
    a = jnp.dot(h, wa_ref[...], preferred_element_type=F32) + ba_ref[...]
    gate = jnp.dot(h, wg_ref[...], preferred_element_type=F32) + bg_ref[...]
    o_ref[...] = a * jax.nn.sigmoid(gate)


class _LayerWeight:
    def __init__(self, stacked, layer):
        self.arr = stacked
        self.layer = layer

    def spec(self, block, index, **kwargs):
        layer = self.layer
        return pl.BlockSpec((None,) + tuple(block), lambda *i: (layer,) + tuple(index(*i)), **kwargs)


class _Mod:
    def __init__(self, mod_all, layer, per_token, rows_per_seq):
        self.layer = layer
        self.per_token = per_token
        self.rows_per_seq = rows_per_seq
        self.arr = mod_all
        self.first_seq_block = (mod_all.shape[1] - MOD_ROW_BLOCK) // MOD_ROW_BLOCK

    def spec(self, chunk, tm, single_tile=False):
        layer = self.layer
        if self.per_token:
            return pl.BlockSpec((None, tm, D_MODEL), lambda *i: (layer, 0 if single_tile else i[0], chunk))
        block = self.first_seq_block
        return pl.BlockSpec((None, MOD_ROW_BLOCK, D_MODEL), lambda *i: (layer, block, chunk))

    def tiles_per_seq(self, tm):
        return None if self.per_token else self.rows_per_seq // tm


def _norm_glu(x, gain, mod, chunks, weights, biases, *, tm, tn, n_cols, out_dtype, name):
    t = x.shape[0]
    kernel = functools.partial(_norm_glu_kernel, tiles_per_seq=mod.tiles_per_seq(tm))
    chunk_shift, chunk_scale = chunks
    in_specs = [
        pl.BlockSpec((tm, D_MODEL), lambda i, j: (i, 0)),
        pl.BlockSpec((1, D_MODEL), lambda i, j: (0, 0)),
        mod.spec(chunk_scale, tm),
        mod.spec(chunk_shift, tm),
    ]
    args = [x, gain.reshape(1, D_MODEL), mod.arr, mod.arr]
    resident = dict(pipeline_mode=pl.Buffered(1)) if n_cols == tn else {}
    for w, col0 in weights:
        in_specs.append(w.spec((D_MODEL, tn), lambda i, j, col0=col0: (0, col0 + j), **resident))
        args.append(w.arr)
    for b, col0 in biases:
        in_specs.append(pl.BlockSpec((1, tn), lambda i, j, col0=col0: (0, col0 + j)))
        args.append(b)
    return pl.pallas_call(
        kernel,
        grid=(t // tm, n_cols // tn),
        in_specs=in_specs,
        out_specs=pl.BlockSpec((tm, tn), lambda i, j: (i, j)),
        out_shape=jax.ShapeDtypeStruct((t, n_cols), out_dtype),
        scratch_shapes=[pltpu.VMEM((tm, D_MODEL), BF16)],
        compiler_params=_params(2),
        name=name,
    )(*args)


def _norm_mm_cast_kernel(x_ref, g_ref, sc_ref, sh_ref, w_ref, o_ref, w16_ref, h_ref):
    @pl.when(pl.program_id(0) == 0)
    def _():
        h_ref[...] = _normed(x_ref, g_ref, sc_ref, sh_ref, None)
    w = w_ref[...].astype(BF16)
    w16_ref[...] = w
    o_ref[...] = jnp.dot(h_ref[...], w, preferred_element_type=F32)


def _qkv_sample(x, gain, mod, w32, layer, tn):
    rows = x.shape[0]
    n = w32.shape[-1]
    return pl.pallas_call(
        _norm_mm_cast_kernel,
        grid=(n // tn,),
        in_specs=[
            pl.BlockSpec((rows, D_MODEL), lambda j: (0, 0)),
            pl.BlockSpec((1, D_MODEL), lambda j: (0, 0)),
            mod.spec(1, rows, single_tile=True),
            mod.spec(0, rows, single_tile=True),
            pl.BlockSpec((None, D_MODEL, tn), lambda j: (layer, 0, j)),
        ],
        out_specs=[
            pl.BlockSpec((rows, tn), lambda j: (0, j)),
            pl.BlockSpec((None, D_MODEL, tn), lambda j: (0, 0, j)),
        ],
        out_shape=[
            jax.ShapeDtypeStruct((rows, n), F32),
            jax.ShapeDtypeStruct((1, D_MODEL, n), BF16),
        ],
        scratch_shapes=[pltpu.VMEM((rows, D_MODEL), BF16)],
        compiler_params=_params(1),
        name="qkv_sample",
    )(x, gain.reshape(1, D_MODEL), mod.arr, mod.arr, w32)


def _mm_kernel(a_ref, w_ref, o_ref):
    o_ref[...] = jnp.dot(a_ref[...], w_ref[...], preferred_element_type=F32)


def _mm_per_seq_kernel(*refs):
    *a_refs, w_ref, o_ref = refs
    for b, a_ref in enumerate(a_refs):
        o_ref[b] = jnp.dot(a_ref[...], w_ref[...], preferred_element_type=F32)


KV_TAIL_ONE_STEP_ROWS = 512


def _kv_tail(h, w_qkv, g, batch, seq, keep):
    if keep <= KV_TAIL_ONE_STEP_ROWS:
        return pl.pallas_call(
            _mm_per_seq_kernel,
            grid=(2,),
            in_specs=[pl.BlockSpec((keep, D_MODEL), lambda j, b=b: ((b + 1) * (seq // keep) - 1, 0))
                      for b in range(batch)]
            + [w_qkv.spec((D_MODEL, D_MODEL), lambda j: (0, 3 * g + 1 + j))],
            out_specs=pl.BlockSpec((batch, None, keep, D_MODEL), lambda j: (0, j, 0, 0)),
            out_shape=jax.ShapeDtypeStruct((batch, 2, keep, D_MODEL), F32),
            compiler_params=_params(1),
            name=f"kv_tail_g{g}",
        )(*([h] * batch), w_qkv.arr)
    tm = min(keep, 1024)
    per_seq = keep // tm
    first = (seq - keep) // tm
    return pl.pallas_call(
        _mm_kernel,
        grid=(2, batch * per_seq),
        in_specs=[
            pl.BlockSpec((tm, D_MODEL), lambda j, i: ((i // per_seq) * (seq // tm) + first + i % per_seq, 0)),
            w_qkv.spec((D_MODEL, D_MODEL), lambda j, i: (0, 3 * g + 1 + j)),
        ],
        out_specs=pl.BlockSpec((None, None, tm, D_MODEL), lambda j, i: (i // per_seq, j, i % per_seq, 0)),
        out_shape=jax.ShapeDtypeStruct((batch, 2, keep, D_MODEL), F32),
        compiler_params=_params(2),
        name=f"kv_tail_g{g}",
    )(h, w_qkv.arr)


LANES = 128
N_LANE_SLABS = D_MODEL // LANES


def _qkv_prompt_kernel(x_ref, g_ref, sc_ref, sh_ref, w_ref, *rest, dil, tiles_per_seq):
    tm = x_ref.shape[0]
    quarter, sixteenth = tm // 4, tm // 16
    if dil == 1:
        o_ref, hn_ref, h_ref = rest
        h = _normed(x_ref, g_ref, sc_ref, sh_ref, tiles_per_seq)
        hn_ref[...] = h
        h_ref[...] = h
    else:
        o_ref, h_ref, slab_ref = rest
        h = _normed_f32(x_ref, g_ref, sc_ref, sh_ref, tiles_per_seq)
        for c in range(N_LANE_SLABS):
            lanes = slice(c * LANES, (c + 1) * LANES)
            slab_ref[0, c] = h[:, lanes]
            for b in range(4):
                part = slab_ref[0, c, pl.ds(b, quarter, stride=4), :]
                if dil == 4:
                    h_ref[b * quarter:(b + 1) * quarter, lanes] = part.astype(BF16)
                else:
                    slab_ref[1, c, b * quarter:(b + 1) * quarter, :] = part
            if dil == 16:
                for b in range(4):
                    for a in range(4):
                        r = 4 * a + b
                        part = slab_ref[1, c, pl.ds(b * quarter + a, sixteenth, stride=4), :]
                        h_ref[r * sixteenth:(r + 1) * sixteenth, lanes] = part.astype(BF16)
    rows = tm // dil
    for c in range(3):
        cols = slice(c * D_MODEL, (c + 1) * D_MODEL)
        acc = jnp.dot(h_ref[...], w_ref[:, cols], preferred_element_type=F32).astype(BF16)
        for r in range(dil):
            o_ref[r, :, cols] = acc[r * rows:(r + 1) * rows, :]


def _qkv_prompt(x, gain, mod, w, batch, seq, tm):
    assert [dil for _, dil in DIL_GROUPS] == [1, 4, 16]
    tiles_per_seq = seq // tm
    results = []
    for g, (_, dil) in enumerate(DIL_GROUPS):
        out_specs = [pl.BlockSpec((None, dil, tm // dil, 3 * D_MODEL),
                                  lambda i: (i // tiles_per_seq, 0, i % tiles_per_seq, 0))]
        out_shapes = [jax.ShapeDtypeStruct((batch, dil, seq // dil, 3 * D_MODEL), BF16)]
        scratch = [pltpu.VMEM((tm, D_MODEL), BF16)]
        if dil == 1:
            out_specs.append(pl.BlockSpec((tm, D_MODEL), lambda i: (i, 0)))
            out_shapes.append(jax.ShapeDtypeStruct((batch * seq, D_MODEL), BF16))
        else:
            scratch.append(pltpu.VMEM((2, N_LANE_SLABS, tm, LANES), F32))
        results.append(pl.pallas_call(
            functools.partial(_qkv_prompt_kernel, dil=dil, tiles_per_seq=tiles_per_seq),
            grid=(batch * tiles_per_seq,),
            in_specs=[
                pl.BlockSpec((tm, D_MODEL), lambda i: (i, 0)),
                pl.BlockSpec((1, D_MODEL), lambda i: (0, 0)),
                mod.spec(1, tm),
                mod.spec(0, tm),
                w.spec((D_MODEL, 3 * D_MODEL), lambda i, g=g: (0, g), pipeline_mode=pl.Buffered(1)),
            ],
            out_specs=out_specs,
            out_shape=out_shapes,
            scratch_shapes=scratch,
            compiler_params=_params(1),
            name=f"qkv_prompt_g{g}",
        )(x, gain.reshape(1, D_MODEL), mod.arr, mod.arr, w.arr))
    (qkv0, h_tokens), (qkv1,), (qkv2,) = results
    return qkv0, qkv1, qkv2, h_tokens


def _mixer_ffn_kernel(*refs, has_bias, final_norm, tiles_per_seq):
    it = iter(refs)
    a_ref, wo_ref = next(it), next(it)
    bo_ref = next(it) if has_bias else None
    x_ref, g1_ref, gffn_ref, sc2_ref, sh2_ref, g2_ref = (next(it) for _ in range(6))
    wg_ref, wu_ref, wd_ref = next(it), next(it), next(it)
    gf_ref = next(it) if final_norm else None
    o_ref = next(it)
    rows = lambda ref: _mod_rows(ref, tiles_per_seq)

    out = jnp.dot(a_ref[...], wo_ref[...], preferred_element_type=F32)
    if has_bias:
        out = out + bo_ref[...]
    x = x_ref[...] + rows(g1_ref) * out
    y = x * lax.rsqrt(jnp.mean(x * x, axis=-1, keepdims=True) + EPS)
    h = ((y * gffn_ref[...]) * (1.0 + rows(sc2_ref)) + rows(sh2_ref)).astype(BF16)
    gate = jnp.dot(h, wg_ref[...], preferred_element_type=F32)
    up = jnp.dot(h, wu_ref[...], preferred_element_type=F32)
    u = ((gate * jax.nn.sigmoid(gate)) * up).astype(BF16)
    x = x + rows(g2_ref) * jnp.dot(u, wd_ref[...], preferred_element_type=F32)
    if final_norm:
        x = (x * lax.rsqrt(jnp.mean(x * x, axis=-1, keepdims=True) + EPS)) * gf_ref[...]
    o_ref[...] = x


def _mixer_ffn(a, w_out, b_out, x, mod, g_ffn, w_gate, w_up, w_down, *, tm, name, g_final=None):
    t, k = a.shape
    tile = pl.BlockSpec((tm, D_MODEL), lambda i: (i, 0))
    vec = pl.BlockSpec((1, D_MODEL), lambda i: (0, 0))
    once = dict(pipeline_mode=pl.Buffered(1))
    in_specs = [pl.BlockSpec((tm, k), lambda i: (i, 0)), w_out.spec((k, D_MODEL), lambda i: (0, 0), **once)]
    args = [a, w_out.arr]
    if b_out is not None:
        in_specs.append(vec)
        args.append(b_out.reshape(1, D_MODEL))
    in_specs += [tile, mod.spec(2, tm), vec, mod.spec(4, tm), mod.spec(3, tm), mod.spec(5, tm),
                 w_gate.spec((D_MODEL, FFN_HIDDEN), lambda i: (0, 0), **once),
                 w_up.spec((D_MODEL, FFN_HIDDEN), lambda i: (0, 0), **once),
                 w_down.spec((FFN_HIDDEN, D_MODEL), lambda i: (0, 0), **once)]
    args += [x, mod.arr, g_ffn.reshape(1, D_MODEL), mod.arr, mod.arr, mod.arr, w_gate.arr, w_up.arr, w_down.arr]
    if g_final is not None:
        in_specs.append(vec)
        args.append(g_final.reshape(1, D_MODEL))
    return pl.pallas_call(
        functools.partial(_mixer_ffn_kernel, has_bias=b_out is not None, final_norm=g_final is not None,
                          tiles_per_seq=mod.tiles_per_seq(tm)),
        grid=(t // tm,),
        in_specs=in_specs,
        out_specs=tile,
        out_shape=jax.ShapeDtypeStruct((t, D_MODEL), F32),
        compiler_params=_params(1),
        name=name,
    )(*args)


FFN_CAST_TILE = 256


def _mixer_ffn_cast_kernel(*refs, has_bias, final_norm):
    it = iter(refs)
    a_ref, wo_ref = next(it), next(it)
    bo_ref = next(it) if has_bias else None
    x_ref, g1_ref, gffn_ref, sc2_ref, sh2_ref, g2_ref = (next(it) for _ in range(6))
    wg_ref, wu_ref, wd_ref = next(it), next(it), next(it)
    gf_ref = next(it) if final_norm else None
    o_ref, wg16_ref, wu16_ref, wd16_ref, x1_ref, h_ref, acc_ref = (next(it) for _ in range(7))
    step = pl.program_id(0)

    @pl.when(step == 0)
    def _():
        out = jnp.dot(a_ref[...], wo_ref[...], preferred_element_type=F32)
        if has_bias:
            out = out + bo_ref[...]
        x = x_ref[...] + g1_ref[...] * out
        y = x * lax.rsqrt(jnp.mean(x * x, axis=-1, keepdims=True) + EPS)
        x1_ref[...] = x
        h_ref[...] = ((y * gffn_ref[...]) * (1.0 + sc2_ref[...]) + sh2_ref[...]).astype(BF16)
        acc_ref[...] = jnp.zeros(acc_ref.shape, F32)

    wg, wu, wd = wg_ref[...].astype(BF16), wu_ref[...].astype(BF16), wd_ref[...].astype(BF16)
    wg16_ref[...] = wg
    wu16_ref[...] = wu
    wd16_ref[...] = wd
    h = h_ref[...]
    gate = jnp.dot(h, wg, preferred_element_type=F32)
    up = jnp.dot(h, wu, preferred_element_type=F32)
    u = ((gate * jax.nn.sigmoid(gate)) * up).astype(BF16)
    acc_ref[...] += jnp.dot(u, wd, preferred_element_type=F32)

    @pl.when(step == pl.num_programs(0) - 1)
    def _():
        x = x1_ref[...] + g2_ref[...] * acc_ref[...]
        if final_norm:
            x = (x * lax.rsqrt(jnp.mean(x * x, axis=-1, keepdims=True) + EPS)) * gf_ref[...]
        o_ref[...] = x


def _mixer_ffn_cast(a, w_out, b_out, x, mod, g_ffn, w_gate32, w_up32, w_down32, layer, *, name, g_final=None):
    rows, k = a.shape
    tf = FFN_CAST_TILE
    tile = pl.BlockSpec((rows, D_MODEL), lambda f: (0, 0))
    vec = pl.BlockSpec((1, D_MODEL), lambda f: (0, 0))
    mod_spec = lambda chunk: mod.spec(chunk, rows, single_tile=True)
    in_specs = [pl.BlockSpec((rows, k), lambda f: (0, 0)),
                w_out.spec((k, D_MODEL), lambda f: (0, 0), pipeline_mode=pl.Buffered(1))]
    args = [a, w_out.arr]
    if b_out is not None:
        in_specs.append(vec)
        args.append(b_out.reshape(1, D_MODEL))
    in_specs += [tile, mod_spec(2), vec, mod_spec(4), mod_spec(3), mod_spec(5),
                 pl.BlockSpec((None, D_MODEL, tf), lambda f: (layer, 0, f)),
                 pl.BlockSpec((None, D_MODEL, tf), lambda f: (layer, 0, f)),
                 pl.BlockSpec((None, tf, D_MODEL), lambda f: (layer, f, 0))]
    args += [x, mod.arr, g_ffn.reshape(1, D_MODEL), mod.arr, mod.arr, mod.arr, w_gate32, w_up32, w_down32]
    if g_final is not None:
        in_specs.append(vec)
        args.append(g_final.reshape(1, D_MODEL))
    y, wg16, wu16, wd16 = pl.pallas_call(
        functools.partial(_mixer_ffn_cast_kernel, has_bias=b_out is not None, final_norm=g_final is not None),
        grid=(FFN_HIDDEN // tf,),
        in_specs=in_specs,
        out_specs=[tile,
                   pl.BlockSpec((None, D_MODEL, tf), lambda f: (0, 0, f)),
                   pl.BlockSpec((None, D_MODEL, tf), lambda f: (0, 0, f)),
                   pl.BlockSpec((None, tf, D_MODEL), lambda f: (0, f, 0))],
        out_shape=[jax.ShapeDtypeStruct((rows, D_MODEL), F32),
                   jax.ShapeDtypeStruct((1, D_MODEL, FFN_HIDDEN), BF16),
                   jax.ShapeDtypeStruct((1, D_MODEL, FFN_HIDDEN), BF16),
                   jax.ShapeDtypeStruct((1, FFN_HIDDEN, D_MODEL), BF16)],
        scratch_shapes=[pltpu.VMEM((rows, D_MODEL), F32), pltpu.VMEM((rows, D_MODEL), BF16),
                        pltpu.VMEM((rows, D_MODEL), F32)],
        compiler_params=_params(1),
        name=name,
    )(*args)
    return y, _LayerWeight(wg16, 0), _LayerWeight(wu16, 0), _LayerWeight(wd16, 0)


ATTN_TILES = 2
ATTN_ROWS = 1024


def _attn_prompt_kernel(q_ref, k_ref, v_ref, bias_ref, o_ref, lse_ref, kcat_ref, vcat_ref):
    step = pl.program_id(2)
    n_res, nq, _ = q_ref.shape
    lane = lax.broadcasted_iota(jnp.int32, (SPAN, HEAD_DIM), 1)
    dims = (((1,), (1,)), ((), ()))
    work = [(j, h) for j in range(ATTN_TILES) for h in range(HEADS)]
    cols = lambda h: slice(h * HEAD_DIM, (h + 1) * HEAD_DIM)

    for res in range(n_res):
        @pl.when(step == 0)
        def _():
            kcat_ref[:SPAN] = jnp.zeros((SPAN, D_MODEL), BF16)
            vcat_ref[:SPAN] = jnp.zeros((SPAN, D_MODEL), BF16)

        @pl.when(step > 0)
        def _():
            kcat_ref[:SPAN] = kcat_ref[nq:]
            vcat_ref[:SPAN] = vcat_ref[nq:]

        kcat_ref[SPAN:] = k_ref[res]
        vcat_ref[SPAN:] = v_ref[res]

        def tiles(it, carry, res=res):
            rows = lambda j: pl.ds(pl.multiple_of((it * ATTN_TILES + j) * SPAN, SPAN), SPAN)
            keys = lambda j: pl.ds(pl.multiple_of((it * ATTN_TILES + j) * SPAN, SPAN), 2 * SPAN)
            starts_seq = jnp.where((step == 0) & (it == 0), 1, 0)
            s, m, p, l, o = {}, {}, {}, {}, {}
            for j, h in work:
                bias = bias_ref[starts_seq if j == 0 else 0, h]
                qk = lax.dot_general(q_ref[res, rows(j), cols(h)], kcat_ref[keys(j), cols(h)], dims,
                                     preferred_element_type=F32)
                s[j, h] = qk * (SCALE * LOG2E) + bias
            for j, h in work:
                m[j, h] = jnp.max(jnp.maximum(s[j, h][:, :SPAN], s[j, h][:, SPAN:]), axis=-1, keepdims=True)
            for j, h in work:
                e = jnp.exp2(s[j, h] - m[j, h])
                l[j, h] = jnp.sum(e[:, :SPAN] + e[:, SPAN:], axis=-1, keepdims=True)
                p[j, h] = e.astype(BF16)
            for j, h in work:
                o[j, h] = jnp.dot(p[j, h], vcat_ref[keys(j), cols(h)], preferred_element_type=F32)
            for j in range(ATTN_TILES):
                lse_all = jnp.zeros((SPAN, HEAD_DIM), F32)
                for h in range(HEADS):
                    o_ref[res, rows(j), cols(h)] = o[j, h] / l[j, h]
                    lse_all = jnp.where(lane == h, (m[j, h] + jnp.log2(l[j, h])) * LN2, lse_all)
                lse_ref[res, rows(j), :] = lse_all
            return carry

        lax.fori_loop(0, nq // (ATTN_TILES * SPAN), tiles, 0)


def _attn_prompt(qkv_g, bias_p, g):
    batch, dil, sub, _ = qkv_g.shape
    nq = min(sub, ATTN_ROWS)
    n_res = min(dil, ATTN_ROWS // nq)
    assert n_res == 1 or nq == sub
    blk = (None, n_res, nq, D_MODEL)
    return pl.pallas_call(
        _attn_prompt_kernel,
        grid=(batch, dil // n_res, sub // nq),
        in_specs=[
            pl.BlockSpec(blk, lambda b, r, t: (b, r, t, 0)),
            pl.BlockSpec(blk, lambda b, r, t: (b, r, t, 1)),
            pl.BlockSpec(blk, lambda b, r, t: (b, r, t, 2)),
            pl.BlockSpec((None, 2, HEADS, SPAN, 2 * SPAN), lambda b, r, t: (g, 0, 0, 0, 0)),
        ],
        out_specs=[
            pl.BlockSpec(blk, lambda b, r, t: (b, r, t, 0)),
            pl.BlockSpec((None, n_res, nq, HEAD_DIM), lambda b, r, t: (b, r, t, 0)),
        ],
        out_shape=[
            jax.ShapeDtypeStruct((batch, dil, sub, D_MODEL), F32),
            jax.ShapeDtypeStruct((batch, dil, sub, HEAD_DIM), F32),
        ],
        scratch_shapes=[pltpu.VMEM((SPAN + nq, D_MODEL), BF16), pltpu.VMEM((SPAN + nq, D_MODEL), BF16)],
        compiler_params=_params(3),
        name=f"attn_prompt_g{g}",
    )(qkv_g, qkv_g, qkv_g, bias_p)


def _attn_sample_kernel(qkv_ref, k0_ref, v0_ref, k1_ref, v1_ref, k2_ref, v2_ref, bias_ref, a_ref, *, bb):
    kv_refs = ((k0_ref, v0_ref), (k1_ref, v1_ref), (k2_ref, v2_ref))
    for bi in range(bb):
        outs, lses = [], []
        for g, (k_ref, v_ref) in enumerate(kv_refs):
            q, k_new, v_new = qkv_ref[bi, 3 * g], qkv_ref[bi, 3 * g + 1], qkv_ref[bi, 3 * g + 2]
            s = jnp.sum(k_ref[bi] * q[None], axis=-1, keepdims=True) * SCALE + bias_ref[g, :SPAN]
            s_new = jnp.sum(k_new * q, axis=-1, keepdims=True) * SCALE + bias_ref[g, SPAN]
            m = jnp.maximum(jnp.max(s, axis=0), s_new)
            e = jnp.exp(s - m[None])
            e_new = jnp.exp(s_new - m)
            l = jnp.sum(e, axis=0) + e_new
            outs.append((jnp.sum(e * v_ref[bi], axis=0) + e_new * v_new) / l)
            lses.append(m + jnp.log(l))
        top = jnp.maximum(jnp.maximum(lses[0], lses[1]), lses[2])
        ws = [jnp.exp(lse - top) for lse in lses]
        den = ws[0] + ws[1] + ws[2]
        a_ref[bi] = (outs[0] * (ws[0] / den) + outs[1] * (ws[1] / den)) + outs[2] * (ws[2] / den)


def _attn_sample(qkv, caches, layer, bias_s):
    b = qkv.shape[0]
    bb = 4
    in_specs = [pl.BlockSpec((bb,) + qkv.shape[1:], lambda i: (i, 0, 0, 0))]
    args = [qkv]
    for g, (_, dil) in enumerate(DIL_GROUPS):
        cache = caches[g]
        assert cache.shape[3] == SPAN * dil
        view = cache.reshape(cache.shape[:3] + (SPAN, dil, HEADS, HEAD_DIM))
        for kv in range(2):
            in_specs.append(pl.BlockSpec((None, bb, None, SPAN, None, HEADS, HEAD_DIM),
                                         lambda i, kv=kv: (layer, i, kv, 0, 0, 0, 0)))
            args.append(view)
    in_specs.append(pl.BlockSpec(bias_s.shape, lambda i: (0, 0, 0, 0)))
    args.append(bias_s)
    return pl.pallas_call(
        functools.partial(_attn_sample_kernel, bb=bb),
        grid=(b // bb,),
        in_specs=in_specs,
        out_specs=pl.BlockSpec((bb, HEADS, HEAD_DIM), lambda i: (i, 0, 0)),
        out_shape=jax.ShapeDtypeStruct((b, HEADS, HEAD_DIM), F32),
        compiler_params=_params(1),
        name="attn_sample",
    )(*args)


def _combine_kernel(o0_ref, o1_ref, o2_ref, l0_ref, l1_ref, l2_ref, a_ref, os_ref, ls_ref):
    tm = a_ref.shape[0]
    for gi, o_ref, l_ref in ((1, o1_ref, l1_ref), (2, o2_ref, l2_ref)):
        dil = DIL_GROUPS[gi][1]
        rows = tm // dil
        for r in range(dil):
            ls_ref[gi - 1, pl.ds(r, rows, stride=dil), :] = l_ref[r]
            for h in range(HEADS):
                os_ref[gi - 1, h, pl.ds(r, rows, stride=dil), :] = o_ref[r, :, h * HEAD_DIM:(h + 1) * HEAD_DIM]
    l0, l1, l2 = l0_ref[0], ls_ref[0], ls_ref[1]
    m = jnp.maximum(jnp.maximum(l0, l1), l2)
    e0, e1, e2 = jnp.exp(l0 - m), jnp.exp(l1 - m), jnp.exp(l2 - m)
    den = e0 + e1 + e2
    w1, w2 = e1 / den, e2 / den
    for h in range(HEADS):
        sl = slice(h * HEAD_DIM, (h + 1) * HEAD_DIM)
        o0 = o0_ref[0, :, sl]
        o = o0 + (os_ref[0, h] - o0) * w1[:, h:h + 1] + (os_ref[1, h] - o0) * w2[:, h:h + 1]
        a_ref[:, sl] = o.astype(a_ref.dtype)


def _combine(outs, lses, seq, tm):
    batch = outs[0].shape[0]
    tiles_per_seq = seq // tm
    in_specs = []
    for width in (D_MODEL, HEAD_DIM):
        for _, dil in DIL_GROUPS:
            in_specs.append(pl.BlockSpec((None, dil, tm // dil, width),
                                         lambda i: (i // tiles_per_seq, 0, i % tiles_per_seq, 0)))
    return pl.pallas_call(
        _combine_kernel,
        grid=(batch * tiles_per_seq,),
        in_specs=in_specs,
        out_specs=pl.BlockSpec((tm, D_MODEL), lambda i: (i, 0)),
        out_shape=jax.ShapeDtypeStruct((batch * seq, D_MODEL), BF16),
        scratch_shapes=[pltpu.VMEM((N_GROUPS - 1, HEADS, tm, HEAD_DIM), F32),
                        pltpu.VMEM((N_GROUPS - 1, tm, HEAD_DIM), F32)],
        compiler_params=_params(1),
        name="combine_groups",
    )(*outs, *lses)


def _ln_silu(z, g_ref, b_ref):
    mu = jnp.mean(z, axis=-1, keepdims=True)
    zc = z - mu
    var = jnp.mean(zc * zc, axis=-1, keepdims=True)
    y = (zc * lax.rsqrt(var + EPS)) * g_ref[...] + b_ref[...]
    return y * jax.nn.sigmoid(y)


SUBLANES = 8
HALO = 32
CONV_ROWS = 128
CONV_CHAINS = 2


def _conv_prompt_kernel(u_ref, prev_ref, w_ref, bdw_ref, g_ref, b_ref, a_ref, ext_ref, z_ref, *, tiles_per_seq):
    tm = u_ref.shape[0]
    starts_seq = pl.program_id(0) % tiles_per_seq == 0
    n_shifted = tm + HALO - SUBLANES
    for c in range(N_LANE_SLABS):
        lanes = slice(c * LANES, (c + 1) * LANES)
        ext_ref[0, c, :HALO, :] = jnp.where(starts_seq, 0.0, prev_ref[:, lanes])
        ext_ref[0, c, HALO:, :] = u_ref[:, lanes]
        for s in range(1, SUBLANES):
            ext_ref[s, c, :n_shifted, :] = ext_ref[0, c, s:s + n_shifted, :]
    off = HALO - (CONV_WIDTH - 1)

    def strip(idx, carry):
        c = idx % N_LANE_SLABS
        r0 = pl.multiple_of((idx // N_LANE_SLABS) * CONV_ROWS, CONV_ROWS)
        accs = [jnp.zeros((CONV_ROWS, LANES), F32) for _ in range(CONV_CHAINS)]
        for shift in range(SUBLANES):
            taps = [k for k in range(CONV_WIDTH) if (off + k) % SUBLANES == shift]
            steps = [(off + k - shift) // SUBLANES for k in taps]
            n_window = SUBLANES * (steps[-1] - steps[0]) + CONV_ROWS
            window = ext_ref[shift, c, pl.ds(r0 + SUBLANES * steps[0], n_window), :]
            for k, q in zip(taps, steps):
                first = SUBLANES * (q - steps[0])
                accs[k % CONV_CHAINS] = accs[k % CONV_CHAINS] + window[first:first + CONV_ROWS] * w_ref[c, k:k + 1, :]
        z_ref[c, pl.ds(r0, CONV_ROWS), :] = sum(accs[1:], accs[0]) + bdw_ref[c]
        return carry

    lax.fori_loop(0, (tm // CONV_ROWS) * N_LANE_SLABS, strip, 0)
    z = jnp.concatenate([z_ref[c] for c in range(N_LANE_SLABS)], axis=-1)
    a_ref[...] = _ln_silu(z, g_ref, b_ref).astype(a_ref.dtype)


def _conv_prompt(u, w_dw, b_dw, ln_g, ln_b, seq, tm):
    t = u.shape[0]
    row = lambda v: v.reshape(1, D_MODEL)
    vec = pl.BlockSpec((1, D_MODEL), lambda i: (0, 0))
    slabs = lambda v: jnp.transpose(v.reshape(-1, N_LANE_SLABS, LANES), (1, 0, 2))
    return pl.pallas_call(
        functools.partial(_conv_prompt_kernel, tiles_per_seq=seq // tm),
        grid=(t // tm,),
        in_specs=[
            pl.BlockSpec((tm, D_MODEL), lambda i: (i, 0)),
            pl.BlockSpec((HALO, D_MODEL), lambda i: (jnp.maximum(i * (tm // HALO) - 1, 0), 0)),
            pl.BlockSpec((N_LANE_SLABS, CONV_WIDTH, LANES), lambda i: (0, 0, 0)),
            pl.BlockSpec((N_LANE_SLABS, 1, LANES), lambda i: (0, 0, 0)),
            vec, vec,
        ],
        out_specs=pl.BlockSpec((tm, D_MODEL), lambda i: (i, 0)),
        out_shape=jax.ShapeDtypeStruct((t, D_MODEL), BF16),
        scratch_shapes=[pltpu.VMEM((SUBLANES, N_LANE_SLABS, tm + HALO, LANES), F32),
                        pltpu.VMEM((N_LANE_SLABS, tm, LANES), F32)],
        compiler_params=_params(1),
        name="conv_prompt",
    )(u, u, slabs(w_dw), slabs(b_dw), row(ln_g), row(ln_b))


def _conv_sample_kernel(u_ref, st_ref, w_ref, bdw_ref, g_ref, b_ref, a_ref, ns_ref):
    n_state = st_ref.shape[0]
    u = u_ref[...]
    z = u * w_ref[n_state:n_state + 1, :] + bdw_ref[...]
    for k in range(n_state):
        z = z + st_ref[k] * w_ref[k:k + 1, :]
    a_ref[...] = _ln_silu(z, g_ref, b_ref).astype(a_ref.dtype)
    for k in range(n_state - 1):
        ns_ref[k] = st_ref[k + 1]
    ns_ref[n_state - 1] = u


def _conv_sample(u, state, layer, w_dw, b_dw, ln_g, ln_b):
    _, b, n_state, _ = state.shape
    bb = 16
    row = lambda v: v.reshape(1, D_MODEL)
    vec = pl.BlockSpec((1, D_MODEL), lambda i: (0, 0))
    a, new_state = pl.pallas_call(
        _conv_sample_kernel,
        grid=(b // bb,),
        in_specs=[
            pl.BlockSpec((bb, D_MODEL), lambda i: (i, 0)),
            pl.BlockSpec((None, n_state, bb, D_MODEL), lambda i: (layer, 0, i, 0)),
            pl.BlockSpec((CONV_WIDTH, D_MODEL), lambda i: (0, 0)),
            vec, vec, vec,
        ],
        out_specs=[
            pl.BlockSpec((bb, D_MODEL), lambda i: (i, 0)),
            pl.BlockSpec((None, n_state, bb, D_MODEL), lambda i: (0, 0, i, 0)),
        ],
        out_shape=[
            jax.ShapeDtypeStruct((b, D_MODEL), BF16),
            jax.ShapeDtypeStruct((1, n_state, b, D_MODEL), F32),
        ],
        compiler_params=_params(1),
        name="conv_sample",
    )(u, jnp.transpose(state, (0, 2, 1, 3)), w_dw, row(b_dw), row(ln_g), row(ln_b))
    return a, jnp.transpose(new_state, (0, 2, 1, 3))


PROMPT_FFN_ROWS = 512


def kernel(x_prompt, x_sample, cache_kv_w128, cache_kv_w512, cache_kv_w2048, state_conv, c_prompt, c_sample,
           w_mod, b_mod, g_mix, g_ffn, g_final, w_qkv, w_o, rel_bias, w_pw1, b_pw1, w_dw, b_dw, ln_g, ln_b,
           w_pw2, b_pw2, w_gate, w_up, w_down):
    batch, seq, d = x_prompt.shape
    dec_batch = x_sample.shape[0]
    assert d == D_MODEL and x_sample.shape[1] == 1
    caches = (cache_kv_w128, cache_kv_w512, cache_kv_w2048)

    def layers(w):
        w = w.astype(BF16)
        return [_LayerWeight(w, layer) for layer in range(w.shape[0])]

    w_o, w_pw1, w_pw2 = map(layers, (w_o, w_pw1, w_pw2))

    c_all = jnp.concatenate([c_sample, c_prompt, jnp.zeros((8 - batch, d), F32)], axis=0)
    mod_all = _modulation(c_all, w_mod, b_mod)
    bias_p, bias_s = _bias_tables(rel_bias)

    xp = x_prompt.reshape(batch * seq, d)
    xs = x_sample.reshape(dec_batch, d)

    mod_p = _Mod(mod_all, 0, False, seq)
    mod_s = _Mod(mod_all, 0, True, 1)

    qkv_s, w_qkv16 = _qkv_sample(xs, g_mix[0], mod_s, w_qkv, 0, 1536)
    w_qkv0 = _LayerWeight(w_qkv16, 0)
    qkv_s4 = qkv_s.reshape(dec_batch, 3 * N_GROUPS, HEADS, HEAD_DIM)
    bias_s4 = jnp.stack([bias_s[g, :, g * HEADS:(g + 1) * HEADS] for g in range(N_GROUPS)])[..., None]
    a_s = _attn_sample(qkv_s4, caches, 0, bias_s4).reshape(dec_batch, d).astype(BF16)
    xs, w_gate0, w_up0, w_down0 = _mixer_ffn_cast(a_s, w_o[0], None, xs, mod_s, g_ffn[0], w_gate, w_up, w_down, 0,
                                                  name="attn_out_ffn_sample")

    qkv_p = _qkv_prompt(xp, g_mix[0], mod_p, w_qkv0, batch, seq, 1024)

    kv_prompt = [_kv_tail(qkv_p[N_GROUPS], w_qkv0, g, batch, seq, min(window, seq))
                 .reshape(1, batch, 2, min(window, seq), HEADS, HEAD_DIM) for g, (window, _) in enumerate(DIL_GROUPS)]

    outs, lses = zip(*[_attn_prompt(qkv_p[g], bias_p, g) for g in range(N_GROUPS)])
    a_p = _combine(outs, lses, seq, 1024)
    xp = _mixer_ffn(a_p, w_o[0], None, xp, mod_p, g_ffn[0], w_gate0, w_up0, w_down0,
                    tm=PROMPT_FFN_ROWS, name="attn_out_ffn_prompt")

    mod_p = _Mod(mod_all, 1, False, seq)
    mod_s = _Mod(mod_all, 1, True, 1)
    b_pw1_row = b_pw1[0].reshape(1, 2 * d)

    def glu(x, mod, tm, tag):
        return _norm_glu(x, g_mix[1], mod, (0, 1), [(w_pw1[0], 0), (w_pw1[0], 1)],
                                 [(b_pw1_row, 0), (b_pw1_row, 1)],
                                 tm=tm, tn=d, n_cols=d, out_dtype=F32, name=f"glu_{tag}")

    u_s = glu(xs, mod_s, dec_batch, "sample")
    a_s, conv_s = _conv_sample(u_s, state_conv, 0, w_dw[0], b_dw[0], ln_g[0], ln_b[0])
    y_s, w_gate1, w_up1, w_down1 = _mixer_ffn_cast(a_s, w_pw2[0], b_pw2[0], xs, mod_s, g_ffn[1], w_gate, w_up,
                                                   w_down, 1, name="conv_out_ffn_sample", g_final=g_final)

    u_p = glu(xp, mod_p, 1024, "prompt")
    a_p = _conv_prompt(u_p, w_dw[0], b_dw[0], ln_g[0], ln_b[0], seq, 512)
    y_p = _mixer_ffn(a_p, w_pw2[0], b_pw2[0], xp, mod_p, g_ffn[1], w_gate1, w_up1, w_down1,
                     tm=PROMPT_FFN_ROWS, name="conv_out_ffn_prompt", g_final=g_final)

    conv_p = u_p.reshape(batch, seq, d)[:, seq - (CONV_WIDTH - 1):][None]
    kv_sample = [qkv_s4[:, 3 * g + 1:3 * g + 3].reshape(1, dec_batch, 2, 1, HEADS, HEAD_DIM)
                 for g in range(N_GROUPS)]
    return (y_p.reshape(batch, seq, d), y_s.reshape(dec_batch, 1, d),
            kv_prompt[0], kv_prompt[1], kv_prompt[2], conv_p,
            kv_sample[0], kv_sample[1], kv_sample[2], conv_s)
```

```python
import functools
import math

import numpy as np
import jax
import jax.numpy as jnp
from jax import lax
from jax.experimental import pallas as pl
from jax.experimental.pallas import tpu as pltpu

D_MODEL = 1024
DIL_GROUPS = ((128, 1), (512, 4), (2048, 16))
N_GROUPS = len(DIL_GROUPS)
HEADS = 8
HEAD_DIM = 128
SPAN = 128
N_BUCKETS = 32
MAX_DISTANCE = 2048
CONV_WIDTH = 31
FFN_HIDDEN = 2816
EPS = 1e-6
NEG_INF = -1e30
SCALE = HEAD_DIM ** -0.5
LOG2E = math.log2(math.e)
LN2 = math.log(2.0)

F32 = jnp.float32
BF16 = jnp.bfloat16

VMEM_LIMIT_BYTES = 56 * 1024 * 1024


def _params(n_axes, vmem=VMEM_LIMIT_BYTES):
    return pltpu.CompilerParams(dimension_semantics=("arbitrary",) * n_axes, vmem_limit_bytes=vmem)


def _t5_bucket_np(dist):
    max_exact = N_BUCKETS // 2
    n = np.maximum(dist, 1).astype(np.float32)
    large = max_exact + (np.log(n / np.float32(max_exact)) / np.float32(math.log(MAX_DISTANCE / max_exact))
                         * np.float32(N_BUCKETS - max_exact)).astype(np.int32)
    large = np.minimum(large, N_BUCKETS - 1)
    return np.where(dist < max_exact, dist, large).astype(np.int32)


def _bucket_tables():
    qi = np.arange(SPAN)[:, None]
    ki = np.arange(2 * SPAN)[None, :]
    delta = qi + SPAN - ki
    in_band = (delta >= 0) & (delta <= SPAN)
    prompt, sample = [], []
    for _, dilation in DIL_GROUPS:
        b = _t5_bucket_np(np.clip(delta, 0, SPAN) * dilation)
        prompt.append(np.where(in_band, b, -1))
        j = SPAN - np.arange(136)
        sample.append(_t5_bucket_np(np.maximum(j, 0) * dilation)[:, None])
    return np.stack(prompt).astype(np.int32), np.stack(sample).astype(np.int32)


def _bias_kernel(rb_smem, rb_ref, bp_ref, bs_ref, op_ref, os_ref):
    g = pl.program_id(0)
    bp = bp_ref[...]
    in_prev_block = lax.broadcasted_iota(jnp.int32, bp.shape, 1) < SPAN
    for h in range(HEADS):
        acc = jnp.full(bp.shape, NEG_INF, F32)
        for b in range(N_BUCKETS):
            acc = jnp.where(bp == b, rb_smem[b, g * HEADS + h] * LOG2E, acc)
        op_ref[0, h] = acc
        op_ref[1, h] = jnp.where(in_prev_block, NEG_INF, acc)
    bs = bs_ref[...]
    acc = jnp.zeros((bs.shape[0], rb_ref.shape[1]), F32)
    for b in range(N_BUCKETS):
        acc = jnp.where(bs == b, rb_ref[b:b + 1, :], acc)
    os_ref[...] = acc


def _bias_tables(rel_bias):
    bp, bs = _bucket_tables()
    n_cols = rel_bias.shape[1]
    return pl.pallas_call(
        _bias_kernel,
        grid=(N_GROUPS,),
        in_specs=[
            pl.BlockSpec(memory_space=pltpu.SMEM),
            pl.BlockSpec((N_BUCKETS, n_cols), lambda g: (0, 0)),
            pl.BlockSpec((None, SPAN, 2 * SPAN), lambda g: (g, 0, 0)),
            pl.BlockSpec((None, 136, 1), lambda g: (g, 0, 0)),
        ],
        out_specs=[
            pl.BlockSpec((None, 2, HEADS, SPAN, 2 * SPAN), lambda g: (g, 0, 0, 0, 0)),
            pl.BlockSpec((None, 136, n_cols), lambda g: (g, 0, 0)),
        ],
        out_shape=[
            jax.ShapeDtypeStruct((N_GROUPS, 2, HEADS, SPAN, 2 * SPAN), F32),
            jax.ShapeDtypeStruct((N_GROUPS, 136, n_cols), F32),
        ],
        compiler_params=_params(1),
        name="bias_tables",
    )(rel_bias, rel_bias, jnp.asarray(bp), jnp.asarray(bs))


def _mod_kernel(c_ref, w_ref, b_ref, o_ref):
    c = c_ref[...]
    a = (c * jax.nn.sigmoid(c)).astype(BF16)
    o_ref[...] = jnp.dot(a, w_ref[...].astype(BF16), preferred_element_type=F32) + b_ref[...]


def _modulation(c_all, w_mod, b_mod):
    depth, d, n = w_mod.shape
    rows = c_all.shape[0]
    tn = 1536
    return pl.pallas_call(
        _mod_kernel,
        grid=(depth, n // tn),
        in_specs=[
            pl.BlockSpec((rows, d), lambda l, j: (0, 0)),
            pl.BlockSpec((None, d, tn), lambda l, j: (l, 0, j)),
            pl.BlockSpec((None, 1, tn), lambda l, j: (l, 0, j)),
        ],
        out_specs=pl.BlockSpec((None, rows, tn), lambda l, j: (l, 0, j)),
        out_shape=jax.ShapeDtypeStruct((depth, rows, n), F32),
        compiler_params=_params(2),
        name="modulation",
    )(c_all, w_mod, b_mod.reshape(depth, 1, n))


MOD_ROW_BLOCK = 8


def _mod_rows(ref, tiles_per_seq, tile_axis=0):
    rows = ref[...]
    if tiles_per_seq is None:
        return rows
    seq_id = pl.program_id(tile_axis) // tiles_per_seq
    pick = lax.broadcasted_iota(jnp.int32, rows.shape, 0) == seq_id
    return jnp.sum(jnp.where(pick, rows, 0.0), axis=0, keepdims=True)


def _normed_f32(x_ref, g_ref, sc_ref, sh_ref, tiles_per_seq, tile_axis=0):
    x = x_ref[...]
    y = x * lax.rsqrt(jnp.mean(x * x, axis=-1, keepdims=True) + EPS)
    scale = _mod_rows(sc_ref, tiles_per_seq, tile_axis)
    return (y * g_ref[...]) * (1.0 + scale) + _mod_rows(sh_ref, tiles_per_seq, tile_axis)


def _normed(x_ref, g_ref, sc_ref, sh_ref, tiles_per_seq, tile_axis=0):
    return _normed_f32(x_ref, g_ref, sc_ref, sh_ref, tiles_per_seq, tile_axis).astype(BF16)


def _norm_glu_kernel(x_ref, g_ref, sc_ref, sh_ref, wa_ref, wg_ref, ba_ref, bg_ref, o_ref, h_ref, *, tiles_per_seq):
    @pl.when(pl.program_id(1) == 0)
    def _():
        h_ref[...] = _normed(x_ref, g_ref, sc_ref, sh_ref, tiles_per_seq)
    h = h_ref[...]
    a = jnp.dot(h, wa_ref[...], preferred_element_type=F32) + ba_ref[...]
    gate = jnp.dot(h, wg_ref[...], preferred_element_type=F32) + bg_ref[...]
    o_ref[...] = a * jax.nn.sigmoid(gate)


class _LayerWeight:
    def __init__(self, stacked, layer):
        self.arr = stacked
        self.layer = layer

    def spec(self, block, index, **kwargs):
        layer = self.layer
        return pl.BlockSpec((None,) + tuple(block), lambda *i: (layer,) + tuple(index(*i)), **kwargs)


class _Mod:
    def __init__(self, mod_all, layer, per_token, rows_per_seq):
        self.layer = layer
        self.per_token = per_token
        self.rows_per_seq = rows_per_seq
        self.arr = mod_all
        self.first_seq_block = (mod_all.shape[1] - MOD_ROW_BLOCK) // MOD_ROW_BLOCK

    def spec(self, chunk, tm, single_tile=False):
        layer = self.layer
        if self.per_token:
            return pl.BlockSpec((None, tm, D_MODEL), lambda *i: (layer, 0 if single_tile else i[0], chunk))
        block = self.first_seq_block
        return pl.BlockSpec((None, MOD_ROW_BLOCK, D_MODEL), lambda *i: (layer, block, chunk))

    def tiles_per_seq(self, tm):
        return None if self.per_token else self.rows_per_seq // tm


def _norm_glu(x, gain, mod, chunks, weights, biases, *, tm, tn, n_cols, out_dtype, name):
    t = x.shape[0]
    kernel = functools.partial(_norm_glu_kernel, tiles_per_seq=mod.tiles_per_seq(tm))
    chunk_shift, chunk_scale = chunks
    in_specs = [
        pl.BlockSpec((tm, D_MODEL), lambda i, j: (i, 0)),
        pl.BlockSpec((1, D_MODEL), lambda i, j: (0, 0)),
        mod.spec(chunk_scale, tm),
        mod.spec(chunk_shift, tm),
    ]
    args = [x, gain.reshape(1, D_MODEL), mod.arr, mod.arr]
    resident = dict(pipeline_mode=pl.Buffered(1)) if n_cols == tn else {}
    for w, col0 in weights:
        in_specs.append(w.spec((D_MODEL, tn), lambda i, j, col0=col0: (0, col0 + j), **resident))
        args.append(w.arr)
    for b, col0 in biases:
        in_specs.append(pl.BlockSpec((1, tn), lambda i, j, col0=col0: (0, col0 + j)))
        args.append(b)
    return pl.pallas_call(
        kernel,
        grid=(t // tm, n_cols // tn),
        in_specs=in_specs,
        out_specs=pl.BlockSpec((tm, tn), lambda i, j: (i, j)),
        out_shape=jax.ShapeDtypeStruct((t, n_cols), out_dtype),
        scratch_shapes=[pltpu.VMEM((tm, D_MODEL), BF16)],
        compiler_params=_params(2),
        name=name,
    )(*args)


def _norm_mm_cast_kernel(x_ref, g_ref, sc_ref, sh_ref, w_ref, o_ref, w16_ref, h_ref):
    @pl.when(pl.program_id(0) == 0)
    def _():
        h_ref[...] = _normed(x_ref, g_ref, sc_ref, sh_ref, None)
    w = w_ref[...].astype(BF16)
    w16_ref[...] = w
    o_ref[...] = jnp.dot(h_ref[...], w, preferred_element_type=F32)


def _qkv_sample(x, gain, mod, w32, layer, tn):
    rows = x.shape[0]
    n = w32.shape[-1]
    return pl.pallas_call(
        _norm_mm_cast_kernel,
        grid=(n // tn,),
        in_specs=[
            pl.BlockSpec((rows, D_MODEL), lambda j: (0, 0)),
            pl.BlockSpec((1, D_MODEL), lambda j: (0, 0)),
            mod.spec(1, rows, single_tile=True),
            mod.spec(0, rows, single_tile=True),
            pl.BlockSpec((None, D_MODEL, tn), lambda j: (layer, 0, j)),
        ],
        out_specs=[
            pl.BlockSpec((rows, tn), lambda j: (0, j)),
            pl.BlockSpec((None, D_MODEL, tn), lambda j: (0, 0, j)),
        ],
        out_shape=[
            jax.ShapeDtypeStruct((rows, n), F32),
            jax.ShapeDtypeStruct((1, D_MODEL, n), BF16),
        ],
        scratch_shapes=[pltpu.VMEM((rows, D_MODEL), BF16)],
        compiler_params=_params(1),
        name="qkv_sample",
    )(x, gain.reshape(1, D_MODEL), mod.arr, mod.arr, w32)


def _mm_kernel(a_ref, w_ref, o_ref):
    o_ref[...] = jnp.dot(a_ref[...], w_ref[...], preferred_element_type=F32)


def _mm_per_seq_kernel(*refs):
    *a_refs, w_ref, o_ref = refs
    for b, a_ref in enumerate(a_refs):
        o_ref[b] = jnp.dot(a_ref[...], w_ref[...], preferred_element_type=F32)


KV_TAIL_ONE_STEP_ROWS = 512


def _kv_tail(h, w_qkv, g, batch, seq, keep):
    if keep <= KV_TAIL_ONE_STEP_ROWS:
        return pl.pallas_call(
            _mm_per_seq_kernel,
            grid=(2,),
            in_specs=[pl.BlockSpec((keep, D_MODEL), lambda j, b=b: ((b + 1) * (seq // keep) - 1, 0))
                      for b in range(batch)]
            + [w_qkv.spec((D_MODEL, D_MODEL), lambda j: (0, 3 * g + 1 + j))],
            out_specs=pl.BlockSpec((batch, None, keep, D_MODEL), lambda j: (0, j, 0, 0)),
            out_shape=jax.ShapeDtypeStruct((batch, 2, keep, D_MODEL), F32),
            compiler_params=_params(1),
            name=f"kv_tail_g{g}",
        )(*([h] * batch), w_qkv.arr)
    tm = min(keep, 1024)
    per_seq = keep // tm
    first = (seq - keep) // tm
    return pl.pallas_call(
        _mm_kernel,
        grid=(2, batch * per_seq),
        in_specs=[
            pl.BlockSpec((tm, D_MODEL), lambda j, i: ((i // per_seq) * (seq // tm) + first + i % per_seq, 0)),
            w_qkv.spec((D_MODEL, D_MODEL), lambda j, i: (0, 3 * g + 1 + j)),
        ],
        out_specs=pl.BlockSpec((None, None, tm, D_MODEL), lambda j, i: (i // per_seq, j, i % per_seq, 0)),
        out_shape=jax.ShapeDtypeStruct((batch, 2, keep, D_MODEL), F32),
        compiler_params=_params(2),
        name=f"kv_tail_g{g}",
    )(h, w_qkv.arr)


LANES = 128
N_LANE_SLABS = D_MODEL // LANES


def _qkv_prompt_kernel(x_ref, g_ref, sc_ref, sh_ref, w_ref, *rest, dil, tiles_per_seq):
    tm = x_ref.shape[0]
    quarter, sixteenth = tm // 4, tm // 16
    if dil == 1:
        o_ref, hn_ref, h_ref = rest
        h = _normed(x_ref, g_ref, sc_ref, sh_ref, tiles_per_seq)
        hn_ref[...] = h
        h_ref[...] = h
    else:
        o_ref, h_ref, slab_ref = rest
        h = _normed_f32(x_ref, g_ref, sc_ref, sh_ref, tiles_per_seq)
        for c in range(N_LANE_SLABS):
            lanes = slice(c * LANES, (c + 1) * LANES)
            slab_ref[0, c] = h[:, lanes]
            for b in range(4):
                part = slab_ref[0, c, pl.ds(b, quarter, stride=4), :]
                if dil == 4:
                    h_ref[b * quarter:(b + 1) * quarter, lanes] = part.astype(BF16)
                else:
                    slab_ref[1, c, b * quarter:(b + 1) * quarter, :] = part
            if dil == 16:
                for b in range(4):
                    for a in range(4):
                        r = 4 * a + b
                        part = slab_ref[1, c, pl.ds(b * quarter + a, sixteenth, stride=4), :]
                        h_ref[r * sixteenth:(r + 1) * sixteenth, lanes] = part.astype(BF16)
    rows = tm // dil
    for c in range(3):
        cols = slice(c * D_MODEL, (c + 1) * D_MODEL)
        acc = jnp.dot(h_ref[...], w_ref[:, cols], preferred_element_type=F32).astype(BF16)
        for r in range(dil):
            o_ref[r, :, cols] = acc[r * rows:(r + 1) * rows, :]


def _qkv_prompt(x, gain, mod, w, batch, seq, tm):
    assert [dil for _, dil in DIL_GROUPS] == [1, 4, 16]
    tiles_per_seq = seq // tm
    results = []
    for g, (_, dil) in enumerate(DIL_GROUPS):
        out_specs = [pl.BlockSpec((None, dil, tm // dil, 3 * D_MODEL),
                                  lambda i: (i // tiles_per_seq, 0, i % tiles_per_seq, 0))]
        out_shapes = [jax.ShapeDtypeStruct((batch, dil, seq // dil, 3 * D_MODEL), BF16)]
        scratch = [pltpu.VMEM((tm, D_MODEL), BF16)]
        if dil == 1:
            out_specs.append(pl.BlockSpec((tm, D_MODEL), lambda i: (i, 0)))
            out_shapes.append(jax.ShapeDtypeStruct((batch * seq, D_MODEL), BF16))
        else:
            scratch.append(pltpu.VMEM((2, N_LANE_SLABS, tm, LANES), F32))
        results.append(pl.pallas_call(
            functools.partial(_qkv_prompt_kernel, dil=dil, tiles_per_seq=tiles_per_seq),
            grid=(batch * tiles_per_seq,),
            in_specs=[
                pl.BlockSpec((tm, D_MODEL), lambda i: (i, 0)),
                pl.BlockSpec((1, D_MODEL), lambda i: (0, 0)),
                mod.spec(1, tm),
                mod.spec(0, tm),
                w.spec((D_MODEL, 3 * D_MODEL), lambda i, g=g: (0, g), pipeline_mode=pl.Buffered(1)),
            ],
            out_specs=out_specs,
            out_shape=out_shapes,
            scratch_shapes=scratch,
            compiler_params=_params(1),
            name=f"qkv_prompt_g{g}",
        )(x, gain.reshape(1, D_MODEL), mod.arr, mod.arr, w.arr))
    (qkv0, h_tokens), (qkv1,), (qkv2,) = results
    return qkv0, qkv1, qkv2, h_tokens


def _mixer_ffn_kernel(*refs, has_bias, final_norm, tiles_per_seq):
    it = iter(refs)
    a_ref, wo_ref = next(it), next(it)
    bo_ref = next(it) if has_bias else None
    x_ref, g1_ref, gffn_ref, sc2_ref, sh2_ref, g2_ref = (next(it) for _ in range(6))
    wg_ref, wu_ref, wd_ref = next(it), next(it), next(it)
    gf_ref = next(it) if final_norm else None
    o_ref = next(it)
    rows = lambda ref: _mod_rows(ref, tiles_per_seq)

    out = jnp.dot(a_ref[...], wo_ref[...], preferred_element_type=F32)
    if has_bias:
        out = out + bo_ref[...]
    x = x_ref[...] + rows(g1_ref) * out
    y = x * lax.rsqrt(jnp.mean(x * x, axis=-1, keepdims=True) + EPS)
    h = ((y * gffn_ref[...]) * (1.0 + rows(sc2_ref)) + rows(sh2_ref)).astype(BF16)
    gate = jnp.dot(h, wg_ref[...], preferred_element_type=F32)
    up = jnp.dot(h, wu_ref[...], preferred_element_type=F32)
    u = ((gate * jax.nn.sigmoid(gate)) * up).astype(BF16)
    x = x + rows(g2_ref) * jnp.dot(u, wd_ref[...], preferred_element_type=F32)
    if final_norm:
        x = (x * lax.rsqrt(jnp.mean(x * x, axis=-1, keepdims=True) + EPS)) * gf_ref[...]
    o_ref[...] = x


def _mixer_ffn(a, w_out, b_out, x, mod, g_ffn, w_gate, w_up, w_down, *, tm, name, g_final=None):
    t, k = a.shape
    tile = pl.BlockSpec((tm, D_MODEL), lambda i: (i, 0))
    vec = pl.BlockSpec((1, D_MODEL), lambda i: (0, 0))
    once = dict(pipeline_mode=pl.Buffered(1))
    in_specs = [pl.BlockSpec((tm, k), lambda i: (i, 0)), w_out.spec((k, D_MODEL), lambda i: (0, 0), **once)]
    args = [a, w_out.arr]
    if b_out is not None:
        in_specs.append(vec)
        args.append(b_out.reshape(1, D_MODEL))
    in_specs += [tile, mod.spec(2, tm), vec, mod.spec(4, tm), mod.spec(3, tm), mod.spec(5, tm),
                 w_gate.spec((D_MODEL, FFN_HIDDEN), lambda i: (0, 0), **once),
                 w_up.spec((D_MODEL, FFN_HIDDEN), lambda i: (0, 0), **once),
                 w_down.spec((FFN_HIDDEN, D_MODEL), lambda i: (0, 0), **once)]
    args += [x, mod.arr, g_ffn.reshape(1, D_MODEL), mod.arr, mod.arr, mod.arr, w_gate.arr, w_up.arr, w_down.arr]
    if g_final is not None:
        in_specs.append(vec)
        args.append(g_final.reshape(1, D_MODEL))
    return pl.pallas_call(
        functools.partial(_mixer_ffn_kernel, has_bias=b_out is not None, final_norm=g_final is not None,
                          tiles_per_seq=mod.tiles_per_seq(tm)),
        grid=(t // tm,),
        in_specs=in_specs,
        out_specs=tile,
        out_shape=jax.ShapeDtypeStruct((t, D_MODEL), F32),
        compiler_params=_params(1),
        name=name,
    )(*args)


FFN_CAST_TILE = 256


def _mixer_ffn_cast_kernel(*refs, has_bias, final_norm):
    it = iter(refs)
    a_ref, wo_ref = next(it), next(it)
    bo_ref = next(it) if has_bias else None
    x_ref, g1_ref, gffn_ref, sc2_ref, sh2_ref, g2_ref = (next(it) for _ in range(6))
    wg_ref, wu_ref, wd_ref = next(it), next(it), next(it)
    gf_ref = next(it) if final_norm else None
    o_ref, wg16_ref, wu16_ref, wd16_ref, x1_ref, h_ref, acc_ref = (next(it) for _ in range(7))
    step = pl.program_id(0)

    @pl.when(step == 0)
    def _():
        out = jnp.dot(a_ref[...], wo_ref[...], preferred_element_type=F32)
        if has_bias:
            out = out + bo_ref[...]
        x = x_ref[...] + g1_ref[...] * out
        y = x * lax.rsqrt(jnp.mean(x * x, axis=-1, keepdims=True) + EPS)
        x1_ref[...] = x
        h_ref[...] = ((y * gffn_ref[...]) * (1.0 + sc2_ref[...]) + sh2_ref[...]).astype(BF16)
        acc_ref[...] = jnp.zeros(acc_ref.shape, F32)

    wg, wu, wd = wg_ref[...].astype(BF16), wu_ref[...].astype(BF16), wd_ref[...].astype(BF16)
    wg16_ref[...] = wg
    wu16_ref[...] = wu
    wd16_ref[...] = wd
    h = h_ref[...]
    gate = jnp.dot(h, wg, preferred_element_type=F32)
    up = jnp.dot(h, wu, preferred_element_type=F32)
    u = ((gate * jax.nn.sigmoid(gate)) * up).astype(BF16)
    acc_ref[...] += jnp.dot(u, wd, preferred_element_type=F32)

    @pl.when(step == pl.num_programs(0) - 1)
    def _():
        x = x1_ref[...] + g2_ref[...] * acc_ref[...]
        if final_norm:
            x = (x * lax.rsqrt(jnp.mean(x * x, axis=-1, keepdims=True) + EPS)) * gf_ref[...]
        o_ref[...] = x


def _mixer_ffn_cast(a, w_out, b_out, x, mod, g_ffn, w_gate32, w_up32, w_down32, layer, *, name, g_final=None):
    rows, k = a.shape
    tf = FFN_CAST_TILE
    tile = pl.BlockSpec((rows, D_MODEL), lambda f: (0, 0))
    vec = pl.BlockSpec((1, D_MODEL), lambda f: (0, 0))
    mod_spec = lambda chunk: mod.spec(chunk, rows, single_tile=True)
    in_specs = [pl.BlockSpec((rows, k), lambda f: (0, 0)),
                w_out.spec((k, D_MODEL), lambda f: (0, 0), pipeline_mode=pl.Buffered(1))]
    args = [a, w_out.arr]
    if b_out is not None:
        in_specs.append(vec)
        args.append(b_out.reshape(1, D_MODEL))
    in_specs += [tile, mod_spec(2), vec, mod_spec(4), mod_spec(3), mod_spec(5),
                 pl.BlockSpec((None, D_MODEL, tf), lambda f: (layer, 0, f)),
                 pl.BlockSpec((None, D_MODEL, tf), lambda f: (layer, 0, f)),
                 pl.BlockSpec((None, tf, D_MODEL), lambda f: (layer, f, 0))]
    args += [x, mod.arr, g_ffn.reshape(1, D_MODEL), mod.arr, mod.arr, mod.arr, w_gate32, w_up32, w_down32]
    if g_final is not None:
        in_specs.append(vec)
        args.append(g_final.reshape(1, D_MODEL))
    y, wg16, wu16, wd16 = pl.pallas_call(
        functools.partial(_mixer_ffn_cast_kernel, has_bias=b_out is not None, final_norm=g_final is not None),
        grid=(FFN_HIDDEN // tf,),
        in_specs=in_specs,
        out_specs=[tile,
                   pl.BlockSpec((None, D_MODEL, tf), lambda f: (0, 0, f)),
                   pl.BlockSpec((None, D_MODEL, tf), lambda f: (0, 0, f)),
                   pl.BlockSpec((None, tf, D_MODEL), lambda f: (0, f, 0))],
        out_shape=[jax.ShapeDtypeStruct((rows, D_MODEL), F32),
                   jax.ShapeDtypeStruct((1, D_MODEL, FFN_HIDDEN), BF16),
                   jax.ShapeDtypeStruct((1, D_MODEL, FFN_HIDDEN), BF16),
                   jax.ShapeDtypeStruct((1, FFN_HIDDEN, D_MODEL), BF16)],
        scratch_shapes=[pltpu.VMEM((rows, D_MODEL), F32), pltpu.VMEM((rows, D_MODEL), BF16),
                        pltpu.VMEM((rows, D_MODEL), F32)],
        compiler_params=_params(1),
        name=name,
    )(*args)
    return y, _LayerWeight(wg16, 0), _LayerWeight(wu16, 0), _LayerWeight(wd16, 0)


ATTN_TILES = 2
ATTN_ROWS = 2048


def _attn_prompt_kernel(q_ref, k_ref, v_ref, bias_ref, o_ref, lse_ref, kcat_ref, vcat_ref):
    step = pl.program_id(2)
    n_res, nq, _ = q_ref.shape
    lane = lax.broadcasted_iota(jnp.int32, (SPAN, HEAD_DIM), 1)
    dims = (((1,), (1,)), ((), ()))
    work = [(j, h) for j in range(ATTN_TILES) for h in range(HEADS)]
    cols = lambda h: slice(h * HEAD_DIM, (h + 1) * HEAD_DIM)

    for res in range(n_res):
        @pl.when(step == 0)
        def _():
            kcat_ref[:SPAN] = jnp.zeros((SPAN, D_MODEL), BF16)
            vcat_ref[:SPAN] = jnp.zeros((SPAN, D_MODEL), BF16)

        @pl.when(step > 0)
        def _():
            kcat_ref[:SPAN] = kcat_ref[nq:]
            vcat_ref[:SPAN] = vcat_ref[nq:]

        kcat_ref[SPAN:] = k_ref[res]
        vcat_ref[SPAN:] = v_ref[res]

        def tiles(it, carry, res=res):
            rows = lambda j: pl.ds(pl.multiple_of((it * ATTN_TILES + j) * SPAN, SPAN), SPAN)
            keys = lambda j: pl.ds(pl.multiple_of((it * ATTN_TILES + j) * SPAN, SPAN), 2 * SPAN)
            starts_seq = jnp.where((step == 0) & (it == 0), 1, 0)
            s, m, p, l, o = {}, {}, {}, {}, {}
            for j, h in work:
                bias = bias_ref[starts_seq if j == 0 else 0, h]
                qk = lax.dot_general(q_ref[res, rows(j), cols(h)], kcat_ref[keys(j), cols(h)], dims,
                                     preferred_element_type=F32)
                s[j, h] = qk * (SCALE * LOG2E) + bias
            for j, h in work:
                m[j, h] = jnp.max(jnp.maximum(s[j, h][:, :SPAN], s[j, h][:, SPAN:]), axis=-1, keepdims=True)
            for j, h in work:
                e = jnp.exp2(s[j, h] - m[j, h])
                l[j, h] = jnp.sum(e[:, :SPAN] + e[:, SPAN:], axis=-1, keepdims=True)
                p[j, h] = e.astype(BF16)
            for j, h in work:
                o[j, h] = jnp.dot(p[j, h], vcat_ref[keys(j), cols(h)], preferred_element_type=F32)
            for j in range(ATTN_TILES):
                lse_all = jnp.zeros((SPAN, HEAD_DIM), F32)
                for h in range(HEADS):
                    o_ref[res, rows(j), cols(h)] = o[j, h] / l[j, h]
                    lse_all = jnp.where(lane == h, (m[j, h] + jnp.log2(l[j, h])) * LN2, lse_all)
                lse_ref[res, rows(j), :] = lse_all
            return carry

        lax.fori_loop(0, nq // (ATTN_TILES * SPAN), tiles, 0)


def _attn_prompt(qkv_g, bias_p, g):
    batch, dil, sub, _ = qkv_g.shape
    nq = min(sub, ATTN_ROWS)
    n_res = min(dil, ATTN_ROWS // nq)
    assert n_res == 1 or nq == sub
    blk = (None, n_res, nq, D_MODEL)
    return pl.pallas_call(
        _attn_prompt_kernel,
        grid=(batch, dil // n_res, sub // nq),
        in_specs=[
            pl.BlockSpec(blk, lambda b, r, t: (b, r, t, 0)),
            pl.BlockSpec(blk, lambda b, r, t: (b, r, t, 1)),
            pl.BlockSpec(blk, lambda b, r, t: (b, r, t, 2)),
            pl.BlockSpec((None, 2, HEADS, SPAN, 2 * SPAN), lambda b, r, t: (g, 0, 0, 0, 0)),
        ],
        out_specs=[
            pl.BlockSpec(blk, lambda b, r, t: (b, r, t, 0)),
            pl.BlockSpec((None, n_res, nq, HEAD_DIM), lambda b, r, t: (b, r, t, 0)),
        ],
        out_shape=[
            jax.ShapeDtypeStruct((batch, dil, sub, D_MODEL), F32),
            jax.ShapeDtypeStruct((batch, dil, sub, HEAD_DIM), F32),
        ],
        scratch_shapes=[pltpu.VMEM((SPAN + nq, D_MODEL), BF16), pltpu.VMEM((SPAN + nq, D_MODEL), BF16)],
        compiler_params=_params(3),
        name=f"attn_prompt_g{g}",
    )(qkv_g, qkv_g, qkv_g, bias_p)


def _attn_sample_kernel(qkv_ref, k0_ref, v0_ref, k1_ref, v1_ref, k2_ref, v2_ref, bias_ref, a_ref, *, bb):
    kv_refs = ((k0_ref, v0_ref), (k1_ref, v1_ref), (k2_ref, v2_ref))
    for bi in range(bb):
        outs, lses = [], []
        for g, (k_ref, v_ref) in enumerate(kv_refs):
            q, k_new, v_new = qkv_ref[bi, 3 * g], qkv_ref[bi, 3 * g + 1], qkv_ref[bi, 3 * g + 2]
            s = jnp.sum(k_ref[bi] * q[None], axis=-1, keepdims=True) * SCALE + bias_ref[g, :SPAN]
            s_new = jnp.sum(k_new * q, axis=-1, keepdims=True) * SCALE + bias_ref[g, SPAN]
            m = jnp.maximum(jnp.max(s, axis=0), s_new)
            e = jnp.exp(s - m[None])
            e_new = jnp.exp(s_new - m)
            l = jnp.sum(e, axis=0) + e_new
            outs.append((jnp.sum(e * v_ref[bi], axis=0) + e_new * v_new) / l)
            lses.append(m + jnp.log(l))
        top = jnp.maximum(jnp.maximum(lses[0], lses[1]), lses[2])
        ws = [jnp.exp(lse - top) for lse in lses]
        den = ws[0] + ws[1] + ws[2]
        a_ref[bi] = (outs[0] * (ws[0] / den) + outs[1] * (ws[1] / den)) + outs[2] * (ws[2] / den)


def _attn_sample(qkv, caches, layer, bias_s):
    b = qkv.shape[0]
    bb = 4
    in_specs = [pl.BlockSpec((bb,) + qkv.shape[1:], lambda i: (i, 0, 0, 0))]
    args = [qkv]
    for g, (_, dil) in enumerate(DIL_GROUPS):
        cache = caches[g]
        assert cache.shape[3] == SPAN * dil
        view = cache.reshape(cache.shape[:3] + (SPAN, dil, HEADS, HEAD_DIM))
        for kv in range(2):
            in_specs.append(pl.BlockSpec((None, bb, None, SPAN, None, HEADS, HEAD_DIM),
                                         lambda i, kv=kv: (layer, i, kv, 0, 0, 0, 0)))
            args.append(view)
    in_specs.append(pl.BlockSpec(bias_s.shape, lambda i: (0, 0, 0, 0)))
    args.append(bias_s)
    return pl.pallas_call(
        functools.partial(_attn_sample_kernel, bb=bb),
        grid=(b // bb,),
        in_specs=in_specs,
        out_specs=pl.BlockSpec((bb, HEADS, HEAD_DIM), lambda i: (i, 0, 0)),
        out_shape=jax.ShapeDtypeStruct((b, HEADS, HEAD_DIM), F32),
        compiler_params=_params(1),
        name="attn_sample",
    )(*args)


def _combine_kernel(o0_ref, o1_ref, o2_ref, l0_ref, l1_ref, l2_ref, a_ref, os_ref, ls_ref):
    tm = a_ref.shape[0]
    for gi, o_ref, l_ref in ((1, o1_ref, l1_ref), (2, o2_ref, l2_ref)):
        dil = DIL_GROUPS[gi][1]
        rows = tm // dil
        for r in range(dil):
            ls_ref[gi - 1, pl.ds(r, rows, stride=dil), :] = l_ref[r]
            for h in range(HEADS):
                os_ref[gi - 1, h, pl.ds(r, rows, stride=dil), :] = o_ref[r, :, h * HEAD_DIM:(h + 1) * HEAD_DIM]
    l0, l1, l2 = l0_ref[0], ls_ref[0], ls_ref[1]
    m = jnp.maximum(jnp.maximum(l0, l1), l2)
    e0, e1, e2 = jnp.exp(l0 - m), jnp.exp(l1 - m), jnp.exp(l2 - m)
    den = e0 + e1 + e2
    w1, w2 = e1 / den, e2 / den
    for h in range(HEADS):
        sl = slice(h * HEAD_DIM, (h + 1) * HEAD_DIM)
        o0 = o0_ref[0, :, sl]
        o = o0 + (os_ref[0, h] - o0) * w1[:, h:h + 1] + (os_ref[1, h] - o0) * w2[:, h:h + 1]
        a_ref[:, sl] = o.astype(a_ref.dtype)


def _combine(outs, lses, seq, tm):
    batch = outs[0].shape[0]
    tiles_per_seq = seq // tm
    in_specs = []
    for width in (D_MODEL, HEAD_DIM):
        for _, dil in DIL_GROUPS:
            in_specs.append(pl.BlockSpec((None, dil, tm // dil, width),
                                         lambda i: (i // tiles_per_seq, 0, i % tiles_per_seq, 0)))
    return pl.pallas_call(
        _combine_kernel,
        grid=(batch * tiles_per_seq,),
        in_specs=in_specs,
        out_specs=pl.BlockSpec((tm, D_MODEL), lambda i: (i, 0)),
        out_shape=jax.ShapeDtypeStruct((batch * seq, D_MODEL), BF16),
        scratch_shapes=[pltpu.VMEM((N_GROUPS - 1, HEADS, tm, HEAD_DIM), F32),
                        pltpu.VMEM((N_GROUPS - 1, tm, HEAD_DIM), F32)],
        compiler_params=_params(1),
        name="combine_groups",
    )(*outs, *lses)


def _ln_silu(z, g_ref, b_ref):
    mu = jnp.mean(z, axis=-1, keepdims=True)
    zc = z - mu
    var = jnp.mean(zc * zc, axis=-1, keepdims=True)
    y = (zc * lax.rsqrt(var + EPS)) * g_ref[...] + b_ref[...]
    return y * jax.nn.sigmoid(y)


SUBLANES = 8
HALO = 32
CONV_ROWS = 128
CONV_CHAINS = 2


def _conv_prompt_kernel(u_ref, prev_ref, w_ref, bdw_ref, g_ref, b_ref, a_ref, ext_ref, z_ref, *, tiles_per_seq):
    tm = u_ref.shape[0]
    starts_seq = pl.program_id(0) % tiles_per_seq == 0
    n_shifted = tm + HALO - SUBLANES
    for c in range(N_LANE_SLABS):
        lanes = slice(c * LANES, (c + 1) * LANES)
        ext_ref[0, c, :HALO, :] = jnp.where(starts_seq, 0.0, prev_ref[:, lanes])
        ext_ref[0, c, HALO:, :] = u_ref[:, lanes]
        for s in range(1, SUBLANES):
            ext_ref[s, c, :n_shifted, :] = ext_ref[0, c, s:s + n_shifted, :]
    off = HALO - (CONV_WIDTH - 1)

    def strip(idx, carry):
        c = idx % N_LANE_SLABS
        r0 = pl.multiple_of((idx // N_LANE_SLABS) * CONV_ROWS, CONV_ROWS)
        accs = [jnp.zeros((CONV_ROWS, LANES), F32) for _ in range(CONV_CHAINS)]
        for shift in range(SUBLANES):
            taps = [k for k in range(CONV_WIDTH) if (off + k) % SUBLANES == shift]
            steps = [(off + k - shift) // SUBLANES for k in taps]
            n_window = SUBLANES * (steps[-1] - steps[0]) + CONV_ROWS
            window = ext_ref[shift, c, pl.ds(r0 + SUBLANES * steps[0], n_window), :]
            for k, q in zip(taps, steps):
                first = SUBLANES * (q - steps[0])
                accs[k % CONV_CHAINS] = accs[k % CONV_CHAINS] + window[first:first + CONV_ROWS] * w_ref[c, k:k + 1, :]
        z_ref[c, pl.ds(r0, CONV_ROWS), :] = sum(accs[1:], accs[0]) + bdw_ref[c]
        return carry

    lax.fori_loop(0, (tm // CONV_ROWS) * N_LANE_SLABS, strip, 0)
    z = jnp.concatenate([z_ref[c] for c in range(N_LANE_SLABS)], axis=-1)
    a_ref[...] = _ln_silu(z, g_ref, b_ref).astype(a_ref.dtype)


def _conv_prompt(u, w_dw, b_dw, ln_g, ln_b, seq, tm):
    t = u.shape[0]
    row = lambda v: v.reshape(1, D_MODEL)
    vec = pl.BlockSpec((1, D_MODEL), lambda i: (0, 0))
    slabs = lambda v: jnp.transpose(v.reshape(-1, N_LANE_SLABS, LANES), (1, 0, 2))
    return pl.pallas_call(
        functools.partial(_conv_prompt_kernel, tiles_per_seq=seq // tm),
        grid=(t // tm,),
        in_specs=[
            pl.BlockSpec((tm, D_MODEL), lambda i: (i, 0)),
            pl.BlockSpec((HALO, D_MODEL), lambda i: (jnp.maximum(i * (tm // HALO) - 1, 0), 0)),
            pl.BlockSpec((N_LANE_SLABS, CONV_WIDTH, LANES), lambda i: (0, 0, 0)),
            pl.BlockSpec((N_LANE_SLABS, 1, LANES), lambda i: (0, 0, 0)),
            vec, vec,
        ],
        out_specs=pl.BlockSpec((tm, D_MODEL), lambda i: (i, 0)),
        out_shape=jax.ShapeDtypeStruct((t, D_MODEL), BF16),
        scratch_shapes=[pltpu.VMEM((SUBLANES, N_LANE_SLABS, tm + HALO, LANES), F32),
                        pltpu.VMEM((N_LANE_SLABS, tm, LANES), F32)],
        compiler_params=_params(1),
        name="conv_prompt",
    )(u, u, slabs(w_dw), slabs(b_dw), row(ln_g), row(ln_b))


def _conv_sample_kernel(u_ref, st_ref, w_ref, bdw_ref, g_ref, b_ref, a_ref, ns_ref):
    n_state = st_ref.shape[0]
    u = u_ref[...]
    z = u * w_ref[n_state:n_state + 1, :] + bdw_ref[...]
    for k in range(n_state):
        z = z + st_ref[k] * w_ref[k:k + 1, :]
    a_ref[...] = _ln_silu(z, g_ref, b_ref).astype(a_ref.dtype)
    for k in range(n_state - 1):
        ns_ref[k] = st_ref[k + 1]
    ns_ref[n_state - 1] = u


def _conv_sample(u, state, layer, w_dw, b_dw, ln_g, ln_b):
    _, b, n_state, _ = state.shape
    bb = 16
    row = lambda v: v.reshape(1, D_MODEL)
    vec = pl.BlockSpec((1, D_MODEL), lambda i: (0, 0))
    a, new_state = pl.pallas_call(
        _conv_sample_kernel,
        grid=(b // bb,),
        in_specs=[
            pl.BlockSpec((bb, D_MODEL), lambda i: (i, 0)),
            pl.BlockSpec((None, n_state, bb, D_MODEL), lambda i: (layer, 0, i, 0)),
            pl.BlockSpec((CONV_WIDTH, D_MODEL), lambda i: (0, 0)),
            vec, vec, vec,
        ],
        out_specs=[
            pl.BlockSpec((bb, D_MODEL), lambda i: (i, 0)),
            pl.BlockSpec((None, n_state, bb, D_MODEL), lambda i: (0, 0, i, 0)),
        ],
        out_shape=[
            jax.ShapeDtypeStruct((b, D_MODEL), BF16),
            jax.ShapeDtypeStruct((1, n_state, b, D_MODEL), F32),
        ],
        compiler_params=_params(1),
        name="conv_sample",
    )(u, jnp.transpose(state, (0, 2, 1, 3)), w_dw, row(b_dw), row(ln_g), row(ln_b))
    return a, jnp.transpose(new_state, (0, 2, 1, 3))


PROMPT_FFN_ROWS = 512


def kernel(x_prompt, x_sample, cache_kv_w128, cache_kv_w512, cache_kv_w2048, state_conv, c_prompt, c_sample,
           w_mod, b_mod, g_mix, g_ffn, g_final, w_qkv, w_o, rel_bias, w_pw1, b_pw1, w_dw, b_dw, ln_g, ln_b,
           w_pw2, b_pw2, w_gate, w_up, w_down):
    batch, seq, d = x_prompt.shape
    dec_batch = x_sample.shape[0]
    assert d == D_MODEL and x_sample.shape[1] == 1
    caches = (cache_kv_w128, cache_kv_w512, cache_kv_w2048)

    def layers(w):
        w = w.astype(BF16)
        return [_LayerWeight(w, layer) for layer in range(w.shape[0])]

    w_o, w_pw1, w_pw2 = map(layers, (w_o, w_pw1, w_pw2))

    c_all = jnp.concatenate([c_sample, c_prompt, jnp.zeros((8 - batch, d), F32)], axis=0)
    mod_all = _modulation(c_all, w_mod, b_mod)
    bias_p, bias_s = _bias_tables(rel_bias)

    xp = x_prompt.reshape(batch * seq, d)
    xs = x_sample.reshape(dec_batch, d)

    mod_p = _Mod(mod_all, 0, False, seq)
    mod_s = _Mod(mod_all, 0, True, 1)

    qkv_s, w_qkv16 = _qkv_sample(xs, g_mix[0], mod_s, w_qkv, 0, 1536)
    w_qkv0 = _LayerWeight(w_qkv16, 0)
    qkv_s4 = qkv_s.reshape(dec_batch, 3 * N_GROUPS, HEADS, HEAD_DIM)
    bias_s4 = jnp.stack([bias_s[g, :, g * HEADS:(g + 1) * HEADS] for g in range(N_GROUPS)])[..., None]
    a_s = _attn_sample(qkv_s4, caches, 0, bias_s4).reshape(dec_batch, d).astype(BF16)
    xs, w_gate0, w_up0, w_down0 = _mixer_ffn_cast(a_s, w_o[0], None, xs, mod_s, g_ffn[0], w_gate, w_up, w_down, 0,
                                                  name="attn_out_ffn_sample")

    qkv_p = _qkv_prompt(xp, g_mix[0], mod_p, w_qkv0, batch, seq, 1024)

    kv_prompt = [_kv_tail(qkv_p[N_GROUPS], w_qkv0, g, batch, seq, min(window, seq))
                 .reshape(1, batch, 2, min(window, seq), HEADS, HEAD_DIM) for g, (window, _) in enumerate(DIL_GROUPS)]

    outs, lses = zip(*[_attn_prompt(qkv_p[g], bias_p, g) for g in range(N_GROUPS)])
    a_p = _combine(outs, lses, seq, 1024)
    xp = _mixer_ffn(a_p, w_o[0], None, xp, mod_p, g_ffn[0], w_gate0, w_up0, w_down0,
                    tm=PROMPT_FFN_ROWS, name="attn_out_ffn_prompt")

    mod_p = _Mod(mod_all, 1, False, seq)
    mod_s = _Mod(mod_all, 1, True, 1)
    b_pw1_row = b_pw1[0].reshape(1, 2 * d)

    def glu(x, mod, tm, tag):
        return _norm_glu(x, g_mix[1], mod, (0, 1), [(w_pw1[0], 0), (w_pw1[0], 1)],
                                 [(b_pw1_row, 0), (b_pw1_row, 1)],
                                 tm=tm, tn=d, n_cols=d, out_dtype=F32, name=f"glu_{tag}")

    u_s = glu(xs, mod_s, dec_batch, "sample")
    a_s, conv_s = _conv_sample(u_s, state_conv, 0, w_dw[0], b_dw[0], ln_g[0], ln_b[0])
    y_s, w_gate1, w_up1, w_down1 = _mixer_ffn_cast(a_s, w_pw2[0], b_pw2[0], xs, mod_s, g_ffn[1], w_gate, w_up,
                                                   w_down, 1, name="conv_out_ffn_sample", g_final=g_final)

    u_p = glu(xp, mod_p, 1024, "prompt")
    a_p = _conv_prompt(u_p, w_dw[0], b_dw[0], ln_g[0], ln_b[0], seq, 512)
    y_p = _mixer_ffn(a_p, w_pw2[0], b_pw2[0], xp, mod_p, g_ffn[1], w_gate1, w_up1, w_down1,
                     tm=PROMPT_FFN_ROWS, name="conv_out_ffn_prompt", g_final=g_final)

    conv_p = u_p.reshape(batch, seq, d)[:, seq - (CONV_WIDTH - 1):][None]
    kv_sample = [qkv_s4[:, 3 * g + 1:3 * g + 3].reshape(1, dec_batch, 2, 1, HEADS, HEAD_DIM)
                 for g in range(N_GROUPS)]
    return (y_p.reshape(batch, seq, d), y_s.reshape(dec_batch, 1, d),
            kv_prompt[0], kv_prompt[1], kv_prompt[2], conv_p,
            kv_sample[0], kv_sample[1], kv_sample[2], conv_s)
```
